```python
import math
import functools
import jax
import jax.numpy as jnp
from jax import lax
import numpy as np

D_MODEL = 1024
BATCH = 4
SEQ = 4096
DEPTH = 1
DEC_BATCH = 128
DEC_SEQ = 4
PAST_LEN = 2048
PAGE_SIZE = 128

SSM_WIDTH = D_MODEL // 2
SSM_GROUP_CH = 16
SSM_GROUPS = SSM_WIDTH // SSM_GROUP_CH
SSM_STATE = 64
ATTN_WIDTH = D_MODEL - SSM_WIDTH
ATTN_HEAD_DIM = 64
ATTN_V_DIM = 2 * ATTN_HEAD_DIM
ATTN_HEADS = ATTN_WIDTH // ATTN_V_DIM
IN_PROJ_COLS = SSM_WIDTH + 3 * ATTN_WIDTH
Q_BLOCK = 128
N_MEM = 256
CA_HEADS = 4
CA_HEAD_DIM = D_MODEL // CA_HEADS
FFN_HIDDEN = ((8 * D_MODEL // 3 + 127) // 128) * 128
CONV_WIDTH = 3
NORM_EPS = 1e-6

kernel_name = 'hymba_s5_diffattn_convffn_step'


def rms_norm(x, g):
    xf = x.astype(jnp.float32)
    y = xf * lax.rsqrt(jnp.mean(xf * xf, axis=-1, keepdims=True) + NORM_EPS)
    return (y * g.astype(jnp.float32)).astype(x.dtype)


def alibi_slopes(n_heads):
    return 2.0 ** (-8.0 * jnp.arange(1, n_heads + 1, dtype=jnp.float32) / n_heads)


def lambda_init(layer):
    return 0.8 - 0.6 * math.exp(-0.3 * layer)


def _complex_affine_combine(e1, e2):
    a1r, a1i, b1r, b1i = e1
    a2r, a2i, b2r, b2i = e2
    return (a1r * a2r - a1i * a2i,
            a1r * a2i + a1i * a2r,
            a2r * b1r - a2i * b1i + b2r,
            a2r * b1i + a2i * b1r + b2i)


def s5_mixer(u, h0_re, h0_im, p):
    f32 = jnp.float32
    bsz, t, _ = u.shape
    uf = u.astype(f32).reshape(bsz, t, SSM_GROUPS, SSM_GROUP_CH)
    dt = jnp.exp(p['ssm_log_dt'].astype(f32))[:, None]
    a_re = p['ssm_a_re'].astype(f32)
    a_im = p['ssm_a_im'].astype(f32)
    mag = jnp.exp(a_re * dt)
    lb_re = mag * jnp.cos(a_im * dt)
    lb_im = mag * jnp.sin(a_im * dt)
    den = a_re * a_re + a_im * a_im
    n_re = lb_re - 1.0
    f_re = (n_re * a_re + lb_im * a_im) / den
    f_im = (lb_im * a_re - n_re * a_im) / den
    b_re = p['ssm_b_re'].astype(f32)
    b_im = p['ssm_b_im'].astype(f32)
    bb_re = f_re[..., None] * b_re - f_im[..., None] * b_im
    bb_im = f_re[..., None] * b_im + f_im[..., None] * b_re
    bu_re = jnp.einsum('btgc,gpc->btgp', uf, bb_re)
    bu_im = jnp.einsum('btgc,gpc->btgp', uf, bb_im)
    dec_re = jnp.broadcast_to(lb_re, bu_re.shape)
    dec_im = jnp.broadcast_to(lb_im, bu_im.shape)
    acc_re, acc_im, s_re, s_im = lax.associative_scan(
        _complex_affine_combine, (dec_re, dec_im, bu_re, bu_im), axis=1)
    h0r = h0_re.astype(f32)[:, None]
    h0i = h0_im.astype(f32)[:, None]
    h_re = acc_re * h0r - acc_im * h0i + s_re
    h_im = acc_re * h0i + acc_im * h0r + s_im
    y = (jnp.einsum('btgp,gcp->btgc', h_re, p['ssm_c_re'].astype(f32))
         - jnp.einsum('btgp,gcp->btgc', h_im, p['ssm_c_im'].astype(f32))
         + p['ssm_d'].astype(f32) * uf)
    g = jax.nn.gelu(y.reshape(bsz, t, SSM_WIDTH))
    out = g * jax.nn.sigmoid(g @ p['ssm_glu_w'].astype(f32))
    return out.astype(u.dtype), h_re[:, -1], h_im[:, -1]


def diff_attn_core(q, k, v, q_pos, k_pos, lam):
    s = jnp.einsum('bqhid,bkhid->bhiqk', q, k).astype(jnp.float32) * (ATTN_HEAD_DIM ** -0.5)
    rel = q_pos[:, None] - k_pos[None, :]
    bias = -alibi_slopes(ATTN_HEADS)[:, None, None, None] * rel.astype(jnp.float32)
    s = jnp.where(rel >= 0, s + bias, -jnp.inf)
    pr = jax.nn.softmax(s, axis=-1)
    a = pr[:, :, 0] - lam * pr[:, :, 1]
    return jnp.einsum('bhqk,bkhd->bqhd', a.astype(v.dtype), v)


def attend_prompt(q, k, v, lam):
    bsz, t = q.shape[:2]
    nb = t // Q_BLOCK
    pos = jnp.arange(t, dtype=jnp.int32)
    q_blocks = q.reshape((bsz, nb, Q_BLOCK) + q.shape[2:]).swapaxes(0, 1)
    pos_blocks = pos.reshape(nb, Q_BLOCK)
    out = lax.map(lambda qp: diff_attn_core(qp[0], k, v, qp[1], pos, lam), (q_blocks, pos_blocks))
    return out.swapaxes(0, 1).reshape(bsz, t, ATTN_HEADS, ATTN_V_DIM)


def attend_sample(q, k, v, lam, past_k, past_v):
    t = q.shape[1]
    past = past_k.shape[1]
    k_all = jnp.concatenate([past_k.astype(k.dtype), k], axis=1)
    v_all = jnp.concatenate([past_v.astype(v.dtype), v], axis=1)
    k_pos = jnp.arange(past + t, dtype=jnp.int32)
    q_pos = past + jnp.arange(t, dtype=jnp.int32)
    return diff_attn_core(q, k_all, v_all, q_pos, k_pos, lam)


def memory_kv(mem, p):
    bsz, m, _ = mem.shape
    mn = rms_norm(mem, p['mem_norm_g'])
    mk = rms_norm((mn @ p['ca_wk']).reshape(bsz, m, CA_HEADS, CA_HEAD_DIM), p['ca_k_norm_g'])
    mv = (mn @ p['ca_wv']).reshape(bsz, m, CA_HEADS, CA_HEAD_DIM)
    return mk, mv


def layer_forward(x, p, layer, h0_re, h0_im, conv_prev, mem_k, mem_v, attend):
    f32 = jnp.float32
    bsz, t, _ = x.shape
    xn = rms_norm(x, p['ln1_g'])
    proj = xn @ p['w_in']
    u = proj[..., :SSM_WIDTH]
    q = proj[..., SSM_WIDTH:SSM_WIDTH + ATTN_WIDTH].reshape(bsz, t, ATTN_HEADS, 2, ATTN_HEAD_DIM)
    k = proj[..., SSM_WIDTH + ATTN_WIDTH:SSM_WIDTH + 2 * ATTN_WIDTH].reshape(bsz, t, ATTN_HEADS, 2, ATTN_HEAD_DIM)
    v = proj[..., SSM_WIDTH + 2 * ATTN_WIDTH:].reshape(bsz, t, ATTN_HEADS, ATTN_V_DIM)
    ssm_out, h_re, h_im = s5_mixer(u, h0_re, h0_im, p)
    q = rms_norm(q, p['q_norm_g'])
    k = rms_norm(k, p['k_norm_g'])
    lam0 = lambda_init(layer)
    lam = (jnp.exp(jnp.sum(p['lam_q1'].astype(f32) * p['lam_k1'].astype(f32)))
           - jnp.exp(jnp.sum(p['lam_q2'].astype(f32) * p['lam_k2'].astype(f32))) + lam0)
    o = attend(q, k, v, lam)
    o = rms_norm(o, p['subln_g']) * (1.0 - lam0)
    mixed = jnp.concatenate([ssm_out, o.reshape(bsz, t, ATTN_WIDTH)], axis=-1)
    x = x + mixed @ p['w_out']
    xn = rms_norm(x, p['ln2_g'])
    cq = rms_norm((xn @ p['ca_wq']).reshape(bsz, t, CA_HEADS, CA_HEAD_DIM), p['ca_q_norm_g'])
    s = jnp.einsum('bqhd,bmhd->bhqm', cq, mem_k.astype(cq.dtype)).astype(f32) * (CA_HEAD_DIM ** -0.5)
    pr = jax.nn.softmax(s, axis=-1).astype(x.dtype)
    co = jnp.einsum('bhqm,bmhd->bqhd', pr, mem_v.astype(x.dtype)).reshape(bsz, t, D_MODEL)
    x = x + co @ p['ca_wo']
    xn = rms_norm(x, p['ln3_g'])
    hg = xn @ p['ffn_wg']
    buf = jnp.concatenate([conv_prev.astype(hg.dtype), hg], axis=1)
    w = p['ffn_conv_w']
    conv = p['ffn_conv_b'] + w[0] * buf[:, 0:t]
    for j in range(1, CONV_WIDTH):
        conv = conv + w[j] * buf[:, j:j + t]
    x = x + (jax.nn.silu(conv) * (xn @ p['ffn_wv'])) @ p['ffn_wd']
    return x, k, v, h_re, h_im, buf[:, t:]


def setup_inputs(seed: int = 0) -> dict:
    key = jax.random.key(seed)
    keys = jax.random.split(key, 48)
    counter = [0]

    def nxt():
        kk = keys[counter[0]]
        counter[0] += 1
        return kk

    def nrm(shape, scale):
        return scale * jax.random.normal(nxt(), shape, jnp.float32)

    def gain(n):
        return 1.0 + nrm((DEPTH, n), 0.02)

    n_pages = PAST_LEN // PAGE_SIZE
    n_phys = (5 * DEC_BATCH * n_pages + 3) // 4
    a_im0 = math.pi * jnp.arange(SSM_STATE, dtype=jnp.float32)
    return {
        'x_prompt': nrm((BATCH, SEQ, D_MODEL), 1.0),
        'x_sample': nrm((DEC_BATCH, DEC_SEQ, D_MODEL), 1.0),
        'mem_prompt': nrm((BATCH, N_MEM, D_MODEL), 1.0),
        'cache_k': nrm((DEPTH, n_phys, PAGE_SIZE, ATTN_HEADS, 2, ATTN_HEAD_DIM), 1.0),
        'cache_v': nrm((DEPTH, n_phys, PAGE_SIZE, ATTN_HEADS, ATTN_V_DIM), 1.0),
        'page_table': jax.random.permutation(nxt(), n_phys)[:DEC_BATCH * n_pages].reshape(DEC_BATCH, n_pages).astype(jnp.int32),
        'state_ssm_re': nrm((DEPTH, DEC_BATCH, SSM_GROUPS, SSM_STATE), 0.1),
        'state_ssm_im': nrm((DEPTH, DEC_BATCH, SSM_GROUPS, SSM_STATE), 0.1),
        'state_conv': nrm((DEPTH, DEC_BATCH, CONV_WIDTH - 1, FFN_HIDDEN), 1.0),
        'cache_mem_k': nrm((DEPTH, DEC_BATCH, N_MEM, CA_HEADS, CA_HEAD_DIM), 1.0),
        'cache_mem_v': nrm((DEPTH, DEC_BATCH, N_MEM, CA_HEADS, CA_HEAD_DIM), 1.0),
        'ln1_g': gain(D_MODEL),
        'w_in': nrm((DEPTH, D_MODEL, IN_PROJ_COLS), D_MODEL ** -0.5),
        'ssm_a_re': -0.5 + nrm((DEPTH, SSM_GROUPS, SSM_STATE), 0.01),
        'ssm_a_im': a_im0 + nrm((DEPTH, SSM_GROUPS, SSM_STATE), 0.01),
        'ssm_b_re': nrm((DEPTH, SSM_GROUPS, SSM_STATE, SSM_GROUP_CH), (2 * SSM_GROUP_CH) ** -0.5),
        'ssm_b_im': nrm((DEPTH, SSM_GROUPS, SSM_STATE, SSM_GROUP_CH), (2 * SSM_GROUP_CH) ** -0.5),
        'ssm_c_re': nrm((DEPTH, SSM_GROUPS, SSM_GROUP_CH, SSM_STATE), (2 * SSM_STATE) ** -0.5),
        'ssm_c_im': nrm((DEPTH, SSM_GROUPS, SSM_GROUP_CH, SSM_STATE), (2 * SSM_STATE) ** -0.5),
        'ssm_d': nrm((DEPTH, SSM_GROUPS, SSM_GROUP_CH), 1.0),
        'ssm_log_dt': jax.random.uniform(nxt(), (DEPTH, SSM_GROUPS), jnp.float32, math.log(1e-3), math.log(1e-1)),
        'ssm_glu_w': nrm((DEPTH, SSM_WIDTH, SSM_WIDTH), SSM_WIDTH ** -0.5),
        'q_norm_g': gain(ATTN_HEAD_DIM),
        'k_norm_g': gain(ATTN_HEAD_DIM),
        'lam_q1': nrm((DEPTH, ATTN_HEAD_DIM), 0.1),
        'lam_k1': nrm((DEPTH, ATTN_HEAD_DIM), 0.1),
        'lam_q2': nrm((DEPTH, ATTN_HEAD_DIM), 0.1),
        'lam_k2': nrm((DEPTH, ATTN_HEAD_DIM), 0.1),
        'subln_g': gain(ATTN_V_DIM),
        'w_out': nrm((DEPTH, SSM_WIDTH + ATTN_WIDTH, D_MODEL), (SSM_WIDTH + ATTN_WIDTH) ** -0.5),
        'ln2_g': gain(D_MODEL),
        'mem_norm_g': gain(D_MODEL),
        'ca_wq': nrm((DEPTH, D_MODEL, D_MODEL), D_MODEL ** -0.5),
        'ca_wk': nrm((DEPTH, D_MODEL, D_MODEL), D_MODEL ** -0.5),
        'ca_wv': nrm((DEPTH, D_MODEL, D_MODEL), D_MODEL ** -0.5),
        'ca_q_norm_g': gain(CA_HEAD_DIM),
        'ca_k_norm_g': gain(CA_HEAD_DIM),
        'ca_wo': nrm((DEPTH, D_MODEL, D_MODEL), D_MODEL ** -0.5),
        'ln3_g': gain(D_MODEL),
        'ffn_wg': nrm((DEPTH, D_MODEL, FFN_HIDDEN), D_MODEL ** -0.5),
        'ffn_wv': nrm((DEPTH, D_MODEL, FFN_HIDDEN), D_MODEL ** -0.5),
        'ffn_conv_w': nrm((DEPTH, CONV_WIDTH, FFN_HIDDEN), CONV_WIDTH ** -0.5),
        'ffn_conv_b': nrm((DEPTH, FFN_HIDDEN), 0.01),
        'ffn_wd': nrm((DEPTH, FFN_HIDDEN, D_MODEL), FFN_HIDDEN ** -0.5),
    }


def reference(x_prompt, x_sample, mem_prompt, cache_k, cache_v, page_table,
              state_ssm_re, state_ssm_im, state_conv, cache_mem_k, cache_mem_v,
              ln1_g, w_in, ssm_a_re, ssm_a_im, ssm_b_re, ssm_b_im, ssm_c_re, ssm_c_im,
              ssm_d, ssm_log_dt, ssm_glu_w, q_norm_g, k_norm_g, lam_q1, lam_k1, lam_q2, lam_k2,
              subln_g, w_out, ln2_g, mem_norm_g, ca_wq, ca_wk, ca_wv, ca_q_norm_g, ca_k_norm_g,
              ca_wo, ln3_g, ffn_wg, ffn_wv, ffn_conv_w, ffn_conv_b, ffn_wd):
    y_prompt, y_sample = x_prompt, x_sample
    n_prompt, n_dec = x_prompt.shape[0], x_sample.shape[0]
    (k_p, v_p, k_s, v_s, hr_p, hi_p, hr_s, hi_s, c_p, c_s, mk_p, mv_p) = ([] for _ in range(12))
    for l in range(DEPTH):
        p = {
            'ln1_g': ln1_g[l], 'w_in': w_in[l],
            'ssm_a_re': ssm_a_re[l], 'ssm_a_im': ssm_a_im[l],
            'ssm_b_re': ssm_b_re[l], 'ssm_b_im': ssm_b_im[l],
            'ssm_c_re': ssm_c_re[l], 'ssm_c_im': ssm_c_im[l],
            'ssm_d': ssm_d[l], 'ssm_log_dt': ssm_log_dt[l], 'ssm_glu_w': ssm_glu_w[l],
            'q_norm_g': q_norm_g[l], 'k_norm_g': k_norm_g[l],
            'lam_q1': lam_q1[l], 'lam_k1': lam_k1[l], 'lam_q2': lam_q2[l], 'lam_k2': lam_k2[l],
            'subln_g': subln_g[l], 'w_out': w_out[l],
            'ln2_g': ln2_g[l], 'mem_norm_g': mem_norm_g[l],
            'ca_wq': ca_wq[l], 'ca_wk': ca_wk[l], 'ca_wv': ca_wv[l],
            'ca_q_norm_g': ca_q_norm_g[l], 'ca_k_norm_g': ca_k_norm_g[l], 'ca_wo': ca_wo[l],
            'ln3_g': ln3_g[l], 'ffn_wg': ffn_wg[l], 'ffn_wv': ffn_wv[l],
            'ffn_conv_w': ffn_conv_w[l], 'ffn_conv_b': ffn_conv_b[l], 'ffn_wd': ffn_wd[l],
        }
        mem_k, mem_v = memory_kv(mem_prompt, p)
        zeros_h = jnp.zeros((n_prompt, SSM_GROUPS, SSM_STATE), jnp.float32)
        zeros_c = jnp.zeros((n_prompt, CONV_WIDTH - 1, FFN_HIDDEN), x_prompt.dtype)
        y_prompt, kp, vp, hrp, hip, cp = layer_forward(
            y_prompt, p, l, zeros_h, zeros_h, zeros_c, mem_k, mem_v, attend_prompt)
        past_k = cache_k[l][page_table].reshape(n_dec, -1, ATTN_HEADS, 2, ATTN_HEAD_DIM)
        past_v = cache_v[l][page_table].reshape(n_dec, -1, ATTN_HEADS, ATTN_V_DIM)
        attend = functools.partial(attend_sample, past_k=past_k, past_v=past_v)
        y_sample, ks, vs, hrs, his, cs = layer_forward(
            y_sample, p, l, state_ssm_re[l], state_ssm_im[l], state_conv[l],
            cache_mem_k[l], cache_mem_v[l], attend)
        k_p.append(kp)
        v_p.append(vp)
        k_s.append(ks)
        v_s.append(vs)
        hr_p.append(hrp)
        hi_p.append(hip)
        hr_s.append(hrs)
        hi_s.append(his)
        c_p.append(cp)
        c_s.append(cs)
        mk_p.append(mem_k)
        mv_p.append(mem_v)
    return (y_prompt, y_sample,
            jnp.stack(k_p), jnp.stack(v_p), jnp.stack(k_s), jnp.stack(v_s),
            jnp.stack(hr_p), jnp.stack(hi_p), jnp.stack(hr_s), jnp.stack(hi_s),
            jnp.stack(c_p), jnp.stack(c_s), jnp.stack(mk_p), jnp.stack(mv_p))
```

```python
import functools
import math

import jax
import jax.numpy as jnp
from jax import lax
from jax.experimental import pallas as pl
from jax.experimental.pallas import tpu as pltpu

F32 = jnp.float32
BF16 = jnp.bfloat16

D_MODEL = 1024
SSM_WIDTH = 512
SSM_GROUP_CH = 16
SSM_GROUPS = 32
SSM_STATE = 64
N_STATE = SSM_GROUPS * SSM_STATE
ATTN_WIDTH = 512
ATTN_HEAD_DIM = 64
ATTN_V_DIM = 128
ATTN_HEADS = 4
PAGE_SIZE = 128
N_MEM = 256
CA_HEADS = 4
CA_HEAD_DIM = 256
FFN_HIDDEN = 2816
CONV_WIDTH = 3
NORM_EPS = 1e-6

LANES = 128
SUBLANES = 8
VMEM_LIMIT = 56 * 1024 * 1024

NEG_BIG = -1e30


def _cparams(n_axes):
    return pltpu.CompilerParams(
        dimension_semantics=("arbitrary",) * n_axes,
        vmem_limit_bytes=VMEM_LIMIT)


def _const_spec(shape):
    nd = len(shape)
    return pl.BlockSpec(shape, lambda *_: (0,) * nd, pipeline_mode=pl.Buffered(1))


def _rms(x, g):
    return x * lax.rsqrt(jnp.mean(x * x, axis=-1, keepdims=True) + NORM_EPS) * g


def _dot(a, b):
    return jnp.dot(a, b, preferred_element_type=F32)


def _dot_nt(a, b):
    return lax.dot_general(a, b, (((1,), (1,)), ((), ())), preferred_element_type=F32)


def _head_rms(x, g, width, scale):
    outs = []
    for h in range(x.shape[-1] // width):
        c = x[:, h * width:(h + 1) * width]
        outs.append(_rms(c, g) * scale)
    return jnp.concatenate(outs, axis=-1)


def _prep_kernel(are_ref, aim_ref, ldt_ref, bre_ref, bim_ref,
                 q1_ref, k1_ref, q2_ref, k2_ref,
                 lbre_ref, lbim_ref, bbre_ref, bbim_ref, lam_ref, *, lam0):
    a_re = are_ref[...]
    a_im = aim_ref[...]
    dt = jnp.exp(ldt_ref[...])
    mag = jnp.exp(a_re * dt)
    lb_re = mag * jnp.cos(a_im * dt)
    lb_im = mag * jnp.sin(a_im * dt)
    den = a_re * a_re + a_im * a_im
    n_re = lb_re - 1.0
    f_re = (n_re * a_re + lb_im * a_im) / den
    f_im = (lb_im * a_re - n_re * a_im) / den
    b_re = bre_ref[...]
    b_im = bim_ref[...]
    lbre_ref[...] = lb_re
    lbim_ref[...] = lb_im
    bbre_ref[...] = f_re * b_re - f_im * b_im
    bbim_ref[...] = f_re * b_im + f_im * b_re
    s1 = jnp.sum(q1_ref[...] * k1_ref[...], axis=-1, keepdims=True)
    s2 = jnp.sum(q2_ref[...] * k2_ref[...], axis=-1, keepdims=True)
    lam_ref[...] = jnp.exp(s1) - jnp.exp(s2) + lam0


def _prep(a_re, a_im, log_dt, b_re, b_im, q1, k1, q2, k2, lam0):
    are = a_re.reshape(1, N_STATE)
    aim = a_im.reshape(1, N_STATE)
    ldt = jnp.repeat(log_dt, SSM_STATE).reshape(1, N_STATE)
    bre = b_re.transpose(2, 0, 1).reshape(SSM_GROUP_CH, N_STATE)
    bim = b_im.transpose(2, 0, 1).reshape(SSM_GROUP_CH, N_STATE)
    vec = lambda v: v.reshape(1, ATTN_HEAD_DIM)
    row = jax.ShapeDtypeStruct((1, N_STATE), F32)
    mat = jax.ShapeDtypeStruct((SSM_GROUP_CH, N_STATE), F32)
    return pl.pallas_call(
        functools.partial(_prep_kernel, lam0=lam0),
        out_shape=(row, row, mat, mat, jax.ShapeDtypeStruct((1, 1), F32)),
        name="prep",
    )(are, aim, ldt, bre, bim, vec(q1), vec(k1), vec(q2), vec(k2))


def _in_proj_kernel(x_ref, g_ref, w_ref, seg_ref, qg_ref, kg_ref,
                    u_ref, q_ref, k_ref, v_ref, kb_ref, vb_ref):
    xn = _rms(x_ref[...], g_ref[...]).astype(BF16)
    proj = _dot(xn, w_ref[...])
    seg = seg_ref[...]

    def chunk_norm(z, g):
        ms = _dot((z * z).astype(BF16), seg) * (1.0 / ATTN_HEAD_DIM)
        return z * lax.rsqrt(ms + NORM_EPS) * g

    u_ref[...] = proj[:, :SSM_WIDTH]
    q = proj[:, SSM_WIDTH:SSM_WIDTH + ATTN_WIDTH]
    k = proj[:, SSM_WIDTH + ATTN_WIDTH:SSM_WIDTH + 2 * ATTN_WIDTH]
    v = proj[:, SSM_WIDTH + 2 * ATTN_WIDTH:]
    qn = chunk_norm(q, qg_ref[...]) * (ATTN_HEAD_DIM ** -0.5)
    kn = chunk_norm(k, kg_ref[...])
    q_ref[...] = qn.astype(q_ref.dtype)
    k_ref[...] = kn
    v_ref[...] = v
    kb_ref[...] = kn.astype(BF16)
    vb_ref[...] = v.astype(BF16)


def _in_proj(x, ln1_g, w_in_b, seg, qg, kg, tm, q_dtype):
    n = x.shape[0]
    tok = lambda width: pl.BlockSpec((tm, width), lambda i: (i, 0))
    cols = w_in_b.shape[1]
    return pl.pallas_call(
        _in_proj_kernel,
        grid=(n // tm,),
        in_specs=[tok(D_MODEL), _const_spec((1, D_MODEL)), _const_spec((D_MODEL, cols)),
                  _const_spec((ATTN_WIDTH, ATTN_WIDTH)), _const_spec((1, ATTN_WIDTH)),
                  _const_spec((1, ATTN_WIDTH))],
        out_specs=[tok(SSM_WIDTH), tok(ATTN_WIDTH), tok(ATTN_WIDTH), tok(ATTN_WIDTH),
                   tok(ATTN_WIDTH), tok(ATTN_WIDTH)],
        out_shape=[jax.ShapeDtypeStruct((n, SSM_WIDTH), F32),
                   jax.ShapeDtypeStruct((n, ATTN_WIDTH), q_dtype),
                   jax.ShapeDtypeStruct((n, ATTN_WIDTH), F32),
                   jax.ShapeDtypeStruct((n, ATTN_WIDTH), F32),
                   jax.ShapeDtypeStruct((n, ATTN_WIDTH), BF16),
                   jax.ShapeDtypeStruct((n, ATTN_WIDTH), BF16)],
        compiler_params=_cparams(1),
        name="in_proj",
    )(x, ln1_g, w_in_b, seg, qg, kg)


N_SLABS = N_STATE // 2 // LANES
HALF = N_STATE // 2


def _gelu_tanh(x):
    c = math.sqrt(2.0 / math.pi)
    return 0.5 * x * (1.0 + jnp.tanh(c * (x + 0.044715 * (x * x * x))))


def _ssm_kernel(u_ref, bre_ref, bim_ref, cre_ref, cim_ref, lre_ref, lim_ref,
                d_ref, glu_ref, h0re_ref, h0im_ref,
                out_ref, hre_ref, him_ref, sre, sim,
                *, n_seq, tc, pitch, n_bulk, rows_bulk, pitch_bulk):
    c = pl.program_id(0)
    rows = n_seq * tc
    n_groups = 2 * n_seq // SUBLANES

    @pl.when(c == 0)
    def _():
        hre_ref[...] = h0re_ref[...]
        him_ref[...] = h0im_ref[...]

    u = u_ref[...].reshape(rows, SSM_WIDTH)
    ub = u.astype(BF16)

    for j in range(4):
        hh, jj = divmod(j, 2)
        uj = ub[:, LANES * j:LANES * (j + 1)]
        for src, dst in ((bre_ref, sre), (bim_ref, sim)):
            bu = _dot(uj, src[j])
            for kk in range(4):
                slab = 4 * jj + kk
                for bb in range(n_bulk):
                    r0 = (hh * n_bulk + bb) * pitch_bulk
                    dst[slab, r0:r0 + rows_bulk, :] = (
                        bu[bb * rows_bulk:(bb + 1) * rows_bulk, LANES * kk:LANES * (kk + 1)])

    def group_body(g, carry):
        base = g * (SUBLANES * pitch)
        lr = [lre_ref[g, k] for k in range(N_SLABS)]
        li = [lim_ref[g, k] for k in range(N_SLABS)]
        hr0 = tuple(hre_ref[g, k] for k in range(N_SLABS))
        hi0 = tuple(him_ref[g, k] for k in range(N_SLABS))

        def step(t, hc):
            hr, hi = hc
            idx = pl.ds(base + t, SUBLANES, stride=pitch)
            nr, ni = [], []
            for k in range(N_SLABS):
                br = sre[k, idx, :]
                bi = sim[k, idx, :]
                r = lr[k] * hr[k] - li[k] * hi[k] + br
                i = lr[k] * hi[k] + li[k] * hr[k] + bi
                sre[k, idx, :] = r
                sim[k, idx, :] = i
                nr.append(r)
                ni.append(i)
            return tuple(nr), tuple(ni)

        hr, hi = lax.fori_loop(0, tc, step, (hr0, hi0), unroll=min(tc, 4))
        for k in range(N_SLABS):
            hre_ref[g, k] = hr[k]
            him_ref[g, k] = hi[k]
        return carry

    lax.fori_loop(0, n_groups, group_body, 0)

    y_rows = []
    for bb in range(n_bulk):
        y_cols = []
        for j in range(4):
            hh, jj = divmod(j, 2)
            r0 = (hh * n_bulk + bb) * pitch_bulk
            hr = jnp.concatenate(
                [sre[4 * jj + kk, r0:r0 + rows_bulk, :] for kk in range(4)], axis=1).astype(BF16)
            hi = jnp.concatenate(
                [sim[4 * jj + kk, r0:r0 + rows_bulk, :] for kk in range(4)], axis=1).astype(BF16)
            y_cols.append(_dot(hr, cre_ref[j]) - _dot(hi, cim_ref[j]))
        y_rows.append(jnp.concatenate(y_cols, axis=1))
    y = jnp.concatenate(y_rows, axis=0) if n_bulk > 1 else y_rows[0]
    y = y + d_ref[...] * u
    gl = _gelu_tanh(y)
    gate = jax.nn.sigmoid(_dot(gl.astype(BF16), glu_ref[...]))
    out_ref[...] = (gl * gate).astype(out_ref.dtype).reshape(out_ref.shape)


def _ssm(u3, mats, lam_g, d_row, glu_b, h0, n_seq, tc):
    bre, bim, cre, cim = mats
    lre_g, lim_g = lam_g
    h0re, h0im = h0
    n_groups = 2 * n_seq // SUBLANES
    t_total = u3.shape[0] * u3.shape[1] // n_seq
    n_chunks = t_total // tc
    if tc % SUBLANES == 0:
        pitch = tc + SUBLANES
        n_bulk, rows_bulk, pitch_bulk = n_seq, tc, pitch
        ublock = (n_seq, tc, SSM_WIDTH)
    else:
        assert n_chunks == 1
        pitch = tc
        n_bulk, rows_bulk, pitch_bulk = 1, n_seq * tc, n_seq * tc
        ublock = (1, n_seq * tc, SSM_WIDTH)
    scr_rows = 2 * n_bulk * pitch_bulk
    hshape = (n_groups, N_SLABS, SUBLANES, LANES)
    kern = functools.partial(_ssm_kernel, n_seq=n_seq, tc=tc, pitch=pitch,
                             n_bulk=n_bulk, rows_bulk=rows_bulk, pitch_bulk=pitch_bulk)
    return pl.pallas_call(
        kern,
        grid=(n_chunks,),
        in_specs=[pl.BlockSpec(ublock, lambda c: (0, c, 0)),
                  _const_spec(bre.shape), _const_spec(bim.shape),
                  _const_spec(cre.shape), _const_spec(cim.shape),
                  _const_spec(hshape), _const_spec(hshape),
                  _const_spec((1, SSM_WIDTH)), _const_spec((SSM_WIDTH, SSM_WIDTH)),
                  _const_spec(hshape), _const_spec(hshape)],
        out_specs=[pl.BlockSpec(ublock, lambda c: (0, c, 0)),
                   _const_spec(hshape), _const_spec(hshape)],
        out_shape=[jax.ShapeDtypeStruct(u3.shape, BF16),
                   jax.ShapeDtypeStruct(hshape, F32),
                   jax.ShapeDtypeStruct(hshape, F32)],
        scratch_shapes=[pltpu.VMEM((N_SLABS, scr_rows, LANES), F32),
                        pltpu.VMEM((N_SLABS, scr_rows, LANES), F32)],
        compiler_params=_cparams(1),
        name="ssm",
    )(u3, bre, bim, cre, cim, lre_g, lim_g, d_row, glu_b, h0re, h0im)


def _state_to_groups(h, n_seq):
    x = h.reshape(n_seq, 2, N_SLABS, LANES).transpose(1, 0, 2, 3)
    x = x.reshape(2 * n_seq // SUBLANES, SUBLANES, N_SLABS, LANES)
    return x.transpose(0, 2, 1, 3)


def _groups_to_state(x, n_seq):
    x = x.transpose(0, 2, 1, 3).reshape(2, n_seq, N_SLABS, LANES)
    return x.transpose(1, 0, 2, 3).reshape(n_seq, SSM_GROUPS, SSM_STATE)


def _head_slope(h):
    return jnp.where(h == 0, 2.0 ** -2, jnp.where(h == 1, 2.0 ** -4,
                     jnp.where(h == 2, 2.0 ** -6, 2.0 ** -8))).astype(F32)


def _attn_prompt_kernel(lam_ref, q_ref, k_ref, v_ref, sg_ref, o_ref, *, blk, out_scale):
    h = pl.program_id(1)
    qi = pl.program_id(2)
    slope = _head_slope(h)
    lam = lam_ref[0, 0]

    q = q_ref[0]
    lane = lax.broadcasted_iota(jnp.int32, q.shape, 1)
    zero = jnp.zeros_like(q)
    qq = jnp.concatenate([jnp.where(lane < ATTN_HEAD_DIM, q, zero),
                          jnp.where(lane >= ATTN_HEAD_DIM, q, zero)], axis=0)

    col = lax.broadcasted_iota(jnp.int32, (1, blk), 1).astype(F32)
    colbias = slope * col

    def update(j, carry, diagonal):
        m, l, acc = carry
        start = pl.multiple_of(j * blk, blk)
        kb = k_ref[0, pl.ds(start, blk), :]
        vb = v_ref[0, pl.ds(start, blk), :]
        s = _dot_nt(qq, kb) + colbias
        if diagonal:
            r = lax.broadcasted_iota(jnp.int32, (blk, blk), 0)
            cidx = lax.broadcasted_iota(jnp.int32, (blk, blk), 1)
            keep = jnp.concatenate([r >= cidx, r >= cidx], axis=0)
            s = jnp.where(keep, s, -jnp.inf)
        off = slope * (j * blk).astype(F32)
        mn = jnp.maximum(m, jnp.max(s, axis=-1, keepdims=True) + off)
        p = jnp.exp(s - (mn - off))
        alpha = jnp.exp(m - mn)
        l = alpha * l + jnp.sum(p, axis=-1, keepdims=True)
        acc = alpha * acc + _dot(p.astype(BF16), vb)
        return mn, l, acc

    init = (jnp.full((2 * blk, 1), NEG_BIG, F32), jnp.zeros((2 * blk, 1), F32),
            jnp.zeros((2 * blk, ATTN_V_DIM), F32))
    carry = lax.fori_loop(0, qi, lambda j, c: update(j, c, False), init)
    m, l, acc = update(qi, carry, True)
    o = acc[:blk] / l[:blk] - lam * (acc[blk:] / l[blk:])
    o_ref[0] = (_rms(o, sg_ref[...]) * out_scale).astype(o_ref.dtype)


def _attn_prompt(lam, qb, kb, vb, sg, blk, out_scale):
    bsz, t, _ = qb.shape
    return pl.pallas_call(
        functools.partial(_attn_prompt_kernel, blk=blk, out_scale=out_scale),
        grid=(bsz, ATTN_HEADS, t // blk),
        in_specs=[pl.BlockSpec(memory_space=pltpu.SMEM),
                  pl.BlockSpec((1, blk, ATTN_V_DIM), lambda b, h, i: (b, i, h)),
                  pl.BlockSpec((1, t, ATTN_V_DIM), lambda b, h, i: (b, 0, h)),
                  pl.BlockSpec((1, t, ATTN_V_DIM), lambda b, h, i: (b, 0, h)),
                  _const_spec((1, ATTN_V_DIM))],
        out_specs=pl.BlockSpec((1, blk, ATTN_V_DIM), lambda b, h, i: (b, i, h)),
        out_shape=jax.ShapeDtypeStruct(qb.shape, BF16),
        compiler_params=_cparams(3),
        name="attn_prompt",
    )(lam, qb, kb, vb, sg)


def _attn_sample_kernel(pt_ref, lam_ref, q_ref, kn_ref, vn_ref, sg_ref, *refs,
                        n_pages, t_new, past_len, out_scale):
    k_refs = refs[:n_pages]
    v_refs = refs[n_pages:2 * n_pages]
    o_ref = refs[2 * n_pages]
    del pt_ref
    lam = lam_ref[0, 0]
    n_rows = 2 * ATTN_HEADS * t_new

    q4 = q_ref[0]
    qe = jnp.concatenate([q4] * (2 * ATTN_HEADS), axis=0)
    row = lax.broadcasted_iota(jnp.int32, qe.shape, 0)
    lane = lax.broadcasted_iota(jnp.int32, qe.shape, 1)
    head = (row // t_new) % ATTN_HEADS
    amap = row // (t_new * ATTN_HEADS)
    qe = jnp.where(lane // ATTN_HEAD_DIM == 2 * head + amap, qe, 0.0)
    qeb = qe.astype(BF16)

    rcol = lax.broadcasted_iota(jnp.int32, (n_rows, 1), 0)
    head_c = (rcol // t_new) % ATTN_HEADS
    tok_c = rcol % t_new
    slope = _head_slope(head_c)
    col = lax.broadcasted_iota(jnp.int32, (1, PAGE_SIZE), 1).astype(F32)
    colbias = slope * col

    m = jnp.full((n_rows, 1), NEG_BIG, F32)
    l = jnp.zeros((n_rows, 1), F32)
    acc = jnp.zeros((n_rows, ATTN_WIDTH), F32)
    for p in range(n_pages):
        kp = k_refs[p][0].astype(BF16)
        vp = v_refs[p][0].astype(BF16)
        s = _dot_nt(qeb, kp) + colbias
        off = slope * float(p * PAGE_SIZE)
        mn = jnp.maximum(m, jnp.max(s, axis=-1, keepdims=True) + off)
        pe = jnp.exp(s - (mn - off))
        alpha = jnp.exp(m - mn)
        l = alpha * l + jnp.sum(pe, axis=-1, keepdims=True)
        acc = alpha * acc + _dot(pe.astype(BF16), vp)
        m = mn

    kn = kn_ref[0]
    vn = vn_ref[0]
    s_new = []
    for c in range(t_new):
        sc = jnp.sum(qe * kn[c:c + 1, :], axis=-1, keepdims=True) + slope * float(past_len + c)
        s_new.append(jnp.where(tok_c >= c, sc, -jnp.inf))
    mn = m
    for sc in s_new:
        mn = jnp.maximum(mn, sc)
    alpha = jnp.exp(m - mn)
    l = alpha * l
    acc = alpha * acc
    for c in range(t_new):
        pc = jnp.exp(s_new[c] - mn)
        l = l + pc
        acc = acc + pc * vn[c:c + 1, :]

    acc = acc / l
    outs = []
    for h in range(ATTN_HEADS):
        r0 = h * t_new
        r1 = (ATTN_HEADS + h) * t_new
        a0 = acc[r0:r0 + t_new, ATTN_V_DIM * h:ATTN_V_DIM * (h + 1)]
        a1 = acc[r1:r1 + t_new, ATTN_V_DIM * h:ATTN_V_DIM * (h + 1)]
        outs.append(_rms(a0 - lam * a1, sg_ref[...]) * out_scale)
    o_ref[0] = jnp.concatenate(outs, axis=-1)


def _attn_sample(page_table, lam, q3, kn3, vn3, sg, cache_k2, cache_v2, past_len, out_scale):
    n_dec, t_new, _ = q3.shape
    n_pages = page_table.shape[1]
    pt = page_table.reshape(-1)
    tok_spec = pl.BlockSpec((1, t_new, ATTN_WIDTH), lambda b, pt: (b, 0, 0))

    def page_spec(p):
        return pl.BlockSpec((1, PAGE_SIZE, ATTN_WIDTH),
                            lambda b, pt, p=p: (pt[b * n_pages + p], 0, 0))

    grid_spec = pltpu.PrefetchScalarGridSpec(
        num_scalar_prefetch=1,
        grid=(n_dec,),
        in_specs=[pl.BlockSpec(memory_space=pltpu.SMEM), tok_spec, tok_spec, tok_spec,
                  pl.BlockSpec((1, ATTN_V_DIM), lambda b, pt: (0, 0))]
                 + [page_spec(p) for p in range(n_pages)] * 2,
        out_specs=tok_spec)
    kern = functools.partial(_attn_sample_kernel, n_pages=n_pages, t_new=t_new,
                             past_len=past_len, out_scale=out_scale)
    return pl.pallas_call(
        kern,
        grid_spec=grid_spec,
        out_shape=jax.ShapeDtypeStruct(q3.shape, F32),
        compiler_params=_cparams(1),
        name="attn_sample",
    )(pt, lam, q3, kn3, vn3, sg, *([cache_k2] * n_pages), *([cache_v2] * n_pages))


def _out_proj_kernel(x_ref, s_ref, o_ref, w1_ref, w2_ref, g2_ref, wq_ref, qg_ref,
                     x1_ref, cq_ref):
    x1 = (x_ref[...] + _dot(s_ref[...].astype(BF16), w1_ref[...])
          + _dot(o_ref[...].astype(BF16), w2_ref[...]))
    x1_ref[...] = x1
    xn = _rms(x1, g2_ref[...]).astype(BF16)
    cq = _dot(xn, wq_ref[...])
    cq_ref[...] = _head_rms(cq, qg_ref[...], CA_HEAD_DIM, CA_HEAD_DIM ** -0.5).astype(cq_ref.dtype)


def _out_proj(x, ssm_o, att_o, w1, w2, g2, wq, qg, tm, cq_dtype):
    n = x.shape[0]
    tok = lambda width: pl.BlockSpec((tm, width), lambda i: (i, 0))
    return pl.pallas_call(
        _out_proj_kernel,
        grid=(n // tm,),
        in_specs=[tok(D_MODEL), tok(SSM_WIDTH), tok(ATTN_WIDTH),
                  _const_spec((SSM_WIDTH, D_MODEL)), _const_spec((ATTN_WIDTH, D_MODEL)),
                  _const_spec((1, D_MODEL)), _const_spec((D_MODEL, D_MODEL)),
                  _const_spec((1, CA_HEAD_DIM))],
        out_specs=[tok(D_MODEL), tok(D_MODEL)],
        out_shape=[jax.ShapeDtypeStruct((n, D_MODEL), F32),
                   jax.ShapeDtypeStruct((n, D_MODEL), cq_dtype)],
        compiler_params=_cparams(1),
        name="out_proj",
    )(x, ssm_o, att_o, w1, w2, g2, wq, qg)


def _mem_kv_kernel(m_ref, g_ref, wk_ref, wv_ref, kg_ref, mk_ref, mv_ref, mkb_ref, mvb_ref):
    mn = _rms(m_ref[...], g_ref[...]).astype(BF16)
    mk = _head_rms(_dot(mn, wk_ref[...]), kg_ref[...], CA_HEAD_DIM, 1.0)
    mv = _dot(mn, wv_ref[...])
    mk_ref[...] = mk
    mv_ref[...] = mv
    mkb_ref[...] = mk.astype(BF16)
    mvb_ref[...] = mv.astype(BF16)


def _mem_kv(mem, g, wk, wv, kg, tm):
    n = mem.shape[0]
    tok = pl.BlockSpec((tm, D_MODEL), lambda i: (i, 0))
    wspec = _const_spec((D_MODEL, D_MODEL))
    return pl.pallas_call(
        _mem_kv_kernel,
        grid=(n // tm,),
        in_specs=[tok, _const_spec((1, D_MODEL)), wspec, wspec, _const_spec((1, CA_HEAD_DIM))],
        out_specs=[tok, tok, tok, tok],
        out_shape=[jax.ShapeDtypeStruct((n, D_MODEL), F32)] * 2
                  + [jax.ShapeDtypeStruct((n, D_MODEL), BF16)] * 2,
        compiler_params=_cparams(1),
        name="mem_kv",
    )(mem, g, wk, wv, kg)


def _cross_prompt_kernel(cq_ref, mk_ref, mv_ref, co_ref):
    cq = cq_ref[...]
    outs = []
    for h in range(CA_HEADS):
        sl = slice(CA_HEAD_DIM * h, CA_HEAD_DIM * (h + 1))
        s = _dot_nt(cq[:, sl], mk_ref[0, :, sl])
        p = jnp.exp(s - jnp.max(s, axis=-1, keepdims=True))
        l = jnp.sum(p, axis=-1, keepdims=True)
        outs.append(_dot((p / l).astype(BF16), mv_ref[0, :, sl]))
    co_ref[...] = jnp.concatenate(outs, axis=-1).astype(co_ref.dtype)


def _cross_prompt(cq, mkb, mvb, tm):
    n = cq.shape[0]
    bsz = mkb.shape[0]
    per_b = n // bsz // tm
    tok = pl.BlockSpec((tm, D_MODEL), lambda b, i: (b * per_b + i, 0))
    mem = pl.BlockSpec((1, N_MEM, D_MODEL), lambda b, i: (b, 0, 0))
    return pl.pallas_call(
        _cross_prompt_kernel,
        grid=(bsz, per_b),
        in_specs=[tok, mem, mem],
        out_specs=tok,
        out_shape=jax.ShapeDtypeStruct((n, D_MODEL), BF16),
        compiler_params=_cparams(2),
        name="cross_prompt",
    )(cq, mkb, mvb)


def _cross_sample_kernel(cq_ref, mk_ref, mv_ref, co_ref, *, t_new):
    n_rows = CA_HEADS * t_new
    q4 = cq_ref[0]
    qe = jnp.concatenate([q4] * CA_HEADS, axis=0)
    row = lax.broadcasted_iota(jnp.int32, qe.shape, 0)
    lane = lax.broadcasted_iota(jnp.int32, qe.shape, 1)
    qe = jnp.where(lane // CA_HEAD_DIM == row // t_new, qe, 0.0).astype(BF16)
    s = _dot_nt(qe, mk_ref[0].astype(BF16))
    p = jnp.exp(s - jnp.max(s, axis=-1, keepdims=True))
    l = jnp.sum(p, axis=-1, keepdims=True)
    full = _dot((p / l).astype(BF16), mv_ref[0].astype(BF16))
    outs = [full[h * t_new:(h + 1) * t_new, CA_HEAD_DIM * h:CA_HEAD_DIM * (h + 1)]
            for h in range(CA_HEADS)]
    co_ref[0] = jnp.concatenate(outs, axis=-1)
    del n_rows


def _cross_sample(cq3, mem_k, mem_v):
    n_dec, t_new, _ = cq3.shape
    tok = pl.BlockSpec((1, t_new, D_MODEL), lambda b: (b, 0, 0))
    mem = pl.BlockSpec((1, N_MEM, D_MODEL), lambda b: (b, 0, 0))
    return pl.pallas_call(
        functools.partial(_cross_sample_kernel, t_new=t_new),
        grid=(n_dec,),
        in_specs=[tok, mem, mem],
        out_specs=tok,
        out_shape=jax.ShapeDtypeStruct(cq3.shape, F32),
        compiler_params=_cparams(1),
        name="cross_sample",
    )(cq3, mem_k, mem_v)


def _ffn_kernel(x_ref, co_ref, wo_ref, g3_ref, wg_ref, wv_ref, cw_ref, cb_ref, wd_ref, prev_ref,
                y_ref, cs_ref, hbuf, *, tm, shift, pad):
    i = pl.program_id(1)

    @pl.when(i == 0)
    def _():
        hbuf[pad - 2 * shift:pad, :] = prev_ref[0]

    x2 = x_ref[...] + _dot(co_ref[...].astype(BF16), wo_ref[...])
    xn = _rms(x2, g3_ref[...]).astype(BF16)
    hbuf[pad:pad + tm, :] = _dot(xn, wg_ref[...])
    hv = _dot(xn, wv_ref[...])
    cw = cw_ref[...]
    conv = (cb_ref[...] + cw[0:1, :] * hbuf[pad - 2 * shift:pad - 2 * shift + tm, :]
            + cw[1:2, :] * hbuf[pad - shift:pad - shift + tm, :]
            + cw[2:3, :] * hbuf[pad:pad + tm, :])
    act = (conv * jax.nn.sigmoid(conv) * hv).astype(BF16)
    y_ref[...] = x2 + _dot(act, wd_ref[...])
    last = hbuf[pad + tm - 2 * shift:pad + tm, :]
    cs_ref[0] = last
    hbuf[pad - 2 * shift:pad, :] = last


def _ffn(x1, co, wo, g3, wg, wv, cw, cb, wd, prev, tm, shift):
    n = x1.shape[0]
    n_seq_blocks = prev.shape[0]
    per_b = n // n_seq_blocks // tm
    pad = -(-2 * shift // SUBLANES) * SUBLANES
    tok = pl.BlockSpec((tm, D_MODEL), lambda b, i: (b * per_b + i, 0))
    state = pl.BlockSpec((1, 2 * shift, FFN_HIDDEN), lambda b, i: (b, 0, 0))
    return pl.pallas_call(
        functools.partial(_ffn_kernel, tm=tm, shift=shift, pad=pad),
        grid=(n_seq_blocks, per_b),
        in_specs=[tok, tok, _const_spec((D_MODEL, D_MODEL)), _const_spec((1, D_MODEL)),
                  _const_spec((D_MODEL, FFN_HIDDEN)), _const_spec((D_MODEL, FFN_HIDDEN)),
                  _const_spec((CONV_WIDTH, FFN_HIDDEN)), _const_spec((1, FFN_HIDDEN)),
                  _const_spec((FFN_HIDDEN, D_MODEL)), state],
        out_specs=[tok, state],
        out_shape=[jax.ShapeDtypeStruct((n, D_MODEL), F32),
                   jax.ShapeDtypeStruct(prev.shape, F32)],
        scratch_shapes=[pltpu.VMEM((pad + tm, FFN_HIDDEN), F32)],
        compiler_params=_cparams(2),
        name="ffn",
    )(x1, co, wo, g3, wg, wv, cw, cb, wd, prev)


def _block_diag(blocks, n_outer):
    n, r, c = blocks.shape
    per = n // n_outer
    b = blocks.reshape(n_outer, per, r, c)
    eye = jnp.eye(per, dtype=blocks.dtype)
    full = b[:, :, :, None, :] * eye[None, :, None, :, None]
    return full.reshape(n_outer, per * r, per * c)


def _lam_groups(lb, n_seq):
    halves = lb.reshape(2, 1, N_SLABS, LANES)
    rows = jnp.broadcast_to(halves, (2, n_seq, N_SLABS, LANES))
    rows = rows.reshape(2 * n_seq // SUBLANES, SUBLANES, N_SLABS, LANES)
    return rows.transpose(0, 2, 1, 3)


def _layer(l, x_prompt, x_sample, mem_prompt, cache_k, cache_v, page_table,
           state_ssm_re, state_ssm_im, state_conv, cache_mem_k, cache_mem_v, p):
    bsz, seq, _ = x_prompt.shape
    n_dec, t_new, _ = x_sample.shape
    n_p = bsz * seq
    n_s = n_dec * t_new
    past_len = page_table.shape[1] * PAGE_SIZE
    lam0 = 0.8 - 0.6 * math.exp(-0.3 * l)
    out_scale = 1.0 - lam0
    row = lambda v: v.reshape(1, -1).astype(F32)
    bf = lambda w: w.astype(BF16)

    lb_re, lb_im, bb_re, bb_im, lam = _prep(
        p['ssm_a_re'], p['ssm_a_im'], p['ssm_log_dt'], p['ssm_b_re'], p['ssm_b_im'],
        p['lam_q1'], p['lam_k1'], p['lam_q2'], p['lam_k2'], lam0)

    def b_mats(bbt):
        blocks = bbt.reshape(SSM_GROUP_CH, SSM_GROUPS, SSM_STATE).transpose(1, 0, 2)
        return bf(_block_diag(blocks, 4))

    def c_mats(cm):
        return bf(_block_diag(cm.transpose(0, 2, 1), 4))

    mats = (b_mats(bb_re), b_mats(bb_im), c_mats(p['ssm_c_re']), c_mats(p['ssm_c_im']))
    d_row = row(p['ssm_d'])
    glu_b = bf(p['ssm_glu_w'])

    w_in_b = bf(p['w_in'])
    seg = jnp.kron(jnp.eye(ATTN_WIDTH // ATTN_HEAD_DIM, dtype=F32),
                   jnp.ones((ATTN_HEAD_DIM, ATTN_HEAD_DIM), F32)).astype(BF16)
    qg = jnp.tile(p['q_norm_g'], ATTN_WIDTH // ATTN_HEAD_DIM).reshape(1, ATTN_WIDTH)
    kg = jnp.tile(p['k_norm_g'], ATTN_WIDTH // ATTN_HEAD_DIM).reshape(1, ATTN_WIDTH)
    sg = row(p['subln_g'])
    w_out_b = bf(p['w_out'])
    w1, w2 = w_out_b[:SSM_WIDTH], w_out_b[SSM_WIDTH:]
    wq_b, wk_b, wv_b, wo_b = bf(p['ca_wq']), bf(p['ca_wk']), bf(p['ca_wv']), bf(p['ca_wo'])
    wg_b, wvf_b, wd_b = bf(p['ffn_wg']), bf(p['ffn_wv']), bf(p['ffn_wd'])
    cw = p['ffn_conv_w'].astype(F32)
    cb = row(p['ffn_conv_b'])

    xp = x_prompt.reshape(n_p, D_MODEL)
    u_p, q_p, k_p, v_p, kb_p, vb_p = _in_proj(xp, row(p['ln1_g']), w_in_b, seg, qg, kg, 512, BF16)
    zeros_h = jnp.zeros((2 * bsz // SUBLANES, N_SLABS, SUBLANES, LANES), F32)
    ssm_p, hre_p, him_p = _ssm(u_p.reshape(bsz, seq, SSM_WIDTH), mats,
                               (_lam_groups(lb_re, bsz), _lam_groups(lb_im, bsz)),
                               d_row, glu_b, (zeros_h, zeros_h), bsz, 256)
    att_p = _attn_prompt(lam, q_p.reshape(bsz, seq, ATTN_WIDTH), kb_p.reshape(bsz, seq, ATTN_WIDTH),
                         vb_p.reshape(bsz, seq, ATTN_WIDTH), sg, 256, out_scale)
    x1_p, cq_p = _out_proj(xp, ssm_p.reshape(n_p, SSM_WIDTH), att_p.reshape(n_p, ATTN_WIDTH),
                           w1, w2, row(p['ln2_g']), wq_b, row(p['ca_q_norm_g']), 512, BF16)
    mk, mv, mkb, mvb = _mem_kv(mem_prompt.reshape(bsz * N_MEM, D_MODEL), row(p['mem_norm_g']),
                               wk_b, wv_b, row(p['ca_k_norm_g']), 256)
    co_p = _cross_prompt(cq_p, mkb.reshape(bsz, N_MEM, D_MODEL), mvb.reshape(bsz, N_MEM, D_MODEL), 512)
    y_p, conv_p = _ffn(x1_p, co_p, wo_b, row(p['ln3_g']), wg_b, wvf_b, cw, cb, wd_b,
                       jnp.zeros((bsz, CONV_WIDTH - 1, FFN_HIDDEN), F32), 512, 1)

    xs = x_sample.reshape(n_s, D_MODEL)
    u_s, q_s, k_s, v_s, _, _ = _in_proj(xs, row(p['ln1_g']), w_in_b, seg, qg, kg, n_s, F32)
    h0 = (_state_to_groups(state_ssm_re.reshape(n_dec, N_STATE), n_dec),
          _state_to_groups(state_ssm_im.reshape(n_dec, N_STATE), n_dec))
    ssm_s, hre_s, him_s = _ssm(u_s.reshape(1, n_s, SSM_WIDTH), mats,
                               (_lam_groups(lb_re, n_dec), _lam_groups(lb_im, n_dec)),
                               d_row, glu_b, h0, n_dec, t_new)
    n_phys = cache_k.shape[0]
    att_s = _attn_sample(page_table, lam, q_s.reshape(n_dec, t_new, ATTN_WIDTH),
                         k_s.reshape(n_dec, t_new, ATTN_WIDTH), v_s.reshape(n_dec, t_new, ATTN_WIDTH),
                         sg, cache_k.reshape(n_phys, PAGE_SIZE, ATTN_WIDTH),
                         cache_v.reshape(n_phys, PAGE_SIZE, ATTN_WIDTH), past_len, out_scale)
    x1_s, cq_s = _out_proj(xs, ssm_s.reshape(n_s, SSM_WIDTH), att_s.reshape(n_s, ATTN_WIDTH),
                           w1, w2, row(p['ln2_g']), wq_b, row(p['ca_q_norm_g']), n_s, F32)
    co_s = _cross_sample(cq_s.reshape(n_dec, t_new, D_MODEL),
                         cache_mem_k.reshape(n_dec, N_MEM, D_MODEL),
                         cache_mem_v.reshape(n_dec, N_MEM, D_MODEL))
    tmaj = lambda a: a.reshape(n_dec, t_new, -1).transpose(1, 0, 2).reshape(n_s, -1)
    prev_s = state_conv.transpose(1, 0, 2).reshape(1, (CONV_WIDTH - 1) * n_dec, FFN_HIDDEN)
    y_s, conv_s = _ffn(tmaj(x1_s), tmaj(co_s), wo_b, row(p['ln3_g']), wg_b, wvf_b, cw, cb, wd_b,
                       prev_s, n_s, n_dec)
    y_s = y_s.reshape(t_new, n_dec, D_MODEL).transpose(1, 0, 2)
    conv_s = conv_s.reshape(CONV_WIDTH - 1, n_dec, FFN_HIDDEN).transpose(1, 0, 2)

    return (y_p.reshape(bsz, seq, D_MODEL), y_s,
            k_p.reshape(bsz, seq, ATTN_HEADS, 2, ATTN_HEAD_DIM),
            v_p.reshape(bsz, seq, ATTN_HEADS, ATTN_V_DIM),
            k_s.reshape(n_dec, t_new, ATTN_HEADS, 2, ATTN_HEAD_DIM),
            v_s.reshape(n_dec, t_new, ATTN_HEADS, ATTN_V_DIM),
            _groups_to_state(hre_p, bsz), _groups_to_state(him_p, bsz),
            _groups_to_state(hre_s, n_dec), _groups_to_state(him_s, n_dec),
            conv_p, conv_s,
            mk.reshape(bsz, N_MEM, CA_HEADS, CA_HEAD_DIM), mv.reshape(bsz, N_MEM, CA_HEADS, CA_HEAD_DIM))


def kernel(x_prompt, x_sample, mem_prompt, cache_k, cache_v, page_table, state_ssm_re, state_ssm_im, state_conv, cache_mem_k, cache_mem_v, ln1_g, w_in, ssm_a_re, ssm_a_im, ssm_b_re, ssm_b_im, ssm_c_re, ssm_c_im, ssm_d, ssm_log_dt, ssm_glu_w, q_norm_g, k_norm_g, lam_q1, lam_k1, lam_q2, lam_k2, subln_g, w_out, ln2_g, mem_norm_g, ca_wq, ca_wk, ca_wv, ca_q_norm_g, ca_k_norm_g, ca_wo, ln3_g, ffn_wg, ffn_wv, ffn_conv_w, ffn_conv_b, ffn_wd):
    params = dict(
        ln1_g=ln1_g, w_in=w_in, ssm_a_re=ssm_a_re, ssm_a_im=ssm_a_im, ssm_b_re=ssm_b_re,
        ssm_b_im=ssm_b_im, ssm_c_re=ssm_c_re, ssm_c_im=ssm_c_im, ssm_d=ssm_d,
        ssm_log_dt=ssm_log_dt, ssm_glu_w=ssm_glu_w, q_norm_g=q_norm_g, k_norm_g=k_norm_g,
        lam_q1=lam_q1, lam_k1=lam_k1, lam_q2=lam_q2, lam_k2=lam_k2, subln_g=subln_g,
        w_out=w_out, ln2_g=ln2_g, mem_norm_g=mem_norm_g, ca_wq=ca_wq, ca_wk=ca_wk, ca_wv=ca_wv,
        ca_q_norm_g=ca_q_norm_g, ca_k_norm_g=ca_k_norm_g, ca_wo=ca_wo, ln3_g=ln3_g,
        ffn_wg=ffn_wg, ffn_wv=ffn_wv, ffn_conv_w=ffn_conv_w, ffn_conv_b=ffn_conv_b, ffn_wd=ffn_wd)
    y_p, y_s = x_prompt, x_sample
    per_layer = []
    for l in range(w_in.shape[0]):
        p = {k: v[l] for k, v in params.items()}
        outs = _layer(l, y_p, y_s, mem_prompt, cache_k[l], cache_v[l], page_table,
                      state_ssm_re[l], state_ssm_im[l], state_conv[l],
                      cache_mem_k[l], cache_mem_v[l], p)
        y_p, y_s = outs[0], outs[1]
        per_layer.append(outs[2:])
    return (y_p, y_s) + tuple(jnp.stack(leaf) for leaf in zip(*per_layer))
```

```python
import functools
import math

import jax
import jax.numpy as jnp
from jax import lax
from jax.experimental import pallas as pl
from jax.experimental.pallas import tpu as pltpu

F32 = jnp.float32
BF16 = jnp.bfloat16

D_MODEL = 1024
SSM_WIDTH = 512
SSM_GROUP_CH = 16
SSM_GROUPS = 32
SSM_STATE = 64
N_STATE = SSM_GROUPS * SSM_STATE
ATTN_WIDTH = 512
ATTN_HEAD_DIM = 64
ATTN_V_DIM = 128
ATTN_HEADS = 4
PAGE_SIZE = 128
N_MEM = 256
CA_HEADS = 4
CA_HEAD_DIM = 256
FFN_HIDDEN = 2816
CONV_WIDTH = 3
NORM_EPS = 1e-6

LANES = 128
SUBLANES = 8
VMEM_LIMIT = 56 * 1024 * 1024

NEG_BIG = -1e30


def _cparams(n_axes):
    return pltpu.CompilerParams(
        dimension_semantics=("arbitrary",) * n_axes,
        vmem_limit_bytes=VMEM_LIMIT)


def _const_spec(shape):
    nd = len(shape)
    return pl.BlockSpec(shape, lambda *_: (0,) * nd, pipeline_mode=pl.Buffered(1))


def _rms(x, g):
    return x * lax.rsqrt(jnp.mean(x * x, axis=-1, keepdims=True) + NORM_EPS) * g


def _dot(a, b):
    return jnp.dot(a, b, preferred_element_type=F32)


def _dot_nt(a, b):
    return lax.dot_general(a, b, (((1,), (1,)), ((), ())), preferred_element_type=F32)


def _head_rms(x, g, width, scale):
    outs = []
    for h in range(x.shape[-1] // width):
        c = x[:, h * width:(h + 1) * width]
        outs.append(_rms(c, g) * scale)
    return jnp.concatenate(outs, axis=-1)


def _prep_kernel(are_ref, aim_ref, ldt_ref, bre_ref, bim_ref,
                 q1_ref, k1_ref, q2_ref, k2_ref,
                 lbre_ref, lbim_ref, bbre_ref, bbim_ref, lam_ref, *, lam0):
    a_re = are_ref[...]
    a_im = aim_ref[...]
    dt = jnp.exp(ldt_ref[...])
    mag = jnp.exp(a_re * dt)
    lb_re = mag * jnp.cos(a_im * dt)
    lb_im = mag * jnp.sin(a_im * dt)
    den = a_re * a_re + a_im * a_im
    n_re = lb_re - 1.0
    f_re = (n_re * a_re + lb_im * a_im) / den
    f_im = (lb_im * a_re - n_re * a_im) / den
    b_re = bre_ref[...]
    b_im = bim_ref[...]
    lbre_ref[...] = lb_re
    lbim_ref[...] = lb_im
    bbre_ref[...] = f_re * b_re - f_im * b_im
    bbim_ref[...] = f_re * b_im + f_im * b_re
    s1 = jnp.sum(q1_ref[...] * k1_ref[...], axis=-1, keepdims=True)
    s2 = jnp.sum(q2_ref[...] * k2_ref[...], axis=-1, keepdims=True)
    lam_ref[...] = jnp.exp(s1) - jnp.exp(s2) + lam0


def _prep(a_re, a_im, log_dt, b_re, b_im, q1, k1, q2, k2, lam0):
    are = a_re.reshape(1, N_STATE)
    aim = a_im.reshape(1, N_STATE)
    ldt = jnp.repeat(log_dt, SSM_STATE).reshape(1, N_STATE)
    bre = b_re.transpose(2, 0, 1).reshape(SSM_GROUP_CH, N_STATE)
    bim = b_im.transpose(2, 0, 1).reshape(SSM_GROUP_CH, N_STATE)
    vec = lambda v: v.reshape(1, ATTN_HEAD_DIM)
    row = jax.ShapeDtypeStruct((1, N_STATE), F32)
    mat = jax.ShapeDtypeStruct((SSM_GROUP_CH, N_STATE), F32)
    return pl.pallas_call(
        functools.partial(_prep_kernel, lam0=lam0),
        out_shape=(row, row, mat, mat, jax.ShapeDtypeStruct((1, 1), F32)),
        name="prep",
    )(are, aim, ldt, bre, bim, vec(q1), vec(k1), vec(q2), vec(k2))


def _in_proj_kernel(x_ref, g_ref, w_ref, wk_ref, seg_ref, qg_ref, kg_ref,
                    u_ref, q_ref, k_ref, v_ref, *bf_refs, key_major, blk):
    xn = _rms(x_ref[...], g_ref[...]).astype(BF16)
    proj = _dot(xn, w_ref[...])
    seg = seg_ref[...]

    def chunk_norm(z, g):
        ms = _dot((z * z).astype(BF16), seg) * (1.0 / ATTN_HEAD_DIM)
        return z * lax.rsqrt(ms + NORM_EPS) * g

    u_ref[...] = proj[:, :SSM_WIDTH]
    q = proj[:, SSM_WIDTH:SSM_WIDTH + ATTN_WIDTH]
    v = proj[:, SSM_WIDTH + ATTN_WIDTH:]
    q_ref[...] = (chunk_norm(q, qg_ref[...]) * (ATTN_HEAD_DIM ** -0.5)).astype(q_ref.dtype)
    if key_major:
        kb_ref, vb_ref = bf_refs
        tm = x_ref.shape[0]
        z = _dot_nt(wk_ref[...], xn).reshape(ATTN_WIDTH // ATTN_HEAD_DIM, ATTN_HEAD_DIM, tm)
        ms = jnp.mean(z * z, axis=1, keepdims=True)
        kn = (z * lax.rsqrt(ms + NORM_EPS) * kg_ref[...]).reshape(ATTN_WIDTH, tm)
        k_ref[0] = kn
        for i in range(tm // blk):
            kb_ref[0, i] = kn[:, i * blk:(i + 1) * blk].astype(BF16)
        for h in range(ATTN_HEADS):
            v_ref[:, h, :] = v[:, ATTN_V_DIM * h:ATTN_V_DIM * (h + 1)]
        vb_ref[...] = v.astype(BF16)
    else:
        k_ref[...] = chunk_norm(_dot(xn, wk_ref[...]), kg_ref[...])
        v_ref[...] = v


def _in_proj(x, ln1_g, w_uqv, wk, seg, qg, kg, tm, q_dtype, n_batch=None, blk=None):
    n = x.shape[0]
    key_major = n_batch is not None
    bsz = n_batch if key_major else 1
    per_b = n // bsz // tm
    tok = lambda width: pl.BlockSpec((tm, width), lambda b, i: (b * per_b + i, 0))
    out_specs = [tok(SSM_WIDTH), tok(ATTN_WIDTH)]
    out_shape = [jax.ShapeDtypeStruct((n, SSM_WIDTH), F32),
                 jax.ShapeDtypeStruct((n, ATTN_WIDTH), q_dtype)]
    if key_major:
        t = n // bsz
        out_specs += [pl.BlockSpec((1, ATTN_WIDTH, tm), lambda b, i: (b, 0, i)),
                      pl.BlockSpec((tm, ATTN_HEADS, ATTN_V_DIM), lambda b, i: (b * per_b + i, 0, 0)),
                      pl.BlockSpec((1, tm // blk, ATTN_WIDTH, blk), lambda b, i: (b, i, 0, 0)),
                      tok(ATTN_WIDTH)]
        out_shape += [jax.ShapeDtypeStruct((bsz, ATTN_WIDTH, t), F32),
                      jax.ShapeDtypeStruct((n, ATTN_HEADS, ATTN_V_DIM), F32),
                      jax.ShapeDtypeStruct((bsz, t // blk, ATTN_WIDTH, blk), BF16),
                      jax.ShapeDtypeStruct((n, ATTN_WIDTH), BF16)]
    else:
        out_specs += [tok(ATTN_WIDTH), tok(ATTN_WIDTH)]
        out_shape += [jax.ShapeDtypeStruct((n, ATTN_WIDTH), F32)] * 2
    return pl.pallas_call(
        functools.partial(_in_proj_kernel, key_major=key_major, blk=blk),
        grid=(bsz, per_b),
        in_specs=[tok(D_MODEL), _const_spec((1, D_MODEL)), _const_spec(w_uqv.shape),
                  _const_spec(wk.shape), _const_spec((ATTN_WIDTH, ATTN_WIDTH)),
                  _const_spec((1, ATTN_WIDTH)), _const_spec(kg.shape)],
        out_specs=out_specs,
        out_shape=out_shape,
        compiler_params=_cparams(2),
        name="in_proj",
    )(x, ln1_g, w_uqv, wk, seg, qg, kg)


N_SLABS = N_STATE // 2 // LANES
HALF = N_STATE // 2


def _gelu_tanh(x):
    c = math.sqrt(2.0 / math.pi)
    return 0.5 * x * (1.0 + jnp.tanh(c * (x + 0.044715 * (x * x * x))))


def _ssm_kernel(u_ref, bre_ref, bim_ref, cre_ref, cim_ref, lre_ref, lim_ref,
                d_ref, glu_ref, h0re_ref, h0im_ref,
                out_ref, hre_ref, him_ref, sre, sim,
                *, n_seq, tc, pitch, n_bulk, rows_bulk, pitch_bulk):
    c = pl.program_id(0)
    rows = n_seq * tc
    n_groups = 2 * n_seq // SUBLANES

    @pl.when(c == 0)
    def _():
        hre_ref[...] = h0re_ref[...]
        him_ref[...] = h0im_ref[...]

    u = u_ref[...].reshape(rows, SSM_WIDTH)
    ub = u.astype(BF16)

    for j in range(4):
        hh, jj = divmod(j, 2)
        uj = ub[:, LANES * j:LANES * (j + 1)]
        for src, dst in ((bre_ref, sre), (bim_ref, sim)):
            bu = _dot(uj, src[j])
            for kk in range(4):
                slab = 4 * jj + kk
                for bb in range(n_bulk):
                    r0 = (hh * n_bulk + bb) * pitch_bulk
                    dst[slab, r0:r0 + rows_bulk, :] = (
                        bu[bb * rows_bulk:(bb + 1) * rows_bulk, LANES * kk:LANES * (kk + 1)])

    def group_body(g, carry):
        base = g * (SUBLANES * pitch)
        lr = [lre_ref[g, k] for k in range(N_SLABS)]
        li = [lim_ref[g, k] for k in range(N_SLABS)]
        hr0 = tuple(hre_ref[g, k] for k in range(N_SLABS))
        hi0 = tuple(him_ref[g, k] for k in range(N_SLABS))

        def step(t, hc):
            hr, hi = hc
            idx = pl.ds(base + t, SUBLANES, stride=pitch)
            nr, ni = [], []
            for k in range(N_SLABS):
                br = sre[k, idx, :]
                bi = sim[k, idx, :]
                r = lr[k] * hr[k] - li[k] * hi[k] + br
                i = lr[k] * hi[k] + li[k] * hr[k] + bi
                sre[k, idx, :] = r
                sim[k, idx, :] = i
                nr.append(r)
                ni.append(i)
            return tuple(nr), tuple(ni)

        hr, hi = lax.fori_loop(0, tc, step, (hr0, hi0), unroll=min(tc, 4))
        for k in range(N_SLABS):
            hre_ref[g, k] = hr[k]
            him_ref[g, k] = hi[k]
        return carry

    lax.fori_loop(0, n_groups, group_body, 0)

    y_rows = []
    for bb in range(n_bulk):
        y_cols = []
        for j in range(4):
            hh, jj = divmod(j, 2)
            r0 = (hh * n_bulk + bb) * pitch_bulk
            hr = jnp.concatenate(
                [sre[4 * jj + kk, r0:r0 + rows_bulk, :] for kk in range(4)], axis=1).astype(BF16)
            hi = jnp.concatenate(
                [sim[4 * jj + kk, r0:r0 + rows_bulk, :] for kk in range(4)], axis=1).astype(BF16)
            y_cols.append(_dot(hr, cre_ref[j]) - _dot(hi, cim_ref[j]))
        y_rows.append(jnp.concatenate(y_cols, axis=1))
    y = jnp.concatenate(y_rows, axis=0) if n_bulk > 1 else y_rows[0]
    y = y + d_ref[...] * u
    gl = _gelu_tanh(y)
    gate = jax.nn.sigmoid(_dot(gl.astype(BF16), glu_ref[...]))
    out_ref[...] = (gl * gate).astype(out_ref.dtype).reshape(out_ref.shape)


def _ssm(u3, mats, lam_g, d_row, glu_b, h0, n_seq, tc):
    bre, bim, cre, cim = mats
    lre_g, lim_g = lam_g
    h0re, h0im = h0
    n_groups = 2 * n_seq // SUBLANES
    t_total = u3.shape[0] * u3.shape[1] // n_seq
    n_chunks = t_total // tc
    if tc % SUBLANES == 0:
        pitch = tc + SUBLANES
        n_bulk, rows_bulk, pitch_bulk = n_seq, tc, pitch
        ublock = (n_seq, tc, SSM_WIDTH)
    else:
        assert n_chunks == 1
        pitch = tc
        n_bulk, rows_bulk, pitch_bulk = 1, n_seq * tc, n_seq * tc
        ublock = (1, n_seq * tc, SSM_WIDTH)
    scr_rows = 2 * n_bulk * pitch_bulk
    hshape = (n_groups, N_SLABS, SUBLANES, LANES)
    kern = functools.partial(_ssm_kernel, n_seq=n_seq, tc=tc, pitch=pitch,
                             n_bulk=n_bulk, rows_bulk=rows_bulk, pitch_bulk=pitch_bulk)
    return pl.pallas_call(
        kern,
        grid=(n_chunks,),
        in_specs=[pl.BlockSpec(ublock, lambda c: (0, c, 0)),
                  _const_spec(bre.shape), _const_spec(bim.shape),
                  _const_spec(cre.shape), _const_spec(cim.shape),
                  _const_spec(hshape), _const_spec(hshape),
                  _const_spec((1, SSM_WIDTH)), _const_spec((SSM_WIDTH, SSM_WIDTH)),
                  _const_spec(hshape), _const_spec(hshape)],
        out_specs=[pl.BlockSpec(ublock, lambda c: (0, c, 0)),
                   _const_spec(hshape), _const_spec(hshape)],
        out_shape=[jax.ShapeDtypeStruct(u3.shape, BF16),
                   jax.ShapeDtypeStruct(hshape, F32),
                   jax.ShapeDtypeStruct(hshape, F32)],
        scratch_shapes=[pltpu.VMEM((N_SLABS, scr_rows, LANES), F32),
                        pltpu.VMEM((N_SLABS, scr_rows, LANES), F32)],
        compiler_params=_cparams(1),
        name="ssm",
    )(u3, bre, bim, cre, cim, lre_g, lim_g, d_row, glu_b, h0re, h0im)


def _state_to_groups(h, n_seq):
    x = h.reshape(n_seq, 2, N_SLABS, LANES).transpose(1, 0, 2, 3)
    x = x.reshape(2 * n_seq // SUBLANES, SUBLANES, N_SLABS, LANES)
    return x.transpose(0, 2, 1, 3)


def _groups_to_state(x, n_seq):
    x = x.transpose(0, 2, 1, 3).reshape(2, n_seq, N_SLABS, LANES)
    return x.transpose(1, 0, 2, 3).reshape(n_seq, SSM_GROUPS, SSM_STATE)


def _head_slope(h):
    return jnp.where(h == 0, 2.0 ** -2, jnp.where(h == 1, 2.0 ** -4,
                     jnp.where(h == 2, 2.0 ** -6, 2.0 ** -8))).astype(F32)


def _attn_prompt_kernel(lam_ref, q_ref, k_ref, v_ref, sg_ref, o_ref, *, blk, out_scale):
    h = pl.program_id(1)
    qi = pl.program_id(2)
    slope = _head_slope(h)
    lam = lam_ref[0, 0]

    q = q_ref[0]
    lane = lax.broadcasted_iota(jnp.int32, q.shape, 1)
    zero = jnp.zeros_like(q)
    qq = jnp.concatenate([jnp.where(lane < ATTN_HEAD_DIM, q, zero),
                          jnp.where(lane >= ATTN_HEAD_DIM, q, zero)], axis=0)

    col = lax.broadcasted_iota(jnp.int32, (1, blk), 1).astype(F32)

    def scores(j, diagonal):
        colbias = slope * (col + ((j - qi) * blk).astype(F32))
        s = _dot(qq, k_ref[0, j]) + colbias
        if diagonal:
            r = lax.broadcasted_iota(jnp.int32, (blk, blk), 0)
            cidx = lax.broadcasted_iota(jnp.int32, (blk, blk), 1)
            keep = jnp.concatenate([r >= cidx, r >= cidx], axis=0)
            s = jnp.where(keep, s, -jnp.inf)
        return s

    def lane_tiles(x):
        return [x[:, LANES * i:LANES * (i + 1)] for i in range(blk // LANES)]

    def max_step(j, mpart, diagonal):
        for piece in lane_tiles(scores(j, diagonal)):
            mpart = jnp.maximum(mpart, piece)
        return mpart

    mpart = jnp.full((2 * blk, LANES), -jnp.inf, F32)
    mpart = lax.fori_loop(0, qi, lambda j, c: max_step(j, c, False), mpart)
    mpart = max_step(qi, mpart, True)
    m = jnp.max(mpart, axis=-1, keepdims=True)

    def sum_step(j, carry, diagonal):
        lpart, acc = carry
        start = pl.multiple_of(j * blk, blk)
        p = jnp.exp(scores(j, diagonal) - m)
        for piece in lane_tiles(p):
            lpart = lpart + piece
        acc = acc + _dot(p.astype(BF16), v_ref[0, pl.ds(start, blk), :])
        return lpart, acc

    carry = (jnp.zeros((2 * blk, LANES), F32), jnp.zeros((2 * blk, ATTN_V_DIM), F32))
    carry = lax.fori_loop(0, qi, lambda j, c: sum_step(j, c, False), carry)
    lpart, acc = sum_step(qi, carry, True)
    l = jnp.sum(lpart, axis=-1, keepdims=True)
    o = acc[:blk] / l[:blk] - lam * (acc[blk:] / l[blk:])
    o_ref[0] = (_rms(o, sg_ref[...]) * out_scale).astype(o_ref.dtype)


def _attn_prompt(lam, qb, kb, vb, sg, blk, out_scale):
    bsz, t, _ = qb.shape
    return pl.pallas_call(
        functools.partial(_attn_prompt_kernel, blk=blk, out_scale=out_scale),
        grid=(bsz, ATTN_HEADS, t // blk),
        in_specs=[pl.BlockSpec(memory_space=pltpu.SMEM),
                  pl.BlockSpec((1, blk, ATTN_V_DIM), lambda b, h, i: (b, i, h)),
                  pl.BlockSpec((1, t // blk, ATTN_V_DIM, blk), lambda b, h, i: (b, 0, h, 0)),
                  pl.BlockSpec((1, t, ATTN_V_DIM), lambda b, h, i: (b, 0, h)),
                  _const_spec((1, ATTN_V_DIM))],
        out_specs=pl.BlockSpec((1, blk, ATTN_V_DIM), lambda b, h, i: (b, i, h)),
        out_shape=jax.ShapeDtypeStruct(qb.shape, BF16),
        compiler_params=_cparams(3),
        name="attn_prompt",
    )(lam, qb, kb, vb, sg)


def _attn_sample_kernel(pt_ref, lam_ref, q_ref, kn_ref, vn_ref, sg_ref, *refs,
                        n_pages, t_new, past_len, out_scale):
    k_refs = refs[:n_pages]
    v_refs = refs[n_pages:2 * n_pages]
    o_ref = refs[2 * n_pages]
    del pt_ref
    lam = lam_ref[0, 0]
    n_rows = 2 * ATTN_HEADS * t_new

    q4 = q_ref[0]
    qe = jnp.concatenate([q4] * (2 * ATTN_HEADS), axis=0)
    row = lax.broadcasted_iota(jnp.int32, qe.shape, 0)
    lane = lax.broadcasted_iota(jnp.int32, qe.shape, 1)
    qe = jnp.where(lane // ATTN_HEAD_DIM == row // t_new, qe, 0.0)
    qeb = qe.astype(BF16)

    rcol = lax.broadcasted_iota(jnp.int32, (n_rows, 1), 0)
    head_c = rcol // (2 * t_new)
    tok_c = rcol % t_new
    slope = _head_slope(head_c)
    col = lax.broadcasted_iota(jnp.int32, (1, PAGE_SIZE), 1).astype(F32)
    colbias = slope * col

    m = jnp.full((n_rows, 1), NEG_BIG, F32)
    l = jnp.zeros((n_rows, 1), F32)
    acc = jnp.zeros((n_rows, ATTN_WIDTH), F32)
    for p in range(n_pages):
        kp = k_refs[p][0].astype(BF16)
        vp = jnp.concatenate([v_refs[p][0, :, h, :] for h in range(ATTN_HEADS)],
                             axis=1).astype(BF16)
        s = _dot(qeb, kp) + colbias
        off = slope * float(p * PAGE_SIZE)
        mn = jnp.maximum(m, jnp.max(s, axis=-1, keepdims=True) + off)
        pe = jnp.exp(s - (mn - off))
        alpha = jnp.exp(m - mn)
        l = alpha * l + jnp.sum(pe, axis=-1, keepdims=True)
        acc = alpha * acc + _dot(pe.astype(BF16), vp)
        m = mn

    kn = kn_ref[0]
    vn = vn_ref[0]
    s_new = []
    for c in range(t_new):
        sc = jnp.sum(qe * kn[c:c + 1, :], axis=-1, keepdims=True) + slope * float(past_len + c)
        s_new.append(jnp.where(tok_c >= c, sc, -jnp.inf))
    mn = m
    for sc in s_new:
        mn = jnp.maximum(mn, sc)
    alpha = jnp.exp(m - mn)
    l = alpha * l
    acc = alpha * acc
    for c in range(t_new):
        pc = jnp.exp(s_new[c] - mn)
        l = l + pc
        acc = acc + pc * vn[c:c + 1, :]

    acc = acc / l
    outs = []
    for h in range(ATTN_HEADS):
        r0 = 2 * h * t_new
        r1 = r0 + t_new
        a0 = acc[r0:r0 + t_new, ATTN_V_DIM * h:ATTN_V_DIM * (h + 1)]
        a1 = acc[r1:r1 + t_new, ATTN_V_DIM * h:ATTN_V_DIM * (h + 1)]
        outs.append(_rms(a0 - lam * a1, sg_ref[...]) * out_scale)
    o_ref[0] = jnp.concatenate(outs, axis=-1)


def _attn_sample(page_table, lam, q3, kn3, vn3, sg, cache_kt, cache_v4, past_len, out_scale):
    n_dec, t_new, _ = q3.shape
    n_pages = page_table.shape[1]
    pt = page_table.reshape(-1)
    tok_spec = pl.BlockSpec((1, t_new, ATTN_WIDTH), lambda b, pt: (b, 0, 0))

    def k_spec(p):
        return pl.BlockSpec((1, ATTN_WIDTH, PAGE_SIZE),
                            lambda b, pt, p=p: (pt[b * n_pages + p], 0, 0))

    def v_spec(p):
        return pl.BlockSpec((1, PAGE_SIZE, ATTN_HEADS, ATTN_V_DIM),
                            lambda b, pt, p=p: (pt[b * n_pages + p], 0, 0, 0))

    grid_spec = pltpu.PrefetchScalarGridSpec(
        num_scalar_prefetch=1,
        grid=(n_dec,),
        in_specs=[pl.BlockSpec(memory_space=pltpu.SMEM), tok_spec, tok_spec, tok_spec,
                  pl.BlockSpec((1, ATTN_V_DIM), lambda b, pt: (0, 0))]
                 + [k_spec(p) for p in range(n_pages)] + [v_spec(p) for p in range(n_pages)],
        out_specs=tok_spec)
    kern = functools.partial(_attn_sample_kernel, n_pages=n_pages, t_new=t_new,
                             past_len=past_len, out_scale=out_scale)
    return pl.pallas_call(
        kern,
        grid_spec=grid_spec,
        out_shape=jax.ShapeDtypeStruct(q3.shape, F32),
        compiler_params=_cparams(1),
        name="attn_sample",
    )(pt, lam, q3, kn3, vn3, sg, *([cache_kt] * n_pages), *([cache_v4] * n_pages))


def _out_proj_kernel(x_ref, s_ref, o_ref, w1_ref, w2_ref, g2_ref, wq_ref, qg_ref,
                     x1_ref, cq_ref):
    x1 = (x_ref[...] + _dot(s_ref[...].astype(BF16), w1_ref[...])
          + _dot(o_ref[...].astype(BF16), w2_ref[...]))
    x1_ref[...] = x1
    xn = _rms(x1, g2_ref[...]).astype(BF16)
    cq = _dot(xn, wq_ref[...])
    cq_ref[...] = _head_rms(cq, qg_ref[...], CA_HEAD_DIM, CA_HEAD_DIM ** -0.5).astype(cq_ref.dtype)


def _out_proj(x, ssm_o, att_o, w1, w2, g2, wq, qg, tm, cq_dtype):
    n = x.shape[0]
    tok = lambda width: pl.BlockSpec((tm, width), lambda i: (i, 0))
    return pl.pallas_call(
        _out_proj_kernel,
        grid=(n // tm,),
        in_specs=[tok(D_MODEL), tok(SSM_WIDTH), tok(ATTN_WIDTH),
                  _const_spec((SSM_WIDTH, D_MODEL)), _const_spec((ATTN_WIDTH, D_MODEL)),
                  _const_spec((1, D_MODEL)), _const_spec((D_MODEL, D_MODEL)),
                  _const_spec((1, CA_HEAD_DIM))],
        out_specs=[tok(D_MODEL), tok(D_MODEL)],
        out_shape=[jax.ShapeDtypeStruct((n, D_MODEL), F32),
                   jax.ShapeDtypeStruct((n, D_MODEL), cq_dtype)],
        compiler_params=_cparams(1),
        name="out_proj",
    )(x, ssm_o, att_o, w1, w2, g2, wq, qg)


def _mem_kv_kernel(m_ref, g_ref, wk_ref, wv_ref, kg_ref, mk_ref, mv_ref, mkb_ref, mvb_ref):
    mn = _rms(m_ref[...], g_ref[...]).astype(BF16)
    mk = _head_rms(_dot(mn, wk_ref[...]), kg_ref[...], CA_HEAD_DIM, 1.0)
    mv = _dot(mn, wv_ref[...])
    mk_ref[...] = mk
    mv_ref[...] = mv
    mkb_ref[...] = mk.astype(BF16)
    mvb_ref[...] = mv.astype(BF16)


def _mem_kv(mem, g, wk, wv, kg, tm):
    n = mem.shape[0]
    tok = pl.BlockSpec((tm, D_MODEL), lambda i: (i, 0))
    wspec = _const_spec((D_MODEL, D_MODEL))
    return pl.pallas_call(
        _mem_kv_kernel,
        grid=(n // tm,),
        in_specs=[tok, _const_spec((1, D_MODEL)), wspec, wspec, _const_spec((1, CA_HEAD_DIM))],
        out_specs=[tok, tok, tok, tok],
        out_shape=[jax.ShapeDtypeStruct((n, D_MODEL), F32)] * 2
                  + [jax.ShapeDtypeStruct((n, D_MODEL), BF16)] * 2,
        compiler_params=_cparams(1),
        name="mem_kv",
    )(mem, g, wk, wv, kg)


def _cross_prompt_kernel(cq_ref, mk_ref, mv_ref, co_ref):
    cq = cq_ref[...]
    outs = []
    for h in range(CA_HEADS):
        sl = slice(CA_HEAD_DIM * h, CA_HEAD_DIM * (h + 1))
        s = _dot_nt(cq[:, sl], mk_ref[0, :, sl])
        p = jnp.exp(s - jnp.max(s, axis=-1, keepdims=True))
        l = jnp.sum(p, axis=-1, keepdims=True)
        outs.append(_dot((p / l).astype(BF16), mv_ref[0, :, sl]))
    co_ref[...] = jnp.concatenate(outs, axis=-1).astype(co_ref.dtype)


def _cross_prompt(cq, mkb, mvb, tm):
    n = cq.shape[0]
    bsz = mkb.shape[0]
    per_b = n // bsz // tm
    tok = pl.BlockSpec((tm, D_MODEL), lambda b, i: (b * per_b + i, 0))
    mem = pl.BlockSpec((1, N_MEM, D_MODEL), lambda b, i: (b, 0, 0))
    return pl.pallas_call(
        _cross_prompt_kernel,
        grid=(bsz, per_b),
        in_specs=[tok, mem, mem],
        out_specs=tok,
        out_shape=jax.ShapeDtypeStruct((n, D_MODEL), BF16),
        compiler_params=_cparams(2),
        name="cross_prompt",
    )(cq, mkb, mvb)


def _cross_sample_kernel(cq_ref, mk_ref, mv_ref, co_ref, *, t_new):
    n_rows = CA_HEADS * t_new
    q4 = cq_ref[0]
    qe = jnp.concatenate([q4] * CA_HEADS, axis=0)
    row = lax.broadcasted_iota(jnp.int32, qe.shape, 0)
    lane = lax.broadcasted_iota(jnp.int32, qe.shape, 1)
    qe = jnp.where(lane // CA_HEAD_DIM == row // t_new, qe, 0.0).astype(BF16)
    heads = lambda ref: jnp.concatenate(
        [ref[0, :, h, :] for h in range(CA_HEADS)], axis=1).astype(BF16)
    s = _dot_nt(qe, heads(mk_ref))
    p = jnp.exp(s - jnp.max(s, axis=-1, keepdims=True))
    l = jnp.sum(p, axis=-1, keepdims=True)
    full = _dot((p / l).astype(BF16), heads(mv_ref))
    outs = [full[h * t_new:(h + 1) * t_new, CA_HEAD_DIM * h:CA_HEAD_DIM * (h + 1)]
            for h in range(CA_HEADS)]
    co_ref[0] = jnp.concatenate(outs, axis=-1)
    del n_rows


def _cross_sample(cq3, mem_k, mem_v):
    n_dec, t_new, _ = cq3.shape
    tok = pl.BlockSpec((1, t_new, D_MODEL), lambda b: (b, 0, 0))
    mem = pl.BlockSpec((1, N_MEM, CA_HEADS, CA_HEAD_DIM), lambda b: (b, 0, 0, 0))
    return pl.pallas_call(
        functools.partial(_cross_sample_kernel, t_new=t_new),
        grid=(n_dec,),
        in_specs=[tok, mem, mem],
        out_specs=tok,
        out_shape=jax.ShapeDtypeStruct(cq3.shape, F32),
        compiler_params=_cparams(1),
        name="cross_sample",
    )(cq3, mem_k, mem_v)


def _ffn_kernel(x_ref, co_ref, wo_ref, g3_ref, wg_ref, wv_ref, cw_ref, cb_ref, wd_ref, prev_ref,
                y_ref, cs_ref, hbuf, *, tm, shift, pad):
    i = pl.program_id(1)

    @pl.when(i == 0)
    def _():
        hbuf[pad - 2 * shift:pad, :] = prev_ref[0]

    x2 = x_ref[...] + _dot(co_ref[...].astype(BF16), wo_ref[...])
    xn = _rms(x2, g3_ref[...]).astype(BF16)
    hbuf[pad:pad + tm, :] = _dot(xn, wg_ref[...])
    hv = _dot(xn, wv_ref[...])
    cw = cw_ref[...]
    conv = (cb_ref[...] + cw[0:1, :] * hbuf[pad - 2 * shift:pad - 2 * shift + tm, :]
            + cw[1:2, :] * hbuf[pad - shift:pad - shift + tm, :]
            + cw[2:3, :] * hbuf[pad:pad + tm, :])
    act = (conv * jax.nn.sigmoid(conv) * hv).astype(BF16)
    y_ref[...] = x2 + _dot(act, wd_ref[...])
    last = hbuf[pad + tm - 2 * shift:pad + tm, :]
    cs_ref[0] = last
    hbuf[pad - 2 * shift:pad, :] = last


def _ffn(x1, co, wo, g3, wg, wv, cw, cb, wd, prev, tm, shift):
    n = x1.shape[0]
    n_seq_blocks = prev.shape[0]
    per_b = n // n_seq_blocks // tm
    pad = -(-2 * shift // SUBLANES) * SUBLANES
    tok = pl.BlockSpec((tm, D_MODEL), lambda b, i: (b * per_b + i, 0))
    state = pl.BlockSpec((1, 2 * shift, FFN_HIDDEN), lambda b, i: (b, 0, 0))
    return pl.pallas_call(
        functools.partial(_ffn_kernel, tm=tm, shift=shift, pad=pad),
        grid=(n_seq_blocks, per_b),
        in_specs=[tok, tok, _const_spec((D_MODEL, D_MODEL)), _const_spec((1, D_MODEL)),
                  _const_spec((D_MODEL, FFN_HIDDEN)), _const_spec((D_MODEL, FFN_HIDDEN)),
                  _const_spec((CONV_WIDTH, FFN_HIDDEN)), _const_spec((1, FFN_HIDDEN)),
                  _const_spec((FFN_HIDDEN, D_MODEL)), state],
        out_specs=[tok, state],
        out_shape=[jax.ShapeDtypeStruct((n, D_MODEL), F32),
                   jax.ShapeDtypeStruct(prev.shape, F32)],
        scratch_shapes=[pltpu.VMEM((pad + tm, FFN_HIDDEN), F32)],
        compiler_params=_cparams(2),
        name="ffn",
    )(x1, co, wo, g3, wg, wv, cw, cb, wd, prev)


def _block_diag(blocks, n_outer):
    n, r, c = blocks.shape
    per = n // n_outer
    b = blocks.reshape(n_outer, per, r, c)
    eye = jnp.eye(per, dtype=blocks.dtype)
    full = b[:, :, :, None, :] * eye[None, :, None, :, None]
    return full.reshape(n_outer, per * r, per * c)


def _lam_groups(lb, n_seq):
    halves = lb.reshape(2, 1, N_SLABS, LANES)
    rows = jnp.broadcast_to(halves, (2, n_seq, N_SLABS, LANES))
    rows = rows.reshape(2 * n_seq // SUBLANES, SUBLANES, N_SLABS, LANES)
    return rows.transpose(0, 2, 1, 3)


def _layer(l, x_prompt, x_sample, mem_prompt, cache_k, cache_v, page_table,
           state_ssm_re, state_ssm_im, state_conv, cache_mem_k, cache_mem_v, p):
    bsz, seq, _ = x_prompt.shape
    n_dec, t_new, _ = x_sample.shape
    n_p = bsz * seq
    n_s = n_dec * t_new
    past_len = page_table.shape[1] * PAGE_SIZE
    lam0 = 0.8 - 0.6 * math.exp(-0.3 * l)
    out_scale = 1.0 - lam0
    row = lambda v: v.reshape(1, -1).astype(F32)
    bf = lambda w: w.astype(BF16)

    lb_re, lb_im, bb_re, bb_im, lam = _prep(
        p['ssm_a_re'], p['ssm_a_im'], p['ssm_log_dt'], p['ssm_b_re'], p['ssm_b_im'],
        p['lam_q1'], p['lam_k1'], p['lam_q2'], p['lam_k2'], lam0)

    def b_mats(bbt):
        blocks = bbt.reshape(SSM_GROUP_CH, SSM_GROUPS, SSM_STATE).transpose(1, 0, 2)
        return bf(_block_diag(blocks, 4))

    def c_mats(cm):
        return bf(_block_diag(cm.transpose(0, 2, 1), 4))

    mats = (b_mats(bb_re), b_mats(bb_im), c_mats(p['ssm_c_re']), c_mats(p['ssm_c_im']))
    d_row = row(p['ssm_d'])
    glu_b = bf(p['ssm_glu_w'])

    w_in = p['w_in']
    k0, k1 = SSM_WIDTH + ATTN_WIDTH, SSM_WIDTH + 2 * ATTN_WIDTH
    w_uqv = bf(jnp.concatenate([w_in[:, :k0], w_in[:, k1:]], axis=1))
    w_k = bf(w_in[:, k0:k1])
    seg = jnp.kron(jnp.eye(ATTN_WIDTH // ATTN_HEAD_DIM, dtype=F32),
                   jnp.ones((ATTN_HEAD_DIM, ATTN_HEAD_DIM), F32)).astype(BF16)
    qg = jnp.tile(p['q_norm_g'], ATTN_WIDTH // ATTN_HEAD_DIM).reshape(1, ATTN_WIDTH)
    kg = jnp.tile(p['k_norm_g'], ATTN_WIDTH // ATTN_HEAD_DIM).reshape(1, ATTN_WIDTH)
    sg = row(p['subln_g'])
    w_out_b = bf(p['w_out'])
    w1, w2 = w_out_b[:SSM_WIDTH], w_out_b[SSM_WIDTH:]
    wq_b, wk_b, wv_b, wo_b = bf(p['ca_wq']), bf(p['ca_wk']), bf(p['ca_wv']), bf(p['ca_wo'])
    wg_b, wvf_b, wd_b = bf(p['ffn_wg']), bf(p['ffn_wv']), bf(p['ffn_wd'])
    cw = p['ffn_conv_w'].astype(F32)
    cb = row(p['ffn_conv_b'])

    xp = x_prompt.reshape(n_p, D_MODEL)
    attn_blk = 256
    kg_col = jnp.broadcast_to(p['k_norm_g'].reshape(ATTN_HEAD_DIM, 1), (ATTN_HEAD_DIM, 512))
    u_p, q_p, kt_p, v_p, kb_p, vb_p = _in_proj(xp, row(p['ln1_g']), w_uqv, w_k.T, seg, qg, kg_col,
                                               512, BF16, n_batch=bsz, blk=attn_blk)
    zeros_h = jnp.zeros((2 * bsz // SUBLANES, N_SLABS, SUBLANES, LANES), F32)
    ssm_p, hre_p, him_p = _ssm(u_p.reshape(bsz, seq, SSM_WIDTH), mats,
                               (_lam_groups(lb_re, bsz), _lam_groups(lb_im, bsz)),
                               d_row, glu_b, (zeros_h, zeros_h), bsz, 256)
    att_p = _attn_prompt(lam, q_p.reshape(bsz, seq, ATTN_WIDTH), kb_p,
                         vb_p.reshape(bsz, seq, ATTN_WIDTH), sg, attn_blk, out_scale)
    x1_p, cq_p = _out_proj(xp, ssm_p.reshape(n_p, SSM_WIDTH), att_p.reshape(n_p, ATTN_WIDTH),
                           w1, w2, row(p['ln2_g']), wq_b, row(p['ca_q_norm_g']), 512, BF16)
    mk, mv, mkb, mvb = _mem_kv(mem_prompt.reshape(bsz * N_MEM, D_MODEL), row(p['mem_norm_g']),
                               wk_b, wv_b, row(p['ca_k_norm_g']), 256)
    co_p = _cross_prompt(cq_p, mkb.reshape(bsz, N_MEM, D_MODEL), mvb.reshape(bsz, N_MEM, D_MODEL), 512)
    y_p, conv_p = _ffn(x1_p, co_p, wo_b, row(p['ln3_g']), wg_b, wvf_b, cw, cb, wd_b,
                       jnp.zeros((bsz, CONV_WIDTH - 1, FFN_HIDDEN), F32), 512, 1)

    xs = x_sample.reshape(n_s, D_MODEL)
    u_s, q_s, k_s, v_s = _in_proj(xs, row(p['ln1_g']), w_uqv, w_k, seg, qg, kg, n_s, F32)
    h0 = (_state_to_groups(state_ssm_re.reshape(n_dec, N_STATE), n_dec),
          _state_to_groups(state_ssm_im.reshape(n_dec, N_STATE), n_dec))
    ssm_s, hre_s, him_s = _ssm(u_s.reshape(1, n_s, SSM_WIDTH), mats,
                               (_lam_groups(lb_re, n_dec), _lam_groups(lb_im, n_dec)),
                               d_row, glu_b, h0, n_dec, t_new)
    n_phys = cache_k.shape[0]
    cache_kt = cache_k.transpose(0, 2, 3, 4, 1).reshape(n_phys, ATTN_WIDTH, PAGE_SIZE)
    att_s = _attn_sample(page_table, lam, q_s.reshape(n_dec, t_new, ATTN_WIDTH),
                         k_s.reshape(n_dec, t_new, ATTN_WIDTH), v_s.reshape(n_dec, t_new, ATTN_WIDTH),
                         sg, cache_kt, cache_v, past_len, out_scale)
    x1_s, cq_s = _out_proj(xs, ssm_s.reshape(n_s, SSM_WIDTH), att_s.reshape(n_s, ATTN_WIDTH),
                           w1, w2, row(p['ln2_g']), wq_b, row(p['ca_q_norm_g']), n_s, F32)
    co_s = _cross_sample(cq_s.reshape(n_dec, t_new, D_MODEL), cache_mem_k, cache_mem_v)
    tmaj = lambda a: a.reshape(n_dec, t_new, -1).transpose(1, 0, 2).reshape(n_s, -1)
    prev_s = state_conv.transpose(1, 0, 2).reshape(1, (CONV_WIDTH - 1) * n_dec, FFN_HIDDEN)
    y_s, conv_s = _ffn(tmaj(x1_s), tmaj(co_s), wo_b, row(p['ln3_g']), wg_b, wvf_b, cw, cb, wd_b,
                       prev_s, n_s, n_dec)
    y_s = y_s.reshape(t_new, n_dec, D_MODEL).transpose(1, 0, 2)
    conv_s = conv_s.reshape(CONV_WIDTH - 1, n_dec, FFN_HIDDEN).transpose(1, 0, 2)

    return (y_p.reshape(bsz, seq, D_MODEL), y_s,
            kt_p.reshape(bsz, ATTN_HEADS, 2, ATTN_HEAD_DIM, seq).transpose(0, 4, 1, 2, 3),
            v_p.reshape(bsz, seq, ATTN_HEADS, ATTN_V_DIM),
            k_s.reshape(n_dec, t_new, ATTN_HEADS, 2, ATTN_HEAD_DIM),
            v_s.reshape(n_dec, t_new, ATTN_HEADS, ATTN_V_DIM),
            _groups_to_state(hre_p, bsz), _groups_to_state(him_p, bsz),
            _groups_to_state(hre_s, n_dec), _groups_to_state(him_s, n_dec),
            conv_p, conv_s,
            mk.reshape(bsz, N_MEM, CA_HEADS, CA_HEAD_DIM), mv.reshape(bsz, N_MEM, CA_HEADS, CA_HEAD_DIM))


def kernel(x_prompt, x_sample, mem_prompt, cache_k, cache_v, page_table, state_ssm_re, state_ssm_im, state_conv, cache_mem_k, cache_mem_v, ln1_g, w_in, ssm_a_re, ssm_a_im, ssm_b_re, ssm_b_im, ssm_c_re, ssm_c_im, ssm_d, ssm_log_dt, ssm_glu_w, q_norm_g, k_norm_g, lam_q1, lam_k1, lam_q2, lam_k2, subln_g, w_out, ln2_g, mem_norm_g, ca_wq, ca_wk, ca_wv, ca_q_norm_g, ca_k_norm_g, ca_wo, ln3_g, ffn_wg, ffn_wv, ffn_conv_w, ffn_conv_b, ffn_wd):
    params = dict(
        ln1_g=ln1_g, w_in=w_in, ssm_a_re=ssm_a_re, ssm_a_im=ssm_a_im, ssm_b_re=ssm_b_re,
        ssm_b_im=ssm_b_im, ssm_c_re=ssm_c_re, ssm_c_im=ssm_c_im, ssm_d=ssm_d,
        ssm_log_dt=ssm_log_dt, ssm_glu_w=ssm_glu_w, q_norm_g=q_norm_g, k_norm_g=k_norm_g,
        lam_q1=lam_q1, lam_k1=lam_k1, lam_q2=lam_q2, lam_k2=lam_k2, subln_g=subln_g,
        w_out=w_out, ln2_g=ln2_g, mem_norm_g=mem_norm_g, ca_wq=ca_wq, ca_wk=ca_wk, ca_wv=ca_wv,
        ca_q_norm_g=ca_q_norm_g, ca_k_norm_g=ca_k_norm_g, ca_wo=ca_wo, ln3_g=ln3_g,
        ffn_wg=ffn_wg, ffn_wv=ffn_wv, ffn_conv_w=ffn_conv_w, ffn_conv_b=ffn_conv_b, ffn_wd=ffn_wd)
    y_p, y_s = x_prompt, x_sample
    per_layer = []
    for l in range(w_in.shape[0]):
        p = {k: v[l] for k, v in params.items()}
        outs = _layer(l, y_p, y_s, mem_prompt, cache_k[l], cache_v[l], page_table,
                      state_ssm_re[l], state_ssm_im[l], state_conv[l],
                      cache_mem_k[l], cache_mem_v[l], p)
        y_p, y_s = outs[0], outs[1]
        per_layer.append(outs[2:])
    return (y_p, y_s) + tuple(jnp.stack(leaf) for leaf in zip(*per_layer))
```

```python
import functools
import math

import jax
import jax.numpy as jnp
from jax import lax
from jax.experimental import pallas as pl
from jax.experimental.pallas import tpu as pltpu

F32 = jnp.float32
BF16 = jnp.bfloat16

D_MODEL = 1024
SSM_WIDTH = 512
SSM_GROUP_CH = 16
SSM_GROUPS = 32
SSM_STATE = 64
N_STATE = SSM_GROUPS * SSM_STATE
ATTN_WIDTH = 512
ATTN_HEAD_DIM = 64
ATTN_V_DIM = 128
ATTN_HEADS = 4
PAGE_SIZE = 128
N_MEM = 256
CA_HEADS = 4
CA_HEAD_DIM = 256
FFN_HIDDEN = 2816
CONV_WIDTH = 3
NORM_EPS = 1e-6

LANES = 128
SUBLANES = 8
VMEM_LIMIT = 56 * 1024 * 1024

NEG_BIG = -1e30
LOG2E = math.log2(math.e)


def _cparams(n_axes):
    return pltpu.CompilerParams(
        dimension_semantics=("arbitrary",) * n_axes,
        vmem_limit_bytes=VMEM_LIMIT)


def _const_spec(shape):
    nd = len(shape)
    return pl.BlockSpec(shape, lambda *_: (0,) * nd, pipeline_mode=pl.Buffered(1))


def _rms(x, g):
    return x * lax.rsqrt(jnp.mean(x * x, axis=-1, keepdims=True) + NORM_EPS) * g


def _dot(a, b):
    return jnp.dot(a, b, preferred_element_type=F32)


def _dot_nt(a, b):
    return lax.dot_general(a, b, (((1,), (1,)), ((), ())), preferred_element_type=F32)


def _head_rms(x, g, width, scale):
    outs = []
    for h in range(x.shape[-1] // width):
        c = x[:, h * width:(h + 1) * width]
        outs.append(_rms(c, g) * scale)
    return jnp.concatenate(outs, axis=-1)


def _prep_kernel(are_ref, aim_ref, ldt_ref, bre_ref, bim_ref,
                 q1_ref, k1_ref, q2_ref, k2_ref,
                 lbre_ref, lbim_ref, bbre_ref, bbim_ref, lam_ref, *, lam0):
    a_re = are_ref[...]
    a_im = aim_ref[...]
    dt = jnp.exp(ldt_ref[...])
    mag = jnp.exp(a_re * dt)
    lb_re = mag * jnp.cos(a_im * dt)
    lb_im = mag * jnp.sin(a_im * dt)
    den = a_re * a_re + a_im * a_im
    n_re = lb_re - 1.0
    f_re = (n_re * a_re + lb_im * a_im) / den
    f_im = (lb_im * a_re - n_re * a_im) / den
    b_re = bre_ref[...]
    b_im = bim_ref[...]
    lbre_ref[...] = lb_re
    lbim_ref[...] = lb_im
    bbre_ref[...] = f_re * b_re - f_im * b_im
    bbim_ref[...] = f_re * b_im + f_im * b_re
    s1 = jnp.sum(q1_ref[...] * k1_ref[...], axis=-1, keepdims=True)
    s2 = jnp.sum(q2_ref[...] * k2_ref[...], axis=-1, keepdims=True)
    lam_ref[...] = jnp.exp(s1) - jnp.exp(s2) + lam0


def _prep(a_re, a_im, log_dt, b_re, b_im, q1, k1, q2, k2, lam0):
    are = a_re.reshape(1, N_STATE)
    aim = a_im.reshape(1, N_STATE)
    ldt = jnp.repeat(log_dt, SSM_STATE).reshape(1, N_STATE)
    bre = b_re.transpose(2, 0, 1).reshape(SSM_GROUP_CH, N_STATE)
    bim = b_im.transpose(2, 0, 1).reshape(SSM_GROUP_CH, N_STATE)
    vec = lambda v: v.reshape(1, ATTN_HEAD_DIM)
    row = jax.ShapeDtypeStruct((1, N_STATE), F32)
    mat = jax.ShapeDtypeStruct((SSM_GROUP_CH, N_STATE), F32)
    return pl.pallas_call(
        functools.partial(_prep_kernel, lam0=lam0),
        out_shape=(row, row, mat, mat, jax.ShapeDtypeStruct((1, 1), F32)),
        name="prep",
    )(are, aim, ldt, bre, bim, vec(q1), vec(k1), vec(q2), vec(k2))


def _in_proj_kernel(x_ref, g_ref, w_ref, wk_ref, seg_ref, qg_ref, kg_ref,
                    u_ref, q_ref, k_ref, v_ref, *bf_refs, key_major, blk):
    xn = _rms(x_ref[...], g_ref[...]).astype(BF16)
    proj = _dot(xn, w_ref[...])
    seg = seg_ref[...]

    def chunk_norm(z, g):
        ms = _dot((z * z).astype(BF16), seg) * (1.0 / ATTN_HEAD_DIM)
        return z * lax.rsqrt(ms + NORM_EPS) * g

    u_ref[...] = proj[:, :SSM_WIDTH]
    q = proj[:, SSM_WIDTH:SSM_WIDTH + ATTN_WIDTH]
    v = proj[:, SSM_WIDTH + ATTN_WIDTH:]
    q_ref[...] = (chunk_norm(q, qg_ref[...]) * (ATTN_HEAD_DIM ** -0.5 * LOG2E)).astype(q_ref.dtype)
    if key_major:
        kb_ref, vb_ref = bf_refs
        tm = x_ref.shape[0]
        z = _dot_nt(wk_ref[...], xn).reshape(ATTN_WIDTH // ATTN_HEAD_DIM, ATTN_HEAD_DIM, tm)
        ms = jnp.mean(z * z, axis=1, keepdims=True)
        kn = (z * lax.rsqrt(ms + NORM_EPS) * kg_ref[...]).reshape(ATTN_WIDTH, tm)
        k_ref[0] = kn
        for i in range(tm // blk):
            kb_ref[0, i] = kn[:, i * blk:(i + 1) * blk].astype(BF16)
        for h in range(ATTN_HEADS):
            v_ref[:, h, :] = v[:, ATTN_V_DIM * h:ATTN_V_DIM * (h + 1)]
        vb_ref[...] = v.astype(BF16)
    else:
        k_ref[...] = chunk_norm(_dot(xn, wk_ref[...]), kg_ref[...])
        v_ref[...] = v


def _in_proj(x, ln1_g, w_uqv, wk, seg, qg, kg, tm, q_dtype, n_batch=None, blk=None):
    n = x.shape[0]
    key_major = n_batch is not None
    bsz = n_batch if key_major else 1
    per_b = n // bsz // tm
    tok = lambda width: pl.BlockSpec((tm, width), lambda b, i: (b * per_b + i, 0))
    out_specs = [tok(SSM_WIDTH), tok(ATTN_WIDTH)]
    out_shape = [jax.ShapeDtypeStruct((n, SSM_WIDTH), F32),
                 jax.ShapeDtypeStruct((n, ATTN_WIDTH), q_dtype)]
    if key_major:
        t = n // bsz
        out_specs += [pl.BlockSpec((1, ATTN_WIDTH, tm), lambda b, i: (b, 0, i)),
                      pl.BlockSpec((tm, ATTN_HEADS, ATTN_V_DIM), lambda b, i: (b * per_b + i, 0, 0)),
                      pl.BlockSpec((1, tm // blk, ATTN_WIDTH, blk), lambda b, i: (b, i, 0, 0)),
                      tok(ATTN_WIDTH)]
        out_shape += [jax.ShapeDtypeStruct((bsz, ATTN_WIDTH, t), F32),
                      jax.ShapeDtypeStruct((n, ATTN_HEADS, ATTN_V_DIM), F32),
                      jax.ShapeDtypeStruct((bsz, t // blk, ATTN_WIDTH, blk), BF16),
                      jax.ShapeDtypeStruct((n, ATTN_WIDTH), BF16)]
    else:
        out_specs += [tok(ATTN_WIDTH), tok(ATTN_WIDTH)]
        out_shape += [jax.ShapeDtypeStruct((n, ATTN_WIDTH), F32)] * 2
    return pl.pallas_call(
        functools.partial(_in_proj_kernel, key_major=key_major, blk=blk),
        grid=(bsz, per_b),
        in_specs=[tok(D_MODEL), _const_spec((1, D_MODEL)), _const_spec(w_uqv.shape),
                  _const_spec(wk.shape), _const_spec((ATTN_WIDTH, ATTN_WIDTH)),
                  _const_spec((1, ATTN_WIDTH)), _const_spec(kg.shape)],
        out_specs=out_specs,
        out_shape=out_shape,
        compiler_params=_cparams(2),
        name="in_proj",
    )(x, ln1_g, w_uqv, wk, seg, qg, kg)


N_SLABS = N_STATE // 2 // LANES
HALF = N_STATE // 2


def _gelu_tanh(x):
    c = math.sqrt(2.0 / math.pi)
    return 0.5 * x * (1.0 + jnp.tanh(c * (x + 0.044715 * (x * x * x))))


def _ssm_kernel(u_ref, bre_ref, bim_ref, cre_ref, cim_ref, lre_ref, lim_ref,
                d_ref, glu_ref, h0re_ref, h0im_ref,
                out_ref, hre_ref, him_ref, sre, sim,
                *, n_seq, tc, pitch, n_bulk, rows_bulk, pitch_bulk):
    c = pl.program_id(0)
    rows = n_seq * tc
    n_groups = 2 * n_seq // SUBLANES

    @pl.when(c == 0)
    def _():
        hre_ref[...] = h0re_ref[...]
        him_ref[...] = h0im_ref[...]

    u = u_ref[...].reshape(rows, SSM_WIDTH)
    ub = u.astype(BF16)

    for j in range(4):
        hh, jj = divmod(j, 2)
        uj = ub[:, LANES * j:LANES * (j + 1)]
        for src, dst in ((bre_ref, sre), (bim_ref, sim)):
            bu = _dot(uj, src[j])
            for kk in range(4):
                slab = 4 * jj + kk
                for bb in range(n_bulk):
                    r0 = (hh * n_bulk + bb) * pitch_bulk
                    dst[slab, r0:r0 + rows_bulk, :] = (
                        bu[bb * rows_bulk:(bb + 1) * rows_bulk, LANES * kk:LANES * (kk + 1)])

    def group_body(g, carry):
        base = g * (SUBLANES * pitch)
        lr = [lre_ref[g, k] for k in range(N_SLABS)]
        li = [lim_ref[g, k] for k in range(N_SLABS)]
        hr0 = tuple(hre_ref[g, k] for k in range(N_SLABS))
        hi0 = tuple(him_ref[g, k] for k in range(N_SLABS))

        def step(t, hc):
            hr, hi = hc
            idx = pl.ds(base + t, SUBLANES, stride=pitch)
            nr, ni = [], []
            for k in range(N_SLABS):
                br = sre[k, idx, :]
                bi = sim[k, idx, :]
                r = lr[k] * hr[k] - li[k] * hi[k] + br
                i = lr[k] * hi[k] + li[k] * hr[k] + bi
                sre[k, idx, :] = r
                sim[k, idx, :] = i
                nr.append(r)
                ni.append(i)
            return tuple(nr), tuple(ni)

        hr, hi = lax.fori_loop(0, tc, step, (hr0, hi0), unroll=min(tc, 4))
        for k in range(N_SLABS):
            hre_ref[g, k] = hr[k]
            him_ref[g, k] = hi[k]
        return carry

    lax.fori_loop(0, n_groups, group_body, 0)

    y_rows = []
    for bb in range(n_bulk):
        y_cols = []
        for j in range(4):
            hh, jj = divmod(j, 2)
            r0 = (hh * n_bulk + bb) * pitch_bulk
            hr = jnp.concatenate(
                [sre[4 * jj + kk, r0:r0 + rows_bulk, :] for kk in range(4)], axis=1).astype(BF16)
            hi = jnp.concatenate(
                [sim[4 * jj + kk, r0:r0 + rows_bulk, :] for kk in range(4)], axis=1).astype(BF16)
            y_cols.append(_dot(hr, cre_ref[j]) - _dot(hi, cim_ref[j]))
        y_rows.append(jnp.concatenate(y_cols, axis=1))
    y = jnp.concatenate(y_rows, axis=0) if n_bulk > 1 else y_rows[0]
    y = y + d_ref[...] * u
    gl = _gelu_tanh(y)
    gate = jax.nn.sigmoid(_dot(gl.astype(BF16), glu_ref[...]))
    out_ref[...] = (gl * gate).astype(out_ref.dtype).reshape(out_ref.shape)


def _ssm(u3, mats, lam_g, d_row, glu_b, h0, n_seq, tc):
    bre, bim, cre, cim = mats
    lre_g, lim_g = lam_g
    h0re, h0im = h0
    n_groups = 2 * n_seq // SUBLANES
    t_total = u3.shape[0] * u3.shape[1] // n_seq
    n_chunks = t_total // tc
    if tc % SUBLANES == 0:
        pitch = tc + SUBLANES
        n_bulk, rows_bulk, pitch_bulk = n_seq, tc, pitch
        ublock = (n_seq, tc, SSM_WIDTH)
    else:
        assert n_chunks == 1
        pitch = tc
        n_bulk, rows_bulk, pitch_bulk = 1, n_seq * tc, n_seq * tc
        ublock = (1, n_seq * tc, SSM_WIDTH)
    scr_rows = 2 * n_bulk * pitch_bulk
    hshape = (n_groups, N_SLABS, SUBLANES, LANES)
    kern = functools.partial(_ssm_kernel, n_seq=n_seq, tc=tc, pitch=pitch,
                             n_bulk=n_bulk, rows_bulk=rows_bulk, pitch_bulk=pitch_bulk)
    return pl.pallas_call(
        kern,
        grid=(n_chunks,),
        in_specs=[pl.BlockSpec(ublock, lambda c: (0, c, 0)),
                  _const_spec(bre.shape), _const_spec(bim.shape),
                  _const_spec(cre.shape), _const_spec(cim.shape),
                  _const_spec(hshape), _const_spec(hshape),
                  _const_spec((1, SSM_WIDTH)), _const_spec((SSM_WIDTH, SSM_WIDTH)),
                  _const_spec(hshape), _const_spec(hshape)],
        out_specs=[pl.BlockSpec(ublock, lambda c: (0, c, 0)),
                   _const_spec(hshape), _const_spec(hshape)],
        out_shape=[jax.ShapeDtypeStruct(u3.shape, BF16),
                   jax.ShapeDtypeStruct(hshape, F32),
                   jax.ShapeDtypeStruct(hshape, F32)],
        scratch_shapes=[pltpu.VMEM((N_SLABS, scr_rows, LANES), F32),
                        pltpu.VMEM((N_SLABS, scr_rows, LANES), F32)],
        compiler_params=_cparams(1),
        name="ssm",
    )(u3, bre, bim, cre, cim, lre_g, lim_g, d_row, glu_b, h0re, h0im)


def _state_to_groups(h, n_seq):
    x = h.reshape(n_seq, 2, N_SLABS, LANES).transpose(1, 0, 2, 3)
    x = x.reshape(2 * n_seq // SUBLANES, SUBLANES, N_SLABS, LANES)
    return x.transpose(0, 2, 1, 3)


def _groups_to_state(x, n_seq):
    x = x.transpose(0, 2, 1, 3).reshape(2, n_seq, N_SLABS, LANES)
    return x.transpose(1, 0, 2, 3).reshape(n_seq, SSM_GROUPS, SSM_STATE)


def _head_slope(h):
    return jnp.where(h == 0, 2.0 ** -2, jnp.where(h == 1, 2.0 ** -4,
                     jnp.where(h == 2, 2.0 ** -6, 2.0 ** -8))).astype(F32)


def _attn_prompt_kernel(lam_ref, q_ref, k_ref, v_ref, sg_ref, o_ref,
                        q_scr, s_scr, m_scr, l_scr, acc_scr, *, bq, bk, out_scale):
    h = pl.program_id(1)
    qi = pl.program_id(2)
    slope = _head_slope(h) * LOG2E
    lam = lam_ref[0, 0]
    n_maps = 2

    q = q_ref[0]
    lane = lax.broadcasted_iota(jnp.int32, q.shape, 1)
    zero = jnp.zeros_like(q)
    q_scr[0] = jnp.where(lane < ATTN_HEAD_DIM, q, zero)
    q_scr[1] = jnp.where(lane >= ATTN_HEAD_DIM, q, zero)

    col = lax.broadcasted_iota(jnp.int32, (1, bk), 1)
    q0 = qi * bq

    def produce(j, masked):
        kpos = col + j * bk
        colbias = slope * (kpos - q0).astype(F32)
        kt = k_ref[0, j]
        for i in range(n_maps):
            s = _dot(q_scr[i], kt) + colbias
            if masked:
                qpos = lax.broadcasted_iota(jnp.int32, (bq, bk), 0) + q0
                s = jnp.where(qpos >= kpos, s, -jnp.inf)
            s_scr[i] = s

    def lane_tiles(x):
        return [x[:, LANES * i:LANES * (i + 1)] for i in range(bk // LANES)]

    n_full = q0 // bk

    def run_pass(consume):
        produce(0, True)

        def body(j, carry):
            consume(j)
            produce(j + 1, False)
            return carry

        lax.fori_loop(0, n_full - 1, body, 0)

        @pl.when(n_full >= 1)
        def _():
            consume(n_full - 1)
            produce(n_full, True)

        consume(n_full)

    def consume_max(j):
        del j
        for i in range(n_maps):
            mpart = m_scr[i]
            for piece in lane_tiles(s_scr[i]):
                mpart = jnp.maximum(mpart, piece)
            m_scr[i] = mpart

    m_scr[...] = jnp.full(m_scr.shape, -jnp.inf, F32)
    run_pass(consume_max)
    ms = [jnp.max(m_scr[i], axis=-1, keepdims=True) for i in range(n_maps)]

    def consume_sum(j):
        start = pl.multiple_of(j * bk, bk)
        vb = v_ref[0, pl.ds(start, bk), :]
        for i in range(n_maps):
            p = jnp.exp2(s_scr[i] - ms[i])
            lpart = l_scr[i]
            for piece in lane_tiles(p):
                lpart = lpart + piece
            l_scr[i] = lpart
            acc_scr[i] += _dot(p.astype(BF16), vb)

    l_scr[...] = jnp.zeros(l_scr.shape, F32)
    acc_scr[...] = jnp.zeros(acc_scr.shape, F32)
    run_pass(consume_sum)
    outs = [acc_scr[i] / jnp.sum(l_scr[i], axis=-1, keepdims=True) for i in range(n_maps)]
    o = outs[0] - lam * outs[1]
    o_ref[0] = (_rms(o, sg_ref[...]) * out_scale).astype(o_ref.dtype)


def _attn_prompt(lam, qb, kb, vb, sg, bq, out_scale):
    bsz, t, _ = qb.shape
    bk = kb.shape[-1]
    return pl.pallas_call(
        functools.partial(_attn_prompt_kernel, bq=bq, bk=bk, out_scale=out_scale),
        grid=(bsz, ATTN_HEADS, t // bq),
        in_specs=[pl.BlockSpec(memory_space=pltpu.SMEM),
                  pl.BlockSpec((1, bq, ATTN_V_DIM), lambda b, h, i: (b, i, h)),
                  pl.BlockSpec((1, t // bk, ATTN_V_DIM, bk), lambda b, h, i: (b, 0, h, 0)),
                  pl.BlockSpec((1, t, ATTN_V_DIM), lambda b, h, i: (b, 0, h)),
                  _const_spec((1, ATTN_V_DIM))],
        out_specs=pl.BlockSpec((1, bq, ATTN_V_DIM), lambda b, h, i: (b, i, h)),
        out_shape=jax.ShapeDtypeStruct(qb.shape, BF16),
        scratch_shapes=[pltpu.VMEM((2, bq, ATTN_V_DIM), BF16),
                        pltpu.VMEM((2, bq, bk), F32),
                        pltpu.VMEM((2, bq, LANES), F32),
                        pltpu.VMEM((2, bq, LANES), F32),
                        pltpu.VMEM((2, bq, ATTN_V_DIM), F32)],
        compiler_params=_cparams(3),
        name="attn_prompt",
    )(lam, qb, kb, vb, sg)


def _attn_sample_kernel(pt_ref, lam_ref, q_ref, kn_ref, vn_ref, sg_ref, *refs,
                        n_pages, t_new, past_len, out_scale):
    k_refs = refs[:n_pages]
    v_refs = refs[n_pages:2 * n_pages]
    o_ref = refs[2 * n_pages]
    del pt_ref
    lam = lam_ref[0, 0]
    n_rows = 2 * ATTN_HEADS * t_new

    q4 = q_ref[0]
    qe = jnp.concatenate([q4] * (2 * ATTN_HEADS), axis=0)
    row = lax.broadcasted_iota(jnp.int32, qe.shape, 0)
    lane = lax.broadcasted_iota(jnp.int32, qe.shape, 1)
    qe = jnp.where(lane // ATTN_HEAD_DIM == row // t_new, qe, 0.0)
    qeb = qe.astype(BF16)

    rcol = lax.broadcasted_iota(jnp.int32, (n_rows, 1), 0)
    head_c = rcol // (2 * t_new)
    tok_c = rcol % t_new
    slope = _head_slope(head_c) * LOG2E
    col = lax.broadcasted_iota(jnp.int32, (1, PAGE_SIZE), 1).astype(F32)
    colbias = slope * col

    m = jnp.full((n_rows, 1), NEG_BIG, F32)
    l = jnp.zeros((n_rows, 1), F32)
    acc = jnp.zeros((n_rows, ATTN_WIDTH), F32)
    for p in range(n_pages):
        kp = k_refs[p][0].astype(BF16)
        vp = jnp.concatenate([v_refs[p][0, :, h, :] for h in range(ATTN_HEADS)],
                             axis=1).astype(BF16)
        s = _dot(qeb, kp) + colbias
        off = slope * float(p * PAGE_SIZE)
        mn = jnp.maximum(m, jnp.max(s, axis=-1, keepdims=True) + off)
        pe = jnp.exp2(s - (mn - off))
        alpha = jnp.exp2(m - mn)
        l = alpha * l + jnp.sum(pe, axis=-1, keepdims=True)
        acc = alpha * acc + _dot(pe.astype(BF16), vp)
        m = mn

    kn = kn_ref[0]
    vn = vn_ref[0]
    s_new = []
    for c in range(t_new):
        sc = jnp.sum(qe * kn[c:c + 1, :], axis=-1, keepdims=True) + slope * float(past_len + c)
        s_new.append(jnp.where(tok_c >= c, sc, -jnp.inf))
    mn = m
    for sc in s_new:
        mn = jnp.maximum(mn, sc)
    alpha = jnp.exp2(m - mn)
    l = alpha * l
    acc = alpha * acc
    for c in range(t_new):
        pc = jnp.exp2(s_new[c] - mn)
        l = l + pc
        acc = acc + pc * vn[c:c + 1, :]

    acc = acc / l
    outs = []
    for h in range(ATTN_HEADS):
        r0 = 2 * h * t_new
        r1 = r0 + t_new
        a0 = acc[r0:r0 + t_new, ATTN_V_DIM * h:ATTN_V_DIM * (h + 1)]
        a1 = acc[r1:r1 + t_new, ATTN_V_DIM * h:ATTN_V_DIM * (h + 1)]
        outs.append(_rms(a0 - lam * a1, sg_ref[...]) * out_scale)
    o_ref[0] = jnp.concatenate(outs, axis=-1)


def _attn_sample(page_table, lam, q3, kn3, vn3, sg, cache_kt, cache_v4, past_len, out_scale):
    n_dec, t_new, _ = q3.shape
    n_pages = page_table.shape[1]
    pt = page_table.reshape(-1)
    tok_spec = pl.BlockSpec((1, t_new, ATTN_WIDTH), lambda b, pt: (b, 0, 0))

    def k_spec(p):
        return pl.BlockSpec((1, ATTN_WIDTH, PAGE_SIZE),
                            lambda b, pt, p=p: (pt[b * n_pages + p], 0, 0))

    def v_spec(p):
        return pl.BlockSpec((1, PAGE_SIZE, ATTN_HEADS, ATTN_V_DIM),
                            lambda b, pt, p=p: (pt[b * n_pages + p], 0, 0, 0))

    grid_spec = pltpu.PrefetchScalarGridSpec(
        num_scalar_prefetch=1,
        grid=(n_dec,),
        in_specs=[pl.BlockSpec(memory_space=pltpu.SMEM), tok_spec, tok_spec, tok_spec,
                  pl.BlockSpec((1, ATTN_V_DIM), lambda b, pt: (0, 0))]
                 + [k_spec(p) for p in range(n_pages)] + [v_spec(p) for p in range(n_pages)],
        out_specs=tok_spec)
    kern = functools.partial(_attn_sample_kernel, n_pages=n_pages, t_new=t_new,
                             past_len=past_len, out_scale=out_scale)
    return pl.pallas_call(
        kern,
        grid_spec=grid_spec,
        out_shape=jax.ShapeDtypeStruct(q3.shape, F32),
        compiler_params=_cparams(1),
        name="attn_sample",
    )(pt, lam, q3, kn3, vn3, sg, *([cache_kt] * n_pages), *([cache_v4] * n_pages))


def _out_proj_kernel(x_ref, s_ref, o_ref, w1_ref, w2_ref, g2_ref, wq_ref, qg_ref,
                     x1_ref, cq_ref):
    x1 = (x_ref[...] + _dot(s_ref[...].astype(BF16), w1_ref[...])
          + _dot(o_ref[...].astype(BF16), w2_ref[...]))
    x1_ref[...] = x1
    xn = _rms(x1, g2_ref[...]).astype(BF16)
    cq = _dot(xn, wq_ref[...])
    cq_ref[...] = _head_rms(cq, qg_ref[...], CA_HEAD_DIM, CA_HEAD_DIM ** -0.5).astype(cq_ref.dtype)


def _out_proj(x, ssm_o, att_o, w1, w2, g2, wq, qg, tm, cq_dtype):
    n = x.shape[0]
    tok = lambda width: pl.BlockSpec((tm, width), lambda i: (i, 0))
    return pl.pallas_call(
        _out_proj_kernel,
        grid=(n // tm,),
        in_specs=[tok(D_MODEL), tok(SSM_WIDTH), tok(ATTN_WIDTH),
                  _const_spec((SSM_WIDTH, D_MODEL)), _const_spec((ATTN_WIDTH, D_MODEL)),
                  _const_spec((1, D_MODEL)), _const_spec((D_MODEL, D_MODEL)),
                  _const_spec((1, CA_HEAD_DIM))],
        out_specs=[tok(D_MODEL), tok(D_MODEL)],
        out_shape=[jax.ShapeDtypeStruct((n, D_MODEL), F32),
                   jax.ShapeDtypeStruct((n, D_MODEL), cq_dtype)],
        compiler_params=_cparams(1),
        name="out_proj",
    )(x, ssm_o, att_o, w1, w2, g2, wq, qg)


def _mem_kv_kernel(m_ref, g_ref, wk_ref, wv_ref, kg_ref, mk_ref, mv_ref, mkb_ref, mvb_ref):
    mn = _rms(m_ref[...], g_ref[...]).astype(BF16)
    mk = _head_rms(_dot(mn, wk_ref[...]), kg_ref[...], CA_HEAD_DIM, 1.0)
    mv = _dot(mn, wv_ref[...])
    mk_ref[...] = mk
    mv_ref[...] = mv
    mkb_ref[...] = mk.astype(BF16)
    mvb_ref[...] = mv.astype(BF16)


def _mem_kv(mem, g, wk, wv, kg, tm):
    n = mem.shape[0]
    tok = pl.BlockSpec((tm, D_MODEL), lambda i: (i, 0))
    wspec = _const_spec((D_MODEL, D_MODEL))
    return pl.pallas_call(
        _mem_kv_kernel,
        grid=(n // tm,),
        in_specs=[tok, _const_spec((1, D_MODEL)), wspec, wspec, _const_spec((1, CA_HEAD_DIM))],
        out_specs=[tok, tok, tok, tok],
        out_shape=[jax.ShapeDtypeStruct((n, D_MODEL), F32)] * 2
                  + [jax.ShapeDtypeStruct((n, D_MODEL), BF16)] * 2,
        compiler_params=_cparams(1),
        name="mem_kv",
    )(mem, g, wk, wv, kg)


def _cross_prompt_kernel(cq_ref, mk_ref, mv_ref, co_ref):
    cq = cq_ref[...]
    outs = []
    for h in range(CA_HEADS):
        sl = slice(CA_HEAD_DIM * h, CA_HEAD_DIM * (h + 1))
        s = _dot_nt(cq[:, sl], mk_ref[0, :, sl])
        p = jnp.exp(s - jnp.max(s, axis=-1, keepdims=True))
        l = jnp.sum(p, axis=-1, keepdims=True)
        outs.append(_dot((p / l).astype(BF16), mv_ref[0, :, sl]))
    co_ref[...] = jnp.concatenate(outs, axis=-1).astype(co_ref.dtype)


def _cross_prompt(cq, mkb, mvb, tm):
    n = cq.shape[0]
    bsz = mkb.shape[0]
    per_b = n // bsz // tm
    tok = pl.BlockSpec((tm, D_MODEL), lambda b, i: (b * per_b + i, 0))
    mem = pl.BlockSpec((1, N_MEM, D_MODEL), lambda b, i: (b, 0, 0))
    return pl.pallas_call(
        _cross_prompt_kernel,
        grid=(bsz, per_b),
        in_specs=[tok, mem, mem],
        out_specs=tok,
        out_shape=jax.ShapeDtypeStruct((n, D_MODEL), BF16),
        compiler_params=_cparams(2),
        name="cross_prompt",
    )(cq, mkb, mvb)


def _cross_sample_kernel(cq_ref, mk_ref, mv_ref, co_ref, *, t_new):
    n_rows = CA_HEADS * t_new
    q4 = cq_ref[0]
    qe = jnp.concatenate([q4] * CA_HEADS, axis=0)
    row = lax.broadcasted_iota(jnp.int32, qe.shape, 0)
    lane = lax.broadcasted_iota(jnp.int32, qe.shape, 1)
    qe = jnp.where(lane // CA_HEAD_DIM == row // t_new, qe, 0.0).astype(BF16)
    heads = lambda ref: jnp.concatenate(
        [ref[0, :, h, :] for h in range(CA_HEADS)], axis=1).astype(BF16)
    s = _dot_nt(qe, heads(mk_ref))
    p = jnp.exp(s - jnp.max(s, axis=-1, keepdims=True))
    l = jnp.sum(p, axis=-1, keepdims=True)
    full = _dot((p / l).astype(BF16), heads(mv_ref))
    outs = [full[h * t_new:(h + 1) * t_new, CA_HEAD_DIM * h:CA_HEAD_DIM * (h + 1)]
            for h in range(CA_HEADS)]
    co_ref[0] = jnp.concatenate(outs, axis=-1)
    del n_rows


def _cross_sample(cq3, mem_k, mem_v):
    n_dec, t_new, _ = cq3.shape
    tok = pl.BlockSpec((1, t_new, D_MODEL), lambda b: (b, 0, 0))
    mem = pl.BlockSpec((1, N_MEM, CA_HEADS, CA_HEAD_DIM), lambda b: (b, 0, 0, 0))
    return pl.pallas_call(
        functools.partial(_cross_sample_kernel, t_new=t_new),
        grid=(n_dec,),
        in_specs=[tok, mem, mem],
        out_specs=tok,
        out_shape=jax.ShapeDtypeStruct(cq3.shape, F32),
        compiler_params=_cparams(1),
        name="cross_sample",
    )(cq3, mem_k, mem_v)


def _ffn_kernel(x_ref, co_ref, wo_ref, g3_ref, wg_ref, wv_ref, cw_ref, cb_ref, wd_ref, prev_ref,
                y_ref, cs_ref, hbuf, *, tm, shift, pad):
    i = pl.program_id(1)

    @pl.when(i == 0)
    def _():
        hbuf[pad - 2 * shift:pad, :] = prev_ref[0]

    x2 = x_ref[...] + _dot(co_ref[...].astype(BF16), wo_ref[...])
    xn = _rms(x2, g3_ref[...]).astype(BF16)
    hbuf[pad:pad + tm, :] = _dot(xn, wg_ref[...])
    hv = _dot(xn, wv_ref[...])
    cw = cw_ref[...]
    conv = (cb_ref[...] + cw[0:1, :] * hbuf[pad - 2 * shift:pad - 2 * shift + tm, :]
            + cw[1:2, :] * hbuf[pad - shift:pad - shift + tm, :]
            + cw[2:3, :] * hbuf[pad:pad + tm, :])
    act = (conv * jax.nn.sigmoid(conv) * hv).astype(BF16)
    y_ref[...] = x2 + _dot(act, wd_ref[...])
    last = hbuf[pad + tm - 2 * shift:pad + tm, :]
    cs_ref[0] = last
    hbuf[pad - 2 * shift:pad, :] = last


def _ffn(x1, co, wo, g3, wg, wv, cw, cb, wd, prev, tm, shift):
    n = x1.shape[0]
    n_seq_blocks = prev.shape[0]
    per_b = n // n_seq_blocks // tm
    pad = -(-2 * shift // SUBLANES) * SUBLANES
    tok = pl.BlockSpec((tm, D_MODEL), lambda b, i: (b * per_b + i, 0))
    state = pl.BlockSpec((1, 2 * shift, FFN_HIDDEN), lambda b, i: (b, 0, 0))
    return pl.pallas_call(
        functools.partial(_ffn_kernel, tm=tm, shift=shift, pad=pad),
        grid=(n_seq_blocks, per_b),
        in_specs=[tok, tok, _const_spec((D_MODEL, D_MODEL)), _const_spec((1, D_MODEL)),
                  _const_spec((D_MODEL, FFN_HIDDEN)), _const_spec((D_MODEL, FFN_HIDDEN)),
                  _const_spec((CONV_WIDTH, FFN_HIDDEN)), _const_spec((1, FFN_HIDDEN)),
                  _const_spec((FFN_HIDDEN, D_MODEL)), state],
        out_specs=[tok, state],
        out_shape=[jax.ShapeDtypeStruct((n, D_MODEL), F32),
                   jax.ShapeDtypeStruct(prev.shape, F32)],
        scratch_shapes=[pltpu.VMEM((pad + tm, FFN_HIDDEN), F32)],
        compiler_params=_cparams(2),
        name="ffn",
    )(x1, co, wo, g3, wg, wv, cw, cb, wd, prev)


def _block_diag(blocks, n_outer):
    n, r, c = blocks.shape
    per = n // n_outer
    b = blocks.reshape(n_outer, per, r, c)
    eye = jnp.eye(per, dtype=blocks.dtype)
    full = b[:, :, :, None, :] * eye[None, :, None, :, None]
    return full.reshape(n_outer, per * r, per * c)


def _lam_groups(lb, n_seq):
    halves = lb.reshape(2, 1, N_SLABS, LANES)
    rows = jnp.broadcast_to(halves, (2, n_seq, N_SLABS, LANES))
    rows = rows.reshape(2 * n_seq // SUBLANES, SUBLANES, N_SLABS, LANES)
    return rows.transpose(0, 2, 1, 3)


def _layer(l, x_prompt, x_sample, mem_prompt, cache_k, cache_v, page_table,
           state_ssm_re, state_ssm_im, state_conv, cache_mem_k, cache_mem_v, p):
    bsz, seq, _ = x_prompt.shape
    n_dec, t_new, _ = x_sample.shape
    n_p = bsz * seq
    n_s = n_dec * t_new
    past_len = page_table.shape[1] * PAGE_SIZE
    lam0 = 0.8 - 0.6 * math.exp(-0.3 * l)
    out_scale = 1.0 - lam0
    row = lambda v: v.reshape(1, -1).astype(F32)
    bf = lambda w: w.astype(BF16)

    lb_re, lb_im, bb_re, bb_im, lam = _prep(
        p['ssm_a_re'], p['ssm_a_im'], p['ssm_log_dt'], p['ssm_b_re'], p['ssm_b_im'],
        p['lam_q1'], p['lam_k1'], p['lam_q2'], p['lam_k2'], lam0)

    def b_mats(bbt):
        blocks = bbt.reshape(SSM_GROUP_CH, SSM_GROUPS, SSM_STATE).transpose(1, 0, 2)
        return bf(_block_diag(blocks, 4))

    def c_mats(cm):
        return bf(_block_diag(cm.transpose(0, 2, 1), 4))

    mats = (b_mats(bb_re), b_mats(bb_im), c_mats(p['ssm_c_re']), c_mats(p['ssm_c_im']))
    d_row = row(p['ssm_d'])
    glu_b = bf(p['ssm_glu_w'])

    w_in = p['w_in']
    k0, k1 = SSM_WIDTH + ATTN_WIDTH, SSM_WIDTH + 2 * ATTN_WIDTH
    w_uqv = bf(jnp.concatenate([w_in[:, :k0], w_in[:, k1:]], axis=1))
    w_k = bf(w_in[:, k0:k1])
    seg = jnp.kron(jnp.eye(ATTN_WIDTH // ATTN_HEAD_DIM, dtype=F32),
                   jnp.ones((ATTN_HEAD_DIM, ATTN_HEAD_DIM), F32)).astype(BF16)
    qg = jnp.tile(p['q_norm_g'], ATTN_WIDTH // ATTN_HEAD_DIM).reshape(1, ATTN_WIDTH)
    kg = jnp.tile(p['k_norm_g'], ATTN_WIDTH // ATTN_HEAD_DIM).reshape(1, ATTN_WIDTH)
    sg = row(p['subln_g'])
    w_out_b = bf(p['w_out'])
    w1, w2 = w_out_b[:SSM_WIDTH], w_out_b[SSM_WIDTH:]
    wq_b, wk_b, wv_b, wo_b = bf(p['ca_wq']), bf(p['ca_wk']), bf(p['ca_wv']), bf(p['ca_wo'])
    wg_b, wvf_b, wd_b = bf(p['ffn_wg']), bf(p['ffn_wv']), bf(p['ffn_wd'])
    cw = p['ffn_conv_w'].astype(F32)
    cb = row(p['ffn_conv_b'])

    xp = x_prompt.reshape(n_p, D_MODEL)
    attn_bq, attn_bk = 512, 512
    kg_col = jnp.broadcast_to(p['k_norm_g'].reshape(ATTN_HEAD_DIM, 1), (ATTN_HEAD_DIM, 512))
    u_p, q_p, kt_p, v_p, kb_p, vb_p = _in_proj(xp, row(p['ln1_g']), w_uqv, w_k.T, seg, qg, kg_col,
                                               512, BF16, n_batch=bsz, blk=attn_bk)
    zeros_h = jnp.zeros((2 * bsz // SUBLANES, N_SLABS, SUBLANES, LANES), F32)
    ssm_p, hre_p, him_p = _ssm(u_p.reshape(bsz, seq, SSM_WIDTH), mats,
                               (_lam_groups(lb_re, bsz), _lam_groups(lb_im, bsz)),
                               d_row, glu_b, (zeros_h, zeros_h), bsz, 256)
    att_p = _attn_prompt(lam, q_p.reshape(bsz, seq, ATTN_WIDTH), kb_p,
                         vb_p.reshape(bsz, seq, ATTN_WIDTH), sg, attn_bq, out_scale)
    x1_p, cq_p = _out_proj(xp, ssm_p.reshape(n_p, SSM_WIDTH), att_p.reshape(n_p, ATTN_WIDTH),
                           w1, w2, row(p['ln2_g']), wq_b, row(p['ca_q_norm_g']), 512, BF16)
    mk, mv, mkb, mvb = _mem_kv(mem_prompt.reshape(bsz * N_MEM, D_MODEL), row(p['mem_norm_g']),
                               wk_b, wv_b, row(p['ca_k_norm_g']), 256)
    co_p = _cross_prompt(cq_p, mkb.reshape(bsz, N_MEM, D_MODEL), mvb.reshape(bsz, N_MEM, D_MODEL), 512)
    y_p, conv_p = _ffn(x1_p, co_p, wo_b, row(p['ln3_g']), wg_b, wvf_b, cw, cb, wd_b,
                       jnp.zeros((bsz, CONV_WIDTH - 1, FFN_HIDDEN), F32), 512, 1)

    xs = x_sample.reshape(n_s, D_MODEL)
    u_s, q_s, k_s, v_s = _in_proj(xs, row(p['ln1_g']), w_uqv, w_k, seg, qg, kg, n_s, F32)
    h0 = (_state_to_groups(state_ssm_re.reshape(n_dec, N_STATE), n_dec),
          _state_to_groups(state_ssm_im.reshape(n_dec, N_STATE), n_dec))
    ssm_s, hre_s, him_s = _ssm(u_s.reshape(1, n_s, SSM_WIDTH), mats,
                               (_lam_groups(lb_re, n_dec), _lam_groups(lb_im, n_dec)),
                               d_row, glu_b, h0, n_dec, t_new)
    n_phys = cache_k.shape[0]
    cache_kt = cache_k.transpose(0, 2, 3, 4, 1).reshape(n_phys, ATTN_WIDTH, PAGE_SIZE)
    att_s = _attn_sample(page_table, lam, q_s.reshape(n_dec, t_new, ATTN_WIDTH),
                         k_s.reshape(n_dec, t_new, ATTN_WIDTH), v_s.reshape(n_dec, t_new, ATTN_WIDTH),
                         sg, cache_kt, cache_v, past_len, out_scale)
    x1_s, cq_s = _out_proj(xs, ssm_s.reshape(n_s, SSM_WIDTH), att_s.reshape(n_s, ATTN_WIDTH),
                           w1, w2, row(p['ln2_g']), wq_b, row(p['ca_q_norm_g']), n_s, F32)
    co_s = _cross_sample(cq_s.reshape(n_dec, t_new, D_MODEL), cache_mem_k, cache_mem_v)
    tmaj = lambda a: a.reshape(n_dec, t_new, -1).transpose(1, 0, 2).reshape(n_s, -1)
    prev_s = state_conv.transpose(1, 0, 2).reshape(1, (CONV_WIDTH - 1) * n_dec, FFN_HIDDEN)
    y_s, conv_s = _ffn(tmaj(x1_s), tmaj(co_s), wo_b, row(p['ln3_g']), wg_b, wvf_b, cw, cb, wd_b,
                       prev_s, n_s, n_dec)
    y_s = y_s.reshape(t_new, n_dec, D_MODEL).transpose(1, 0, 2)
    conv_s = conv_s.reshape(CONV_WIDTH - 1, n_dec, FFN_HIDDEN).transpose(1, 0, 2)

    return (y_p.reshape(bsz, seq, D_MODEL), y_s,
            kt_p.reshape(bsz, ATTN_HEADS, 2, ATTN_HEAD_DIM, seq).transpose(0, 4, 1, 2, 3),
            v_p.reshape(bsz, seq, ATTN_HEADS, ATTN_V_DIM),
            k_s.reshape(n_dec, t_new, ATTN_HEADS, 2, ATTN_HEAD_DIM),
            v_s.reshape(n_dec, t_new, ATTN_HEADS, ATTN_V_DIM),
            _groups_to_state(hre_p, bsz), _groups_to_state(him_p, bsz),
            _groups_to_state(hre_s, n_dec), _groups_to_state(him_s, n_dec),
            conv_p, conv_s,
            mk.reshape(bsz, N_MEM, CA_HEADS, CA_HEAD_DIM), mv.reshape(bsz, N_MEM, CA_HEADS, CA_HEAD_DIM))


def kernel(x_prompt, x_sample, mem_prompt, cache_k, cache_v, page_table, state_ssm_re, state_ssm_im, state_conv, cache_mem_k, cache_mem_v, ln1_g, w_in, ssm_a_re, ssm_a_im, ssm_b_re, ssm_b_im, ssm_c_re, ssm_c_im, ssm_d, ssm_log_dt, ssm_glu_w, q_norm_g, k_norm_g, lam_q1, lam_k1, lam_q2, lam_k2, subln_g, w_out, ln2_g, mem_norm_g, ca_wq, ca_wk, ca_wv, ca_q_norm_g, ca_k_norm_g, ca_wo, ln3_g, ffn_wg, ffn_wv, ffn_conv_w, ffn_conv_b, ffn_wd):
    params = dict(
        ln1_g=ln1_g, w_in=w_in, ssm_a_re=ssm_a_re, ssm_a_im=ssm_a_im, ssm_b_re=ssm_b_re,
        ssm_b_im=ssm_b_im, ssm_c_re=ssm_c_re, ssm_c_im=ssm_c_im, ssm_d=ssm_d,
        ssm_log_dt=ssm_log_dt, ssm_glu_w=ssm_glu_w, q_norm_g=q_norm_g, k_norm_g=k_norm_g,
        lam_q1=lam_q1, lam_k1=lam_k1, lam_q2=lam_q2, lam_k2=lam_k2, subln_g=subln_g,
        w_out=w_out, ln2_g=ln2_g, mem_norm_g=mem_norm_g, ca_wq=ca_wq, ca_wk=ca_wk, ca_wv=ca_wv,
        ca_q_norm_g=ca_q_norm_g, ca_k_norm_g=ca_k_norm_g, ca_wo=ca_wo, ln3_g=ln3_g,
        ffn_wg=ffn_wg, ffn_wv=ffn_wv, ffn_conv_w=ffn_conv_w, ffn_conv_b=ffn_conv_b, ffn_wd=ffn_wd)
    y_p, y_s = x_prompt, x_sample
    per_layer = []
    for l in range(w_in.shape[0]):
        p = {k: v[l] for k, v in params.items()}
        outs = _layer(l, y_p, y_s, mem_prompt, cache_k[l], cache_v[l], page_table,
                      state_ssm_re[l], state_ssm_im[l], state_conv[l],
                      cache_mem_k[l], cache_mem_v[l], p)
        y_p, y_s = outs[0], outs[1]
        per_layer.append(outs[2:])
    return (y_p, y_s) + tuple(jnp.stack(leaf) for leaf in zip(*per_layer))
```

```python
import functools
import math

import jax
import jax.numpy as jnp
from jax import lax
from jax.experimental import pallas as pl
from jax.experimental.pallas import tpu as pltpu

F32 = jnp.float32
BF16 = jnp.bfloat16

D_MODEL = 1024
SSM_WIDTH = 512
SSM_GROUP_CH = 16
SSM_GROUPS = 32
SSM_STATE = 64
N_STATE = SSM_GROUPS * SSM_STATE
ATTN_WIDTH = 512
ATTN_HEAD_DIM = 64
ATTN_V_DIM = 128
ATTN_HEADS = 4
PAGE_SIZE = 128
N_MEM = 256
CA_HEADS = 4
CA_HEAD_DIM = 256
FFN_HIDDEN = 2816
CONV_WIDTH = 3
NORM_EPS = 1e-6

LANES = 128
SUBLANES = 8
VMEM_LIMIT = 56 * 1024 * 1024

NEG_BIG = -1e30
LOG2E = math.log2(math.e)


def _cparams(n_axes):
    return pltpu.CompilerParams(
        dimension_semantics=("arbitrary",) * n_axes,
        vmem_limit_bytes=VMEM_LIMIT)


def _const_spec(shape):
    nd = len(shape)
    return pl.BlockSpec(shape, lambda *_: (0,) * nd, pipeline_mode=pl.Buffered(1))


def _rms(x, g):
    return x * lax.rsqrt(jnp.mean(x * x, axis=-1, keepdims=True) + NORM_EPS) * g


def _dot(a, b):
    return jnp.dot(a, b, preferred_element_type=F32)


def _dot_nt(a, b):
    return lax.dot_general(a, b, (((1,), (1,)), ((), ())), preferred_element_type=F32)


def _head_rms(x, g, width, scale):
    outs = []
    for h in range(x.shape[-1] // width):
        c = x[:, h * width:(h + 1) * width]
        outs.append(_rms(c, g) * scale)
    return jnp.concatenate(outs, axis=-1)


def _prep_kernel(are_ref, aim_ref, ldt_ref, bre_ref, bim_ref,
                 q1_ref, k1_ref, q2_ref, k2_ref,
                 lbre_ref, lbim_ref, bbre_ref, bbim_ref, lam_ref, *, lam0):
    a_re = are_ref[...]
    a_im = aim_ref[...]
    dt = jnp.exp(ldt_ref[...])
    mag = jnp.exp(a_re * dt)
    lb_re = mag * jnp.cos(a_im * dt)
    lb_im = mag * jnp.sin(a_im * dt)
    den = a_re * a_re + a_im * a_im
    n_re = lb_re - 1.0
    f_re = (n_re * a_re + lb_im * a_im) / den
    f_im = (lb_im * a_re - n_re * a_im) / den
    b_re = bre_ref[...]
    b_im = bim_ref[...]
    lbre_ref[...] = lb_re
    lbim_ref[...] = lb_im
    bbre_ref[...] = f_re * b_re - f_im * b_im
    bbim_ref[...] = f_re * b_im + f_im * b_re
    s1 = jnp.sum(q1_ref[...] * k1_ref[...], axis=-1, keepdims=True)
    s2 = jnp.sum(q2_ref[...] * k2_ref[...], axis=-1, keepdims=True)
    lam_ref[...] = jnp.exp(s1) - jnp.exp(s2) + lam0


def _prep(a_re, a_im, log_dt, b_re, b_im, q1, k1, q2, k2, lam0):
    are = a_re.reshape(1, N_STATE)
    aim = a_im.reshape(1, N_STATE)
    ldt = jnp.repeat(log_dt, SSM_STATE).reshape(1, N_STATE)
    bre = b_re.transpose(2, 0, 1).reshape(SSM_GROUP_CH, N_STATE)
    bim = b_im.transpose(2, 0, 1).reshape(SSM_GROUP_CH, N_STATE)
    vec = lambda v: v.reshape(1, ATTN_HEAD_DIM)
    row = jax.ShapeDtypeStruct((1, N_STATE), F32)
    mat = jax.ShapeDtypeStruct((SSM_GROUP_CH, N_STATE), F32)
    return pl.pallas_call(
        functools.partial(_prep_kernel, lam0=lam0),
        out_shape=(row, row, mat, mat, jax.ShapeDtypeStruct((1, 1), F32)),
        name="prep",
    )(are, aim, ldt, bre, bim, vec(q1), vec(k1), vec(q2), vec(k2))


def _in_proj_kernel(x_ref, g_ref, w_ref, wk_ref, seg_ref, qg_ref, kg_ref,
                    u_ref, q_ref, k_ref, v_ref, *bf_refs, key_major, blk):
    xn = _rms(x_ref[...], g_ref[...]).astype(BF16)
    proj = _dot(xn, w_ref[...])
    seg = seg_ref[...]

    def chunk_norm(z, g):
        ms = _dot((z * z).astype(BF16), seg) * (1.0 / ATTN_HEAD_DIM)
        return z * lax.rsqrt(ms + NORM_EPS) * g

    u_ref[...] = proj[:, :SSM_WIDTH]
    q = proj[:, SSM_WIDTH:SSM_WIDTH + ATTN_WIDTH]
    v = proj[:, SSM_WIDTH + ATTN_WIDTH:]
    q_ref[...] = (chunk_norm(q, qg_ref[...]) * (ATTN_HEAD_DIM ** -0.5 * LOG2E)).astype(q_ref.dtype)
    if key_major:
        kb_ref, vb_ref = bf_refs
        tm = x_ref.shape[0]
        z = _dot_nt(wk_ref[...], xn).reshape(ATTN_WIDTH // ATTN_HEAD_DIM, ATTN_HEAD_DIM, tm)
        ms = jnp.mean(z * z, axis=1, keepdims=True)
        kn = (z * lax.rsqrt(ms + NORM_EPS) * kg_ref[...]).reshape(ATTN_WIDTH, tm)
        k_ref[0] = kn
        for i in range(tm // blk):
            kb_ref[0, i] = kn[:, i * blk:(i + 1) * blk].astype(BF16)
        for h in range(ATTN_HEADS):
            v_ref[:, h, :] = v[:, ATTN_V_DIM * h:ATTN_V_DIM * (h + 1)]
        vb_ref[...] = v.astype(BF16)
    else:
        k_ref[...] = chunk_norm(_dot(xn, wk_ref[...]), kg_ref[...])
        v_ref[...] = v


def _in_proj(x, ln1_g, w_uqv, wk, seg, qg, kg, tm, q_dtype, n_batch=None, blk=None):
    n = x.shape[0]
    key_major = n_batch is not None
    bsz = n_batch if key_major else 1
    per_b = n // bsz // tm
    tok = lambda width: pl.BlockSpec((tm, width), lambda b, i: (b * per_b + i, 0))
    out_specs = [tok(SSM_WIDTH), tok(ATTN_WIDTH)]
    out_shape = [jax.ShapeDtypeStruct((n, SSM_WIDTH), F32),
                 jax.ShapeDtypeStruct((n, ATTN_WIDTH), q_dtype)]
    if key_major:
        t = n // bsz
        out_specs += [pl.BlockSpec((1, ATTN_WIDTH, tm), lambda b, i: (b, 0, i)),
                      pl.BlockSpec((tm, ATTN_HEADS, ATTN_V_DIM), lambda b, i: (b * per_b + i, 0, 0)),
                      pl.BlockSpec((1, tm // blk, ATTN_WIDTH, blk), lambda b, i: (b, i, 0, 0)),
                      tok(ATTN_WIDTH)]
        out_shape += [jax.ShapeDtypeStruct((bsz, ATTN_WIDTH, t), F32),
                      jax.ShapeDtypeStruct((n, ATTN_HEADS, ATTN_V_DIM), F32),
                      jax.ShapeDtypeStruct((bsz, t // blk, ATTN_WIDTH, blk), BF16),
                      jax.ShapeDtypeStruct((n, ATTN_WIDTH), BF16)]
    else:
        out_specs += [tok(ATTN_WIDTH), tok(ATTN_WIDTH)]
        out_shape += [jax.ShapeDtypeStruct((n, ATTN_WIDTH), F32)] * 2
    return pl.pallas_call(
        functools.partial(_in_proj_kernel, key_major=key_major, blk=blk),
        grid=(bsz, per_b),
        in_specs=[tok(D_MODEL), _const_spec((1, D_MODEL)), _const_spec(w_uqv.shape),
                  _const_spec(wk.shape), _const_spec((ATTN_WIDTH, ATTN_WIDTH)),
                  _const_spec((1, ATTN_WIDTH)), _const_spec(kg.shape)],
        out_specs=out_specs,
        out_shape=out_shape,
        compiler_params=_cparams(2),
        name="in_proj",
    )(x, ln1_g, w_uqv, wk, seg, qg, kg)


N_SLABS = N_STATE // 2 // LANES
HALF = N_STATE // 2


def _gelu_tanh(x):
    c = math.sqrt(2.0 / math.pi)
    return 0.5 * x * (1.0 + jnp.tanh(c * (x + 0.044715 * (x * x * x))))


def _ssm_kernel(u_ref, bre_ref, bim_ref, cre_ref, cim_ref, lre_ref, lim_ref,
                d_ref, glu_ref, h0re_ref, h0im_ref,
                out_ref, hre_ref, him_ref, sre, sim,
                *, n_seq, tc, pitch, n_bulk, rows_bulk, pitch_bulk):
    c = pl.program_id(0)
    rows = n_seq * tc
    n_groups = 2 * n_seq // SUBLANES

    @pl.when(c == 0)
    def _():
        hre_ref[...] = h0re_ref[...]
        him_ref[...] = h0im_ref[...]

    u = u_ref[...].reshape(rows, SSM_WIDTH)
    ub = u.astype(BF16)

    for j in range(4):
        hh, jj = divmod(j, 2)
        uj = ub[:, LANES * j:LANES * (j + 1)]
        for src, dst in ((bre_ref, sre), (bim_ref, sim)):
            bu = _dot(uj, src[j])
            for kk in range(4):
                slab = 4 * jj + kk
                for bb in range(n_bulk):
                    r0 = (hh * n_bulk + bb) * pitch_bulk
                    dst[slab, r0:r0 + rows_bulk, :] = (
                        bu[bb * rows_bulk:(bb + 1) * rows_bulk, LANES * kk:LANES * (kk + 1)])

    def group_body(g, carry):
        base = g * (SUBLANES * pitch)
        lr = [lre_ref[g, k] for k in range(N_SLABS)]
        li = [lim_ref[g, k] for k in range(N_SLABS)]
        hr0 = tuple(hre_ref[g, k] for k in range(N_SLABS))
        hi0 = tuple(him_ref[g, k] for k in range(N_SLABS))

        def step(t, hc):
            hr, hi = hc
            idx = pl.ds(base + t, SUBLANES, stride=pitch)
            nr, ni = [], []
            for k in range(N_SLABS):
                br = sre[k, idx, :]
                bi = sim[k, idx, :]
                r = lr[k] * hr[k] - li[k] * hi[k] + br
                i = lr[k] * hi[k] + li[k] * hr[k] + bi
                sre[k, idx, :] = r
                sim[k, idx, :] = i
                nr.append(r)
                ni.append(i)
            return tuple(nr), tuple(ni)

        hr, hi = lax.fori_loop(0, tc, step, (hr0, hi0), unroll=min(tc, 4))
        for k in range(N_SLABS):
            hre_ref[g, k] = hr[k]
            him_ref[g, k] = hi[k]
        return carry

    lax.fori_loop(0, n_groups, group_body, 0)

    y_rows = []
    for bb in range(n_bulk):
        y_cols = []
        for j in range(4):
            hh, jj = divmod(j, 2)
            r0 = (hh * n_bulk + bb) * pitch_bulk
            hr = jnp.concatenate(
                [sre[4 * jj + kk, r0:r0 + rows_bulk, :] for kk in range(4)], axis=1).astype(BF16)
            hi = jnp.concatenate(
                [sim[4 * jj + kk, r0:r0 + rows_bulk, :] for kk in range(4)], axis=1).astype(BF16)
            y_cols.append(_dot(hr, cre_ref[j]) - _dot(hi, cim_ref[j]))
        y_rows.append(jnp.concatenate(y_cols, axis=1))
    y = jnp.concatenate(y_rows, axis=0) if n_bulk > 1 else y_rows[0]
    y = y + d_ref[...] * u
    gl = _gelu_tanh(y)
    gate = jax.nn.sigmoid(_dot(gl.astype(BF16), glu_ref[...]))
    out_ref[...] = (gl * gate).astype(out_ref.dtype).reshape(out_ref.shape)


def _ssm(u3, mats, lam_g, d_row, glu_b, h0, n_seq, tc):
    bre, bim, cre, cim = mats
    lre_g, lim_g = lam_g
    h0re, h0im = h0
    n_groups = 2 * n_seq // SUBLANES
    t_total = u3.shape[0] * u3.shape[1] // n_seq
    n_chunks = t_total // tc
    if tc % SUBLANES == 0:
        pitch = tc + SUBLANES
        n_bulk, rows_bulk, pitch_bulk = n_seq, tc, pitch
        ublock = (n_seq, tc, SSM_WIDTH)
    else:
        assert n_chunks == 1
        pitch = tc
        n_bulk, rows_bulk, pitch_bulk = 1, n_seq * tc, n_seq * tc
        ublock = (1, n_seq * tc, SSM_WIDTH)
    scr_rows = 2 * n_bulk * pitch_bulk
    hshape = (n_groups, N_SLABS, SUBLANES, LANES)
    kern = functools.partial(_ssm_kernel, n_seq=n_seq, tc=tc, pitch=pitch,
                             n_bulk=n_bulk, rows_bulk=rows_bulk, pitch_bulk=pitch_bulk)
    return pl.pallas_call(
        kern,
        grid=(n_chunks,),
        in_specs=[pl.BlockSpec(ublock, lambda c: (0, c, 0)),
                  _const_spec(bre.shape), _const_spec(bim.shape),
                  _const_spec(cre.shape), _const_spec(cim.shape),
                  _const_spec(hshape), _const_spec(hshape),
                  _const_spec((1, SSM_WIDTH)), _const_spec((SSM_WIDTH, SSM_WIDTH)),
                  _const_spec(hshape), _const_spec(hshape)],
        out_specs=[pl.BlockSpec(ublock, lambda c: (0, c, 0)),
                   _const_spec(hshape), _const_spec(hshape)],
        out_shape=[jax.ShapeDtypeStruct(u3.shape, BF16),
                   jax.ShapeDtypeStruct(hshape, F32),
                   jax.ShapeDtypeStruct(hshape, F32)],
        scratch_shapes=[pltpu.VMEM((N_SLABS, scr_rows, LANES), F32),
                        pltpu.VMEM((N_SLABS, scr_rows, LANES), F32)],
        compiler_params=_cparams(1),
        name="ssm",
    )(u3, bre, bim, cre, cim, lre_g, lim_g, d_row, glu_b, h0re, h0im)


def _state_to_groups(h, n_seq):
    x = h.reshape(n_seq, 2, N_SLABS, LANES).transpose(1, 0, 2, 3)
    x = x.reshape(2 * n_seq // SUBLANES, SUBLANES, N_SLABS, LANES)
    return x.transpose(0, 2, 1, 3)


def _groups_to_state(x, n_seq):
    x = x.transpose(0, 2, 1, 3).reshape(2, n_seq, N_SLABS, LANES)
    return x.transpose(1, 0, 2, 3).reshape(n_seq, SSM_GROUPS, SSM_STATE)


def _head_slope(h):
    return jnp.where(h == 0, 2.0 ** -2, jnp.where(h == 1, 2.0 ** -4,
                     jnp.where(h == 2, 2.0 ** -6, 2.0 ** -8))).astype(F32)


def _attn_prompt_kernel(lam_ref, q_ref, k_ref, v_ref, sg_ref, o_ref,
                        q_scr, s_scr, m_scr, l_scr, acc_scr, *, bq, bk, out_scale):
    h = pl.program_id(1)
    qi = pl.program_id(2)
    slope = _head_slope(h) * LOG2E
    lam = lam_ref[0, 0]
    n_maps = 2

    q = q_ref[0]
    lane = lax.broadcasted_iota(jnp.int32, q.shape, 1)
    zero = jnp.zeros_like(q)
    q_scr[0] = jnp.where(lane < ATTN_HEAD_DIM, q, zero)
    q_scr[1] = jnp.where(lane >= ATTN_HEAD_DIM, q, zero)

    col = lax.broadcasted_iota(jnp.int32, (1, bk), 1)
    q0 = qi * bq

    def produce(j, masked):
        kpos = col + j * bk
        colbias = slope * (kpos - q0).astype(F32)
        kt = k_ref[0, j]
        for i in range(n_maps):
            s = _dot(q_scr[i], kt) + colbias
            if masked:
                qpos = lax.broadcasted_iota(jnp.int32, (bq, bk), 0) + q0
                s = jnp.where(qpos >= kpos, s, -jnp.inf)
            s_scr[i] = s

    def lane_tiles(x):
        return [x[:, LANES * i:LANES * (i + 1)] for i in range(bk // LANES)]

    n_full = q0 // bk

    def run_pass(consume):
        produce(0, True)

        def body(j, carry):
            consume(j)
            produce(j + 1, False)
            return carry

        lax.fori_loop(0, n_full - 1, body, 0)

        @pl.when(n_full >= 1)
        def _():
            consume(n_full - 1)
            produce(n_full, True)

        consume(n_full)

    def consume_max(j):
        del j
        for i in range(n_maps):
            mpart = m_scr[i]
            for piece in lane_tiles(s_scr[i]):
                mpart = jnp.maximum(mpart, piece)
            m_scr[i] = mpart

    m_scr[...] = jnp.full(m_scr.shape, -jnp.inf, F32)
    run_pass(consume_max)
    ms = [jnp.max(m_scr[i], axis=-1, keepdims=True) for i in range(n_maps)]

    def consume_sum(j):
        start = pl.multiple_of(j * bk, bk)
        vb = v_ref[0, pl.ds(start, bk), :]
        for i in range(n_maps):
            p = jnp.exp2(s_scr[i] - ms[i])
            lpart = l_scr[i]
            for piece in lane_tiles(p):
                lpart = lpart + piece
            l_scr[i] = lpart
            acc_scr[i] += _dot(p.astype(BF16), vb)

    l_scr[...] = jnp.zeros(l_scr.shape, F32)
    acc_scr[...] = jnp.zeros(acc_scr.shape, F32)
    run_pass(consume_sum)
    outs = [acc_scr[i] / jnp.sum(l_scr[i], axis=-1, keepdims=True) for i in range(n_maps)]
    o = outs[0] - lam * outs[1]
    o_ref[0] = (_rms(o, sg_ref[...]) * out_scale).astype(o_ref.dtype)


def _attn_prompt(lam, qb, kb, vb, sg, bq, out_scale):
    bsz, t, _ = qb.shape
    bk = kb.shape[-1]
    return pl.pallas_call(
        functools.partial(_attn_prompt_kernel, bq=bq, bk=bk, out_scale=out_scale),
        grid=(bsz, ATTN_HEADS, t // bq),
        in_specs=[pl.BlockSpec(memory_space=pltpu.SMEM),
                  pl.BlockSpec((1, bq, ATTN_V_DIM), lambda b, h, i: (b, i, h)),
                  pl.BlockSpec((1, t // bk, ATTN_V_DIM, bk), lambda b, h, i: (b, 0, h, 0)),
                  pl.BlockSpec((1, t, ATTN_V_DIM), lambda b, h, i: (b, 0, h)),
                  _const_spec((1, ATTN_V_DIM))],
        out_specs=pl.BlockSpec((1, bq, ATTN_V_DIM), lambda b, h, i: (b, i, h)),
        out_shape=jax.ShapeDtypeStruct(qb.shape, BF16),
        scratch_shapes=[pltpu.VMEM((2, bq, ATTN_V_DIM), BF16),
                        pltpu.VMEM((2, bq, bk), F32),
                        pltpu.VMEM((2, bq, LANES), F32),
                        pltpu.VMEM((2, bq, LANES), F32),
                        pltpu.VMEM((2, bq, ATTN_V_DIM), F32)],
        compiler_params=_cparams(3),
        name="attn_prompt",
    )(lam, qb, kb, vb, sg)


def _attn_sample_kernel(pt_ref, lam_ref, q_ref, kn_ref, vn_ref, sg_ref, *refs,
                        n_pages, t_new, past_len, out_scale):
    k_refs = refs[:n_pages]
    v_refs = refs[n_pages:2 * n_pages]
    o_ref = refs[2 * n_pages]
    del pt_ref
    lam = lam_ref[0, 0]
    n_rows = 2 * ATTN_HEADS * t_new

    q4 = q_ref[0]
    qe = jnp.concatenate([q4] * (2 * ATTN_HEADS), axis=0)
    row = lax.broadcasted_iota(jnp.int32, qe.shape, 0)
    lane = lax.broadcasted_iota(jnp.int32, qe.shape, 1)
    qe = jnp.where(lane // ATTN_HEAD_DIM == row // t_new, qe, 0.0)
    qeb = qe.astype(BF16)

    rcol = lax.broadcasted_iota(jnp.int32, (n_rows, 1), 0)
    head_c = rcol // (2 * t_new)
    tok_c = rcol % t_new
    slope = _head_slope(head_c) * LOG2E
    rows_h = 2 * t_new

    kt = jnp.concatenate([k_refs[p][0].astype(BF16) for p in range(n_pages)], axis=1)
    kpos = lax.broadcasted_iota(jnp.int32, (1, n_pages * PAGE_SIZE), 1) - past_len
    s = _dot(qeb, kt) + slope * kpos.astype(F32)

    kn = kn_ref[0]
    vn = vn_ref[0]
    s_new = []
    for c in range(t_new):
        sc = jnp.sum(qe * kn[c:c + 1, :], axis=-1, keepdims=True) + slope * float(c)
        s_new.append(jnp.where(tok_c >= c, sc, -jnp.inf))

    m = jnp.max(s, axis=-1, keepdims=True)
    for sc in s_new:
        m = jnp.maximum(m, sc)
    pe = jnp.exp2(s - m)
    l = jnp.sum(pe, axis=-1, keepdims=True)

    pv = []
    for h in range(ATTN_HEADS):
        vh = jnp.concatenate(
            [v_refs[p][0, pl.ds(h, PAGE_SIZE, stride=ATTN_HEADS), :].astype(BF16)
             for p in range(n_pages)], axis=0)
        pv.append(_dot(pe[rows_h * h:rows_h * (h + 1)].astype(BF16), vh))
    acc = jnp.concatenate(pv, axis=0)

    for c in range(t_new):
        pc = jnp.exp2(s_new[c] - m)
        l = l + pc
        vc = jnp.concatenate(
            [jnp.broadcast_to(vn[c:c + 1, ATTN_V_DIM * h:ATTN_V_DIM * (h + 1)], (rows_h, ATTN_V_DIM))
             for h in range(ATTN_HEADS)], axis=0)
        acc = acc + pc * vc

    acc = acc / l
    outs = []
    for h in range(ATTN_HEADS):
        a0 = acc[rows_h * h:rows_h * h + t_new]
        a1 = acc[rows_h * h + t_new:rows_h * (h + 1)]
        outs.append(_rms(a0 - lam * a1, sg_ref[...]) * out_scale)
    o_ref[0] = jnp.concatenate(outs, axis=-1)


def _attn_sample(page_table, lam, q3, kn3, vn3, sg, cache_kt, cache_v3, past_len, out_scale):
    n_dec, t_new, _ = q3.shape
    n_pages = page_table.shape[1]
    pt = page_table.reshape(-1)
    tok_spec = pl.BlockSpec((1, t_new, ATTN_WIDTH), lambda b, pt: (b, 0, 0))

    def k_spec(p):
        return pl.BlockSpec((1, ATTN_WIDTH, PAGE_SIZE),
                            lambda b, pt, p=p: (pt[b * n_pages + p], 0, 0))

    def v_spec(p):
        return pl.BlockSpec((1, PAGE_SIZE * ATTN_HEADS, ATTN_V_DIM),
                            lambda b, pt, p=p: (pt[b * n_pages + p], 0, 0))

    grid_spec = pltpu.PrefetchScalarGridSpec(
        num_scalar_prefetch=1,
        grid=(n_dec,),
        in_specs=[pl.BlockSpec(memory_space=pltpu.SMEM), tok_spec, tok_spec, tok_spec,
                  pl.BlockSpec((1, ATTN_V_DIM), lambda b, pt: (0, 0))]
                 + [k_spec(p) for p in range(n_pages)] + [v_spec(p) for p in range(n_pages)],
        out_specs=tok_spec)
    kern = functools.partial(_attn_sample_kernel, n_pages=n_pages, t_new=t_new,
                             past_len=past_len, out_scale=out_scale)
    return pl.pallas_call(
        kern,
        grid_spec=grid_spec,
        out_shape=jax.ShapeDtypeStruct(q3.shape, F32),
        compiler_params=_cparams(1),
        name="attn_sample",
    )(pt, lam, q3, kn3, vn3, sg, *([cache_kt] * n_pages), *([cache_v3] * n_pages))


def _out_proj_kernel(x_ref, s_ref, o_ref, w1_ref, w2_ref, g2_ref, wq_ref, qg_ref,
                     x1_ref, cq_ref):
    x1 = (x_ref[...] + _dot(s_ref[...].astype(BF16), w1_ref[...])
          + _dot(o_ref[...].astype(BF16), w2_ref[...]))
    x1_ref[...] = x1
    xn = _rms(x1, g2_ref[...]).astype(BF16)
    cq = _dot(xn, wq_ref[...])
    cq_ref[...] = _head_rms(cq, qg_ref[...], CA_HEAD_DIM, CA_HEAD_DIM ** -0.5).astype(cq_ref.dtype)


def _out_proj(x, ssm_o, att_o, w1, w2, g2, wq, qg, tm, cq_dtype):
    n = x.shape[0]
    tok = lambda width: pl.BlockSpec((tm, width), lambda i: (i, 0))
    return pl.pallas_call(
        _out_proj_kernel,
        grid=(n // tm,),
        in_specs=[tok(D_MODEL), tok(SSM_WIDTH), tok(ATTN_WIDTH),
                  _const_spec((SSM_WIDTH, D_MODEL)), _const_spec((ATTN_WIDTH, D_MODEL)),
                  _const_spec((1, D_MODEL)), _const_spec((D_MODEL, D_MODEL)),
                  _const_spec((1, CA_HEAD_DIM))],
        out_specs=[tok(D_MODEL), tok(D_MODEL)],
        out_shape=[jax.ShapeDtypeStruct((n, D_MODEL), F32),
                   jax.ShapeDtypeStruct((n, D_MODEL), cq_dtype)],
        compiler_params=_cparams(1),
        name="out_proj",
    )(x, ssm_o, att_o, w1, w2, g2, wq, qg)


def _mem_kv_kernel(m_ref, g_ref, wk_ref, wv_ref, kg_ref, mk_ref, mv_ref, mkb_ref, mvb_ref):
    mn = _rms(m_ref[...], g_ref[...]).astype(BF16)
    mk = _head_rms(_dot(mn, wk_ref[...]), kg_ref[...], CA_HEAD_DIM, 1.0)
    mv = _dot(mn, wv_ref[...])
    mk_ref[...] = mk
    mv_ref[...] = mv
    mkb_ref[...] = mk.astype(BF16)
    mvb_ref[...] = mv.astype(BF16)


def _mem_kv(mem, g, wk, wv, kg, tm):
    n = mem.shape[0]
    tok = pl.BlockSpec((tm, D_MODEL), lambda i: (i, 0))
    wspec = _const_spec((D_MODEL, D_MODEL))
    return pl.pallas_call(
        _mem_kv_kernel,
        grid=(n // tm,),
        in_specs=[tok, _const_spec((1, D_MODEL)), wspec, wspec, _const_spec((1, CA_HEAD_DIM))],
        out_specs=[tok, tok, tok, tok],
        out_shape=[jax.ShapeDtypeStruct((n, D_MODEL), F32)] * 2
                  + [jax.ShapeDtypeStruct((n, D_MODEL), BF16)] * 2,
        compiler_params=_cparams(1),
        name="mem_kv",
    )(mem, g, wk, wv, kg)


def _cross_prompt_kernel(cq_ref, mk_ref, mv_ref, co_ref):
    cq = cq_ref[...]
    outs = []
    for h in range(CA_HEADS):
        sl = slice(CA_HEAD_DIM * h, CA_HEAD_DIM * (h + 1))
        s = _dot_nt(cq[:, sl], mk_ref[0, :, sl])
        p = jnp.exp(s - jnp.max(s, axis=-1, keepdims=True))
        l = jnp.sum(p, axis=-1, keepdims=True)
        outs.append(_dot((p / l).astype(BF16), mv_ref[0, :, sl]))
    co_ref[...] = jnp.concatenate(outs, axis=-1).astype(co_ref.dtype)


def _cross_prompt(cq, mkb, mvb, tm):
    n = cq.shape[0]
    bsz = mkb.shape[0]
    per_b = n // bsz // tm
    tok = pl.BlockSpec((tm, D_MODEL), lambda b, i: (b * per_b + i, 0))
    mem = pl.BlockSpec((1, N_MEM, D_MODEL), lambda b, i: (b, 0, 0))
    return pl.pallas_call(
        _cross_prompt_kernel,
        grid=(bsz, per_b),
        in_specs=[tok, mem, mem],
        out_specs=tok,
        out_shape=jax.ShapeDtypeStruct((n, D_MODEL), BF16),
        compiler_params=_cparams(2),
        name="cross_prompt",
    )(cq, mkb, mvb)


def _cross_sample_kernel(cq_ref, mk_ref, mv_ref, co_ref, *, t_new):
    n_rows = CA_HEADS * t_new
    q4 = cq_ref[0]
    qe = jnp.concatenate([q4] * CA_HEADS, axis=0)
    row = lax.broadcasted_iota(jnp.int32, qe.shape, 0)
    lane = lax.broadcasted_iota(jnp.int32, qe.shape, 1)
    qe = jnp.where(lane // CA_HEAD_DIM == row // t_new, qe, 0.0).astype(BF16)
    halves = CA_HEAD_DIM // LANES
    per_key = halves * CA_HEADS

    def heads(ref):
        return jnp.concatenate(
            [ref[0, pl.ds(half * CA_HEADS + h, N_MEM, stride=per_key), :]
             for h in range(CA_HEADS) for half in range(halves)], axis=1).astype(BF16)
    s = _dot_nt(qe, heads(mk_ref))
    p = jnp.exp(s - jnp.max(s, axis=-1, keepdims=True))
    l = jnp.sum(p, axis=-1, keepdims=True)
    full = _dot((p / l).astype(BF16), heads(mv_ref))
    outs = [full[h * t_new:(h + 1) * t_new, CA_HEAD_DIM * h:CA_HEAD_DIM * (h + 1)]
            for h in range(CA_HEADS)]
    co_ref[0] = jnp.concatenate(outs, axis=-1)
    del n_rows


def _mem_rows(mem):
    n = mem.shape[0]
    halves = CA_HEAD_DIM // LANES
    return (mem.reshape(n, N_MEM, CA_HEADS, halves, LANES).transpose(0, 1, 3, 2, 4)
            .reshape(n, N_MEM * halves * CA_HEADS, LANES))


def _cross_sample(cq3, mem_k, mem_v):
    n_dec, t_new, _ = cq3.shape
    tok = pl.BlockSpec((1, t_new, D_MODEL), lambda b: (b, 0, 0))
    mem = pl.BlockSpec((1,) + mem_k.shape[1:], lambda b: (b, 0, 0))
    return pl.pallas_call(
        functools.partial(_cross_sample_kernel, t_new=t_new),
        grid=(n_dec,),
        in_specs=[tok, mem, mem],
        out_specs=tok,
        out_shape=jax.ShapeDtypeStruct(cq3.shape, F32),
        compiler_params=_cparams(1),
        name="cross_sample",
    )(cq3, mem_k, mem_v)


def _ffn_kernel(x_ref, co_ref, wo_ref, g3_ref, wg_ref, wv_ref, cw_ref, cb_ref, wd_ref, prev_ref,
                y_ref, cs_ref, hbuf, *, tm, shift, pad):
    i = pl.program_id(1)

    @pl.when(i == 0)
    def _():
        hbuf[pad - 2 * shift:pad, :] = prev_ref[0]

    x2 = x_ref[...] + _dot(co_ref[...].astype(BF16), wo_ref[...])
    xn = _rms(x2, g3_ref[...]).astype(BF16)
    hbuf[pad:pad + tm, :] = _dot(xn, wg_ref[...])
    hv = _dot(xn, wv_ref[...])
    cw = cw_ref[...]
    conv = (cb_ref[...] + cw[0:1, :] * hbuf[pad - 2 * shift:pad - 2 * shift + tm, :]
            + cw[1:2, :] * hbuf[pad - shift:pad - shift + tm, :]
            + cw[2:3, :] * hbuf[pad:pad + tm, :])
    act = (conv * jax.nn.sigmoid(conv) * hv).astype(BF16)
    y_ref[...] = x2 + _dot(act, wd_ref[...])
    last = hbuf[pad + tm - 2 * shift:pad + tm, :]
    cs_ref[0] = last
    hbuf[pad - 2 * shift:pad, :] = last


def _ffn(x1, co, wo, g3, wg, wv, cw, cb, wd, prev, tm, shift):
    n = x1.shape[0]
    n_seq_blocks = prev.shape[0]
    per_b = n // n_seq_blocks // tm
    pad = -(-2 * shift // SUBLANES) * SUBLANES
    tok = pl.BlockSpec((tm, D_MODEL), lambda b, i: (b * per_b + i, 0))
    state = pl.BlockSpec((1, 2 * shift, FFN_HIDDEN), lambda b, i: (b, 0, 0))
    return pl.pallas_call(
        functools.partial(_ffn_kernel, tm=tm, shift=shift, pad=pad),
        grid=(n_seq_blocks, per_b),
        in_specs=[tok, tok, _const_spec((D_MODEL, D_MODEL)), _const_spec((1, D_MODEL)),
                  _const_spec((D_MODEL, FFN_HIDDEN)), _const_spec((D_MODEL, FFN_HIDDEN)),
                  _const_spec((CONV_WIDTH, FFN_HIDDEN)), _const_spec((1, FFN_HIDDEN)),
                  _const_spec((FFN_HIDDEN, D_MODEL)), state],
        out_specs=[tok, state],
        out_shape=[jax.ShapeDtypeStruct((n, D_MODEL), F32),
                   jax.ShapeDtypeStruct(prev.shape, F32)],
        scratch_shapes=[pltpu.VMEM((pad + tm, FFN_HIDDEN), F32)],
        compiler_params=_cparams(2),
        name="ffn",
    )(x1, co, wo, g3, wg, wv, cw, cb, wd, prev)


def _block_diag(blocks, n_outer):
    n, r, c = blocks.shape
    per = n // n_outer
    b = blocks.reshape(n_outer, per, r, c)
    eye = jnp.eye(per, dtype=blocks.dtype)
    full = b[:, :, :, None, :] * eye[None, :, None, :, None]
    return full.reshape(n_outer, per * r, per * c)


def _lam_groups(lb, n_seq):
    halves = lb.reshape(2, 1, N_SLABS, LANES)
    rows = jnp.broadcast_to(halves, (2, n_seq, N_SLABS, LANES))
    rows = rows.reshape(2 * n_seq // SUBLANES, SUBLANES, N_SLABS, LANES)
    return rows.transpose(0, 2, 1, 3)


def _layer(l, x_prompt, x_sample, mem_prompt, cache_k, cache_v, page_table,
           state_ssm_re, state_ssm_im, state_conv, cache_mem_k, cache_mem_v, p):
    bsz, seq, _ = x_prompt.shape
    n_dec, t_new, _ = x_sample.shape
    n_p = bsz * seq
    n_s = n_dec * t_new
    past_len = page_table.shape[1] * PAGE_SIZE
    lam0 = 0.8 - 0.6 * math.exp(-0.3 * l)
    out_scale = 1.0 - lam0
    row = lambda v: v.reshape(1, -1).astype(F32)
    bf = lambda w: w.astype(BF16)

    lb_re, lb_im, bb_re, bb_im, lam = _prep(
        p['ssm_a_re'], p['ssm_a_im'], p['ssm_log_dt'], p['ssm_b_re'], p['ssm_b_im'],
        p['lam_q1'], p['lam_k1'], p['lam_q2'], p['lam_k2'], lam0)

    def b_mats(bbt):
        blocks = bbt.reshape(SSM_GROUP_CH, SSM_GROUPS, SSM_STATE).transpose(1, 0, 2)
        return bf(_block_diag(blocks, 4))

    def c_mats(cm):
        return bf(_block_diag(cm.transpose(0, 2, 1), 4))

    mats = (b_mats(bb_re), b_mats(bb_im), c_mats(p['ssm_c_re']), c_mats(p['ssm_c_im']))
    d_row = row(p['ssm_d'])
    glu_b = bf(p['ssm_glu_w'])

    w_in = p['w_in']
    k0, k1 = SSM_WIDTH + ATTN_WIDTH, SSM_WIDTH + 2 * ATTN_WIDTH
    w_uqv = bf(jnp.concatenate([w_in[:, :k0], w_in[:, k1:]], axis=1))
    w_k = bf(w_in[:, k0:k1])
    seg = jnp.kron(jnp.eye(ATTN_WIDTH // ATTN_HEAD_DIM, dtype=F32),
                   jnp.ones((ATTN_HEAD_DIM, ATTN_HEAD_DIM), F32)).astype(BF16)
    qg = jnp.tile(p['q_norm_g'], ATTN_WIDTH // ATTN_HEAD_DIM).reshape(1, ATTN_WIDTH)
    kg = jnp.tile(p['k_norm_g'], ATTN_WIDTH // ATTN_HEAD_DIM).reshape(1, ATTN_WIDTH)
    sg = row(p['subln_g'])
    w_out_b = bf(p['w_out'])
    w1, w2 = w_out_b[:SSM_WIDTH], w_out_b[SSM_WIDTH:]
    wq_b, wk_b, wv_b, wo_b = bf(p['ca_wq']), bf(p['ca_wk']), bf(p['ca_wv']), bf(p['ca_wo'])
    wg_b, wvf_b, wd_b = bf(p['ffn_wg']), bf(p['ffn_wv']), bf(p['ffn_wd'])
    cw = p['ffn_conv_w'].astype(F32)
    cb = row(p['ffn_conv_b'])

    xp = x_prompt.reshape(n_p, D_MODEL)
    attn_bq, attn_bk = 512, 512
    kg_col = jnp.broadcast_to(p['k_norm_g'].reshape(ATTN_HEAD_DIM, 1), (ATTN_HEAD_DIM, 512))
    u_p, q_p, kt_p, v_p, kb_p, vb_p = _in_proj(xp, row(p['ln1_g']), w_uqv, w_k.T, seg, qg, kg_col,
                                               512, BF16, n_batch=bsz, blk=attn_bk)
    zeros_h = jnp.zeros((2 * bsz // SUBLANES, N_SLABS, SUBLANES, LANES), F32)
    ssm_p, hre_p, him_p = _ssm(u_p.reshape(bsz, seq, SSM_WIDTH), mats,
                               (_lam_groups(lb_re, bsz), _lam_groups(lb_im, bsz)),
                               d_row, glu_b, (zeros_h, zeros_h), bsz, 256)
    att_p = _attn_prompt(lam, q_p.reshape(bsz, seq, ATTN_WIDTH), kb_p,
                         vb_p.reshape(bsz, seq, ATTN_WIDTH), sg, attn_bq, out_scale)
    x1_p, cq_p = _out_proj(xp, ssm_p.reshape(n_p, SSM_WIDTH), att_p.reshape(n_p, ATTN_WIDTH),
                           w1, w2, row(p['ln2_g']), wq_b, row(p['ca_q_norm_g']), 512, BF16)
    mk, mv, mkb, mvb = _mem_kv(mem_prompt.reshape(bsz * N_MEM, D_MODEL), row(p['mem_norm_g']),
                               wk_b, wv_b, row(p['ca_k_norm_g']), 256)
    co_p = _cross_prompt(cq_p, mkb.reshape(bsz, N_MEM, D_MODEL), mvb.reshape(bsz, N_MEM, D_MODEL), 512)
    y_p, conv_p = _ffn(x1_p, co_p, wo_b, row(p['ln3_g']), wg_b, wvf_b, cw, cb, wd_b,
                       jnp.zeros((bsz, CONV_WIDTH - 1, FFN_HIDDEN), F32), 512, 1)

    xs = x_sample.reshape(n_s, D_MODEL)
    u_s, q_s, k_s, v_s = _in_proj(xs, row(p['ln1_g']), w_uqv, w_k, seg, qg, kg, n_s, F32)
    h0 = (_state_to_groups(state_ssm_re.reshape(n_dec, N_STATE), n_dec),
          _state_to_groups(state_ssm_im.reshape(n_dec, N_STATE), n_dec))
    ssm_s, hre_s, him_s = _ssm(u_s.reshape(1, n_s, SSM_WIDTH), mats,
                               (_lam_groups(lb_re, n_dec), _lam_groups(lb_im, n_dec)),
                               d_row, glu_b, h0, n_dec, t_new)
    n_phys = cache_k.shape[0]
    cache_kt = cache_k.transpose(0, 2, 3, 4, 1).reshape(n_phys, ATTN_WIDTH, PAGE_SIZE)
    att_s = _attn_sample(page_table, lam, q_s.reshape(n_dec, t_new, ATTN_WIDTH),
                         k_s.reshape(n_dec, t_new, ATTN_WIDTH), v_s.reshape(n_dec, t_new, ATTN_WIDTH),
                         sg, cache_kt, cache_v.reshape(n_phys, PAGE_SIZE * ATTN_HEADS, ATTN_V_DIM),
                         past_len, out_scale)
    x1_s, cq_s = _out_proj(xs, ssm_s.reshape(n_s, SSM_WIDTH), att_s.reshape(n_s, ATTN_WIDTH),
                           w1, w2, row(p['ln2_g']), wq_b, row(p['ca_q_norm_g']), n_s, F32)
    co_s = _cross_sample(cq_s.reshape(n_dec, t_new, D_MODEL),
                         _mem_rows(cache_mem_k), _mem_rows(cache_mem_v))
    tmaj = lambda a: a.reshape(n_dec, t_new, -1).transpose(1, 0, 2).reshape(n_s, -1)
    prev_s = state_conv.transpose(1, 0, 2).reshape(1, (CONV_WIDTH - 1) * n_dec, FFN_HIDDEN)
    y_s, conv_s = _ffn(tmaj(x1_s), tmaj(co_s), wo_b, row(p['ln3_g']), wg_b, wvf_b, cw, cb, wd_b,
                       prev_s, n_s, n_dec)
    y_s = y_s.reshape(t_new, n_dec, D_MODEL).transpose(1, 0, 2)
    conv_s = conv_s.reshape(CONV_WIDTH - 1, n_dec, FFN_HIDDEN).transpose(1, 0, 2)

    return (y_p.reshape(bsz, seq, D_MODEL), y_s,
            kt_p.reshape(bsz, ATTN_HEADS, 2, ATTN_HEAD_DIM, seq).transpose(0, 4, 1, 2, 3),
            v_p.reshape(bsz, seq, ATTN_HEADS, ATTN_V_DIM),
            k_s.reshape(n_dec, t_new, ATTN_HEADS, 2, ATTN_HEAD_DIM),
            v_s.reshape(n_dec, t_new, ATTN_HEADS, ATTN_V_DIM),
            _groups_to_state(hre_p, bsz), _groups_to_state(him_p, bsz),
            _groups_to_state(hre_s, n_dec), _groups_to_state(him_s, n_dec),
            conv_p, conv_s,
            mk.reshape(bsz, N_MEM, CA_HEADS, CA_HEAD_DIM), mv.reshape(bsz, N_MEM, CA_HEADS, CA_HEAD_DIM))


def kernel(x_prompt, x_sample, mem_prompt, cache_k, cache_v, page_table, state_ssm_re, state_ssm_im, state_conv, cache_mem_k, cache_mem_v, ln1_g, w_in, ssm_a_re, ssm_a_im, ssm_b_re, ssm_b_im, ssm_c_re, ssm_c_im, ssm_d, ssm_log_dt, ssm_glu_w, q_norm_g, k_norm_g, lam_q1, lam_k1, lam_q2, lam_k2, subln_g, w_out, ln2_g, mem_norm_g, ca_wq, ca_wk, ca_wv, ca_q_norm_g, ca_k_norm_g, ca_wo, ln3_g, ffn_wg, ffn_wv, ffn_conv_w, ffn_conv_b, ffn_wd):
    params = dict(
        ln1_g=ln1_g, w_in=w_in, ssm_a_re=ssm_a_re, ssm_a_im=ssm_a_im, ssm_b_re=ssm_b_re,
        ssm_b_im=ssm_b_im, ssm_c_re=ssm_c_re, ssm_c_im=ssm_c_im, ssm_d=ssm_d,
        ssm_log_dt=ssm_log_dt, ssm_glu_w=ssm_glu_w, q_norm_g=q_norm_g, k_norm_g=k_norm_g,
        lam_q1=lam_q1, lam_k1=lam_k1, lam_q2=lam_q2, lam_k2=lam_k2, subln_g=subln_g,
        w_out=w_out, ln2_g=ln2_g, mem_norm_g=mem_norm_g, ca_wq=ca_wq, ca_wk=ca_wk, ca_wv=ca_wv,
        ca_q_norm_g=ca_q_norm_g, ca_k_norm_g=ca_k_norm_g, ca_wo=ca_wo, ln3_g=ln3_g,
        ffn_wg=ffn_wg, ffn_wv=ffn_wv, ffn_conv_w=ffn_conv_w, ffn_conv_b=ffn_conv_b, ffn_wd=ffn_wd)
    y_p, y_s = x_prompt, x_sample
    per_layer = []
    for l in range(w_in.shape[0]):
        p = {k: v[l] for k, v in params.items()}
        outs = _layer(l, y_p, y_s, mem_prompt, cache_k[l], cache_v[l], page_table,
                      state_ssm_re[l], state_ssm_im[l], state_conv[l],
                      cache_mem_k[l], cache_mem_v[l], p)
        y_p, y_s = outs[0], outs[1]
        per_layer.append(outs[2:])
    return (y_p, y_s) + tuple(jnp.stack(leaf) for leaf in zip(*per_layer))
```

```python
import functools
import math

import jax
import jax.numpy as jnp
from jax import lax
from jax.experimental import pallas as pl
from jax.experimental.pallas import tpu as pltpu

F32 = jnp.float32
BF16 = jnp.bfloat16

D_MODEL = 1024
SSM_WIDTH = 512
SSM_GROUP_CH = 16
SSM_GROUPS = 32
SSM_STATE = 64
N_STATE = SSM_GROUPS * SSM_STATE
ATTN_WIDTH = 512
ATTN_HEAD_DIM = 64
ATTN_V_DIM = 128
ATTN_HEADS = 4
PAGE_SIZE = 128
N_MEM = 256
CA_HEADS = 4
CA_HEAD_DIM = 256
FFN_HIDDEN = 2816
CONV_WIDTH = 3
NORM_EPS = 1e-6

LANES = 128
SUBLANES = 8
VMEM_LIMIT = 56 * 1024 * 1024

NEG_BIG = -1e30
LOG2E = math.log2(math.e)


def _cparams(n_axes):
    return pltpu.CompilerParams(
        dimension_semantics=("arbitrary",) * n_axes,
        vmem_limit_bytes=VMEM_LIMIT)


def _const_spec(shape):
    nd = len(shape)
    return pl.BlockSpec(shape, lambda *_: (0,) * nd, pipeline_mode=pl.Buffered(1))


def _rms(x, g):
    return x * lax.rsqrt(jnp.mean(x * x, axis=-1, keepdims=True) + NORM_EPS) * g


def _dot(a, b):
    return jnp.dot(a, b, preferred_element_type=F32)


def _dot_nt(a, b):
    return lax.dot_general(a, b, (((1,), (1,)), ((), ())), preferred_element_type=F32)


def _head_rms(x, g, width, scale):
    outs = []
    for h in range(x.shape[-1] // width):
        c = x[:, h * width:(h + 1) * width]
        outs.append(_rms(c, g) * scale)
    return jnp.concatenate(outs, axis=-1)


def _prep_kernel(are_ref, aim_ref, ldt_ref, bre_ref, bim_ref,
                 q1_ref, k1_ref, q2_ref, k2_ref,
                 lbre_ref, lbim_ref, bbre_ref, bbim_ref, lam_ref, *, lam0):
    a_re = are_ref[...]
    a_im = aim_ref[...]
    dt = jnp.exp(ldt_ref[...])
    mag = jnp.exp(a_re * dt)
    lb_re = mag * jnp.cos(a_im * dt)
    lb_im = mag * jnp.sin(a_im * dt)
    den = a_re * a_re + a_im * a_im
    n_re = lb_re - 1.0
    f_re = (n_re * a_re + lb_im * a_im) / den
    f_im = (lb_im * a_re - n_re * a_im) / den
    b_re = bre_ref[...]
    b_im = bim_ref[...]
    lbre_ref[...] = lb_re
    lbim_ref[...] = lb_im
    bbre_ref[...] = f_re * b_re - f_im * b_im
    bbim_ref[...] = f_re * b_im + f_im * b_re
    s1 = jnp.sum(q1_ref[...] * k1_ref[...], axis=-1, keepdims=True)
    s2 = jnp.sum(q2_ref[...] * k2_ref[...], axis=-1, keepdims=True)
    lam_ref[...] = jnp.exp(s1) - jnp.exp(s2) + lam0


def _prep(a_re, a_im, log_dt, b_re, b_im, q1, k1, q2, k2, lam0):
    are = a_re.reshape(1, N_STATE)
    aim = a_im.reshape(1, N_STATE)
    ldt = jnp.repeat(log_dt, SSM_STATE).reshape(1, N_STATE)
    bre = b_re.transpose(2, 0, 1).reshape(SSM_GROUP_CH, N_STATE)
    bim = b_im.transpose(2, 0, 1).reshape(SSM_GROUP_CH, N_STATE)
    vec = lambda v: v.reshape(1, ATTN_HEAD_DIM)
    row = jax.ShapeDtypeStruct((1, N_STATE), F32)
    mat = jax.ShapeDtypeStruct((SSM_GROUP_CH, N_STATE), F32)
    return pl.pallas_call(
        functools.partial(_prep_kernel, lam0=lam0),
        out_shape=(row, row, mat, mat, jax.ShapeDtypeStruct((1, 1), F32)),
        name="prep",
    )(are, aim, ldt, bre, bim, vec(q1), vec(k1), vec(q2), vec(k2))


def _in_proj_kernel(x_ref, g_ref, w_ref, wk_ref, seg_ref, qg_ref, kg_ref,
                    u_ref, q_ref, k_ref, v_ref, *bf_refs, key_major, blk):
    xn = _rms(x_ref[...], g_ref[...]).astype(BF16)
    proj = _dot(xn, w_ref[...])
    seg = seg_ref[...]

    def chunk_norm(z, g):
        ms = _dot((z * z).astype(BF16), seg) * (1.0 / ATTN_HEAD_DIM)
        return z * lax.rsqrt(ms + NORM_EPS) * g

    u_ref[...] = proj[:, :SSM_WIDTH]
    q = proj[:, SSM_WIDTH:SSM_WIDTH + ATTN_WIDTH]
    v = proj[:, SSM_WIDTH + ATTN_WIDTH:]
    q_ref[...] = (chunk_norm(q, qg_ref[...]) * (ATTN_HEAD_DIM ** -0.5 * LOG2E)).astype(q_ref.dtype)
    if key_major:
        kb_ref, vb_ref = bf_refs
        tm = x_ref.shape[0]
        z = _dot_nt(wk_ref[...], xn).reshape(ATTN_WIDTH // ATTN_HEAD_DIM, ATTN_HEAD_DIM, tm)
        ms = jnp.mean(z * z, axis=1, keepdims=True)
        kn = (z * lax.rsqrt(ms + NORM_EPS) * kg_ref[...]).reshape(ATTN_WIDTH, tm)
        k_ref[0] = kn
        for i in range(tm // blk):
            kb_ref[0, i] = kn[:, i * blk:(i + 1) * blk].astype(BF16)
        for h in range(ATTN_HEADS):
            v_ref[:, h, :] = v[:, ATTN_V_DIM * h:ATTN_V_DIM * (h + 1)]
        vb_ref[...] = v.astype(BF16)
    else:
        k_ref[...] = chunk_norm(_dot(xn, wk_ref[...]), kg_ref[...])
        v_ref[...] = v


def _in_proj(x, ln1_g, w_uqv, wk, seg, qg, kg, tm, q_dtype, n_batch=None, blk=None):
    n = x.shape[0]
    key_major = n_batch is not None
    bsz = n_batch if key_major else 1
    per_b = n // bsz // tm
    tok = lambda width: pl.BlockSpec((tm, width), lambda b, i: (b * per_b + i, 0))
    out_specs = [tok(SSM_WIDTH), tok(ATTN_WIDTH)]
    out_shape = [jax.ShapeDtypeStruct((n, SSM_WIDTH), F32),
                 jax.ShapeDtypeStruct((n, ATTN_WIDTH), q_dtype)]
    if key_major:
        t = n // bsz
        out_specs += [pl.BlockSpec((1, ATTN_WIDTH, tm), lambda b, i: (b, 0, i)),
                      pl.BlockSpec((tm, ATTN_HEADS, ATTN_V_DIM), lambda b, i: (b * per_b + i, 0, 0)),
                      pl.BlockSpec((1, tm // blk, ATTN_WIDTH, blk), lambda b, i: (b, i, 0, 0)),
                      tok(ATTN_WIDTH)]
        out_shape += [jax.ShapeDtypeStruct((bsz, ATTN_WIDTH, t), F32),
                      jax.ShapeDtypeStruct((n, ATTN_HEADS, ATTN_V_DIM), F32),
                      jax.ShapeDtypeStruct((bsz, t // blk, ATTN_WIDTH, blk), BF16),
                      jax.ShapeDtypeStruct((n, ATTN_WIDTH), BF16)]
    else:
        out_specs += [tok(ATTN_WIDTH), tok(ATTN_WIDTH)]
        out_shape += [jax.ShapeDtypeStruct((n, ATTN_WIDTH), F32)] * 2
    return pl.pallas_call(
        functools.partial(_in_proj_kernel, key_major=key_major, blk=blk),
        grid=(bsz, per_b),
        in_specs=[tok(D_MODEL), _const_spec((1, D_MODEL)), _const_spec(w_uqv.shape),
                  _const_spec(wk.shape), _const_spec((ATTN_WIDTH, ATTN_WIDTH)),
                  _const_spec((1, ATTN_WIDTH)), _const_spec(kg.shape)],
        out_specs=out_specs,
        out_shape=out_shape,
        compiler_params=_cparams(2),
        name="in_proj",
    )(x, ln1_g, w_uqv, wk, seg, qg, kg)


N_SLABS = N_STATE // 2 // LANES
HALF = N_STATE // 2


def _gelu_tanh(x):
    c = math.sqrt(2.0 / math.pi)
    return 0.5 * x * (1.0 + jnp.tanh(c * (x + 0.044715 * (x * x * x))))


def _ssm_kernel(u_ref, bre_ref, bim_ref, cre_ref, cim_ref, lre_ref, lim_ref,
                d_ref, glu_ref, h0re_ref, h0im_ref,
                out_ref, hre_ref, him_ref, sre, sim,
                *, n_seq, tc, pitch, n_bulk, rows_bulk, pitch_bulk):
    c = pl.program_id(0)
    rows = n_seq * tc
    n_groups = 2 * n_seq // SUBLANES

    @pl.when(c == 0)
    def _():
        hre_ref[...] = h0re_ref[...]
        him_ref[...] = h0im_ref[...]

    u = u_ref[...].reshape(rows, SSM_WIDTH)
    ub = u.astype(BF16)

    for j in range(4):
        hh, jj = divmod(j, 2)
        uj = ub[:, LANES * j:LANES * (j + 1)]
        for src, dst in ((bre_ref, sre), (bim_ref, sim)):
            bu = _dot(uj, src[j])
            for kk in range(4):
                slab = 4 * jj + kk
                for bb in range(n_bulk):
                    r0 = (hh * n_bulk + bb) * pitch_bulk
                    dst[slab, r0:r0 + rows_bulk, :] = (
                        bu[bb * rows_bulk:(bb + 1) * rows_bulk, LANES * kk:LANES * (kk + 1)])

    def group_body(g, carry):
        base = g * (SUBLANES * pitch)
        lr = [lre_ref[g, k] for k in range(N_SLABS)]
        li = [lim_ref[g, k] for k in range(N_SLABS)]
        hr0 = tuple(hre_ref[g, k] for k in range(N_SLABS))
        hi0 = tuple(him_ref[g, k] for k in range(N_SLABS))

        def step(t, hc):
            hr, hi = hc
            idx = pl.ds(base + t, SUBLANES, stride=pitch)
            nr, ni = [], []
            for k in range(N_SLABS):
                br = sre[k, idx, :]
                bi = sim[k, idx, :]
                r = lr[k] * hr[k] - li[k] * hi[k] + br
                i = lr[k] * hi[k] + li[k] * hr[k] + bi
                sre[k, idx, :] = r
                sim[k, idx, :] = i
                nr.append(r)
                ni.append(i)
            return tuple(nr), tuple(ni)

        hr, hi = lax.fori_loop(0, tc, step, (hr0, hi0), unroll=min(tc, 4))
        for k in range(N_SLABS):
            hre_ref[g, k] = hr[k]
            him_ref[g, k] = hi[k]
        return carry

    lax.fori_loop(0, n_groups, group_body, 0)

    y_rows = []
    for bb in range(n_bulk):
        y_cols = []
        for j in range(4):
            hh, jj = divmod(j, 2)
            r0 = (hh * n_bulk + bb) * pitch_bulk
            hr = jnp.concatenate(
                [sre[4 * jj + kk, r0:r0 + rows_bulk, :] for kk in range(4)], axis=1).astype(BF16)
            hi = jnp.concatenate(
                [sim[4 * jj + kk, r0:r0 + rows_bulk, :] for kk in range(4)], axis=1).astype(BF16)
            y_cols.append(_dot(hr, cre_ref[j]) - _dot(hi, cim_ref[j]))
        y_rows.append(jnp.concatenate(y_cols, axis=1))
    y = jnp.concatenate(y_rows, axis=0) if n_bulk > 1 else y_rows[0]
    y = y + d_ref[...] * u
    gl = _gelu_tanh(y)
    gate = jax.nn.sigmoid(_dot(gl.astype(BF16), glu_ref[...]))
    out_ref[...] = (gl * gate).astype(out_ref.dtype).reshape(out_ref.shape)


def _ssm(u3, mats, lam_g, d_row, glu_b, h0, n_seq, tc):
    bre, bim, cre, cim = mats
    lre_g, lim_g = lam_g
    h0re, h0im = h0
    n_groups = 2 * n_seq // SUBLANES
    t_total = u3.shape[0] * u3.shape[1] // n_seq
    n_chunks = t_total // tc
    if tc % SUBLANES == 0:
        pitch = tc + SUBLANES
        n_bulk, rows_bulk, pitch_bulk = n_seq, tc, pitch
        ublock = (n_seq, tc, SSM_WIDTH)
    else:
        assert n_chunks == 1
        pitch = tc
        n_bulk, rows_bulk, pitch_bulk = 1, n_seq * tc, n_seq * tc
        ublock = (1, n_seq * tc, SSM_WIDTH)
    scr_rows = 2 * n_bulk * pitch_bulk
    hshape = (n_groups, N_SLABS, SUBLANES, LANES)
    kern = functools.partial(_ssm_kernel, n_seq=n_seq, tc=tc, pitch=pitch,
                             n_bulk=n_bulk, rows_bulk=rows_bulk, pitch_bulk=pitch_bulk)
    return pl.pallas_call(
        kern,
        grid=(n_chunks,),
        in_specs=[pl.BlockSpec(ublock, lambda c: (0, c, 0)),
                  _const_spec(bre.shape), _const_spec(bim.shape),
                  _const_spec(cre.shape), _const_spec(cim.shape),
                  _const_spec(hshape), _const_spec(hshape),
                  _const_spec((1, SSM_WIDTH)), _const_spec((SSM_WIDTH, SSM_WIDTH)),
                  _const_spec(hshape), _const_spec(hshape)],
        out_specs=[pl.BlockSpec(ublock, lambda c: (0, c, 0)),
                   _const_spec(hshape), _const_spec(hshape)],
        out_shape=[jax.ShapeDtypeStruct(u3.shape, BF16),
                   jax.ShapeDtypeStruct(hshape, F32),
                   jax.ShapeDtypeStruct(hshape, F32)],
        scratch_shapes=[pltpu.VMEM((N_SLABS, scr_rows, LANES), F32),
                        pltpu.VMEM((N_SLABS, scr_rows, LANES), F32)],
        compiler_params=_cparams(1),
        name="ssm",
    )(u3, bre, bim, cre, cim, lre_g, lim_g, d_row, glu_b, h0re, h0im)


def _state_to_groups(h, n_seq):
    x = h.reshape(n_seq, 2, N_SLABS, LANES).transpose(1, 0, 2, 3)
    x = x.reshape(2 * n_seq // SUBLANES, SUBLANES, N_SLABS, LANES)
    return x.transpose(0, 2, 1, 3)


def _groups_to_state(x, n_seq):
    x = x.transpose(0, 2, 1, 3).reshape(2, n_seq, N_SLABS, LANES)
    return x.transpose(1, 0, 2, 3).reshape(n_seq, SSM_GROUPS, SSM_STATE)


def _head_slope(h):
    return jnp.where(h == 0, 2.0 ** -2, jnp.where(h == 1, 2.0 ** -4,
                     jnp.where(h == 2, 2.0 ** -6, 2.0 ** -8))).astype(F32)


def _attn_prompt_kernel(lam_ref, q_ref, k_ref, v_ref, sg_ref, o_ref,
                        q_scr, s_scr, m_scr, l_scr, acc_scr, *, bq, bk, out_scale):
    h = pl.program_id(1)
    qi = pl.program_id(2)
    slope = _head_slope(h) * LOG2E
    lam = lam_ref[0, 0]
    n_maps = 2

    q = q_ref[0]
    lane = lax.broadcasted_iota(jnp.int32, q.shape, 1)
    zero = jnp.zeros_like(q)
    q_scr[0] = jnp.where(lane < ATTN_HEAD_DIM, q, zero)
    q_scr[1] = jnp.where(lane >= ATTN_HEAD_DIM, q, zero)

    col = lax.broadcasted_iota(jnp.int32, (1, bk), 1)
    q0 = qi * bq

    def lane_tiles(x):
        return [x[:, LANES * i:LANES * (i + 1)] for i in range(bk // LANES)]

    n_full = q0 // bk

    def run_pairs(first, count, step):
        odd = lax.rem(count, 2)

        @pl.when(odd == 1)
        def _():
            step(first)

        def body(t, carry):
            j = first + odd + 2 * t
            step(j)
            step(j + 1)
            return carry

        lax.fori_loop(0, count // 2, body, 0)

    def score_step(j, masked=False):
        kpos = col + j * bk
        colbias = slope * (kpos - q0).astype(F32)
        kt = k_ref[0, j]
        for i in range(n_maps):
            s = _dot(q_scr[i], kt) + colbias
            if masked:
                qpos = lax.broadcasted_iota(jnp.int32, (bq, bk), 0) + q0
                s = jnp.where(qpos >= kpos, s, -jnp.inf)
            s_scr[j, i] = s
            mpart = m_scr[i]
            for piece in lane_tiles(s):
                mpart = jnp.maximum(mpart, piece)
            m_scr[i] = mpart

    m_scr[...] = jnp.full(m_scr.shape, -jnp.inf, F32)
    score_step(n_full, masked=True)
    run_pairs(0, n_full, score_step)
    ms = [jnp.max(m_scr[i], axis=-1, keepdims=True) for i in range(n_maps)]

    def sum_step(j):
        start = pl.multiple_of(j * bk, bk)
        vb = v_ref[0, pl.ds(start, bk), :]
        for i in range(n_maps):
            p = jnp.exp2(s_scr[j, i] - ms[i])
            lpart = l_scr[i]
            for piece in lane_tiles(p):
                lpart = lpart + piece
            l_scr[i] = lpart
            acc_scr[i] += _dot(p.astype(BF16), vb)

    l_scr[...] = jnp.zeros(l_scr.shape, F32)
    acc_scr[...] = jnp.zeros(acc_scr.shape, F32)
    run_pairs(0, n_full + 1, sum_step)
    outs = [acc_scr[i] / jnp.sum(l_scr[i], axis=-1, keepdims=True) for i in range(n_maps)]
    o = outs[0] - lam * outs[1]
    o_ref[0] = (_rms(o, sg_ref[...]) * out_scale).astype(o_ref.dtype)


def _attn_prompt(lam, qb, kb, vb, sg, bq, out_scale):
    bsz, t, _ = qb.shape
    bk = kb.shape[-1]
    return pl.pallas_call(
        functools.partial(_attn_prompt_kernel, bq=bq, bk=bk, out_scale=out_scale),
        grid=(bsz, ATTN_HEADS, t // bq),
        in_specs=[pl.BlockSpec(memory_space=pltpu.SMEM),
                  pl.BlockSpec((1, bq, ATTN_V_DIM), lambda b, h, i: (b, i, h)),
                  pl.BlockSpec((1, t // bk, ATTN_V_DIM, bk), lambda b, h, i: (b, 0, h, 0)),
                  pl.BlockSpec((1, t, ATTN_V_DIM), lambda b, h, i: (b, 0, h)),
                  _const_spec((1, ATTN_V_DIM))],
        out_specs=pl.BlockSpec((1, bq, ATTN_V_DIM), lambda b, h, i: (b, i, h)),
        out_shape=jax.ShapeDtypeStruct(qb.shape, BF16),
        scratch_shapes=[pltpu.VMEM((2, bq, ATTN_V_DIM), BF16),
                        pltpu.VMEM((t // bk, 2, bq, bk), F32),
                        pltpu.VMEM((2, bq, LANES), F32),
                        pltpu.VMEM((2, bq, LANES), F32),
                        pltpu.VMEM((2, bq, ATTN_V_DIM), F32)],
        compiler_params=_cparams(3),
        name="attn_prompt",
    )(lam, qb, kb, vb, sg)


def _attn_sample_kernel(pt_ref, lam_ref, q_ref, kn_ref, vn_ref, sg_ref, *refs,
                        n_pages, t_new, past_len, out_scale):
    k_refs = refs[:n_pages]
    v_refs = refs[n_pages:2 * n_pages]
    o_ref = refs[2 * n_pages]
    del pt_ref
    lam = lam_ref[0, 0]
    n_rows = 2 * ATTN_HEADS * t_new

    q4 = q_ref[0]
    qe = jnp.concatenate([q4] * (2 * ATTN_HEADS), axis=0)
    row = lax.broadcasted_iota(jnp.int32, qe.shape, 0)
    lane = lax.broadcasted_iota(jnp.int32, qe.shape, 1)
    qe = jnp.where(lane // ATTN_HEAD_DIM == row // t_new, qe, 0.0)
    qeb = qe.astype(BF16)

    rcol = lax.broadcasted_iota(jnp.int32, (n_rows, 1), 0)
    head_c = rcol // (2 * t_new)
    tok_c = rcol % t_new
    slope = _head_slope(head_c) * LOG2E
    rows_h = 2 * t_new

    kt = jnp.concatenate([k_refs[p][0].astype(BF16) for p in range(n_pages)], axis=1)
    kpos = lax.broadcasted_iota(jnp.int32, (1, n_pages * PAGE_SIZE), 1) - past_len
    s = _dot(qeb, kt) + slope * kpos.astype(F32)

    kn = kn_ref[0]
    vn = vn_ref[0]
    s_new = []
    for c in range(t_new):
        sc = jnp.sum(qe * kn[c:c + 1, :], axis=-1, keepdims=True) + slope * float(c)
        s_new.append(jnp.where(tok_c >= c, sc, -jnp.inf))

    m = jnp.max(s, axis=-1, keepdims=True)
    for sc in s_new:
        m = jnp.maximum(m, sc)
    pe = jnp.exp2(s - m)
    l = jnp.sum(pe, axis=-1, keepdims=True)

    pv = []
    for h in range(ATTN_HEADS):
        vh = jnp.concatenate(
            [v_refs[p][0, pl.ds(h, PAGE_SIZE, stride=ATTN_HEADS), :].astype(BF16)
             for p in range(n_pages)], axis=0)
        pv.append(_dot(pe[rows_h * h:rows_h * (h + 1)].astype(BF16), vh))
    acc = jnp.concatenate(pv, axis=0)

    for c in range(t_new):
        pc = jnp.exp2(s_new[c] - m)
        l = l + pc
        vc = jnp.concatenate(
            [jnp.broadcast_to(vn[c:c + 1, ATTN_V_DIM * h:ATTN_V_DIM * (h + 1)], (rows_h, ATTN_V_DIM))
             for h in range(ATTN_HEADS)], axis=0)
        acc = acc + pc * vc

    acc = acc / l
    outs = []
    for h in range(ATTN_HEADS):
        a0 = acc[rows_h * h:rows_h * h + t_new]
        a1 = acc[rows_h * h + t_new:rows_h * (h + 1)]
        outs.append(_rms(a0 - lam * a1, sg_ref[...]) * out_scale)
    o_ref[0] = jnp.concatenate(outs, axis=-1)


def _attn_sample(page_table, lam, q3, kn3, vn3, sg, cache_kt, cache_v3, past_len, out_scale):
    n_dec, t_new, _ = q3.shape
    n_pages = page_table.shape[1]
    pt = page_table.reshape(-1)
    tok_spec = pl.BlockSpec((1, t_new, ATTN_WIDTH), lambda b, pt: (b, 0, 0))

    def k_spec(p):
        return pl.BlockSpec((1, ATTN_WIDTH, PAGE_SIZE),
                            lambda b, pt, p=p: (pt[b * n_pages + p], 0, 0))

    def v_spec(p):
        return pl.BlockSpec((1, PAGE_SIZE * ATTN_HEADS, ATTN_V_DIM),
                            lambda b, pt, p=p: (pt[b * n_pages + p], 0, 0))

    grid_spec = pltpu.PrefetchScalarGridSpec(
        num_scalar_prefetch=1,
        grid=(n_dec,),
        in_specs=[pl.BlockSpec(memory_space=pltpu.SMEM), tok_spec, tok_spec, tok_spec,
                  pl.BlockSpec((1, ATTN_V_DIM), lambda b, pt: (0, 0))]
                 + [k_spec(p) for p in range(n_pages)] + [v_spec(p) for p in range(n_pages)],
        out_specs=tok_spec)
    kern = functools.partial(_attn_sample_kernel, n_pages=n_pages, t_new=t_new,
                             past_len=past_len, out_scale=out_scale)
    return pl.pallas_call(
        kern,
        grid_spec=grid_spec,
        out_shape=jax.ShapeDtypeStruct(q3.shape, F32),
        compiler_params=_cparams(1),
        name="attn_sample",
    )(pt, lam, q3, kn3, vn3, sg, *([cache_kt] * n_pages), *([cache_v3] * n_pages))


def _out_proj_kernel(x_ref, s_ref, o_ref, w1_ref, w2_ref, g2_ref, wq_ref, qg_ref,
                     x1_ref, cq_ref):
    x1 = (x_ref[...] + _dot(s_ref[...].astype(BF16), w1_ref[...])
          + _dot(o_ref[...].astype(BF16), w2_ref[...]))
    x1_ref[...] = x1
    xn = _rms(x1, g2_ref[...]).astype(BF16)
    cq = _dot(xn, wq_ref[...])
    cq_ref[...] = _head_rms(cq, qg_ref[...], CA_HEAD_DIM, CA_HEAD_DIM ** -0.5).astype(cq_ref.dtype)


def _out_proj(x, ssm_o, att_o, w1, w2, g2, wq, qg, tm, cq_dtype):
    n = x.shape[0]
    tok = lambda width: pl.BlockSpec((tm, width), lambda i: (i, 0))
    return pl.pallas_call(
        _out_proj_kernel,
        grid=(n // tm,),
        in_specs=[tok(D_MODEL), tok(SSM_WIDTH), tok(ATTN_WIDTH),
                  _const_spec((SSM_WIDTH, D_MODEL)), _const_spec((ATTN_WIDTH, D_MODEL)),
                  _const_spec((1, D_MODEL)), _const_spec((D_MODEL, D_MODEL)),
                  _const_spec((1, CA_HEAD_DIM))],
        out_specs=[tok(D_MODEL), tok(D_MODEL)],
        out_shape=[jax.ShapeDtypeStruct((n, D_MODEL), F32),
                   jax.ShapeDtypeStruct((n, D_MODEL), cq_dtype)],
        compiler_params=_cparams(1),
        name="out_proj",
    )(x, ssm_o, att_o, w1, w2, g2, wq, qg)


def _mem_kv_kernel(m_ref, g_ref, wk_ref, wv_ref, kg_ref, mk_ref, mv_ref, mkb_ref, mvb_ref):
    mn = _rms(m_ref[...], g_ref[...]).astype(BF16)
    mk = _head_rms(_dot(mn, wk_ref[...]), kg_ref[...], CA_HEAD_DIM, 1.0)
    mv = _dot(mn, wv_ref[...])
    mk_ref[...] = mk
    mv_ref[...] = mv
    mkb_ref[...] = mk.astype(BF16)
    mvb_ref[...] = mv.astype(BF16)


def _mem_kv(mem, g, wk, wv, kg, tm):
    n = mem.shape[0]
    tok = pl.BlockSpec((tm, D_MODEL), lambda i: (i, 0))
    wspec = _const_spec((D_MODEL, D_MODEL))
    return pl.pallas_call(
        _mem_kv_kernel,
        grid=(n // tm,),
        in_specs=[tok, _const_spec((1, D_MODEL)), wspec, wspec, _const_spec((1, CA_HEAD_DIM))],
        out_specs=[tok, tok, tok, tok],
        out_shape=[jax.ShapeDtypeStruct((n, D_MODEL), F32)] * 2
                  + [jax.ShapeDtypeStruct((n, D_MODEL), BF16)] * 2,
        compiler_params=_cparams(1),
        name="mem_kv",
    )(mem, g, wk, wv, kg)


def _cross_prompt_kernel(cq_ref, mk_ref, mv_ref, co_ref):
    cq = cq_ref[...]
    outs = []
    for h in range(CA_HEADS):
        sl = slice(CA_HEAD_DIM * h, CA_HEAD_DIM * (h + 1))
        s = _dot_nt(cq[:, sl], mk_ref[0, :, sl])
        p = jnp.exp(s - jnp.max(s, axis=-1, keepdims=True))
        l = jnp.sum(p, axis=-1, keepdims=True)
        outs.append(_dot((p / l).astype(BF16), mv_ref[0, :, sl]))
    co_ref[...] = jnp.concatenate(outs, axis=-1).astype(co_ref.dtype)


def _cross_prompt(cq, mkb, mvb, tm):
    n = cq.shape[0]
    bsz = mkb.shape[0]
    per_b = n // bsz // tm
    tok = pl.BlockSpec((tm, D_MODEL), lambda b, i: (b * per_b + i, 0))
    mem = pl.BlockSpec((1, N_MEM, D_MODEL), lambda b, i: (b, 0, 0))
    return pl.pallas_call(
        _cross_prompt_kernel,
        grid=(bsz, per_b),
        in_specs=[tok, mem, mem],
        out_specs=tok,
        out_shape=jax.ShapeDtypeStruct((n, D_MODEL), BF16),
        compiler_params=_cparams(2),
        name="cross_prompt",
    )(cq, mkb, mvb)


def _cross_sample_kernel(cq_ref, mk_ref, mv_ref, co_ref, *, t_new):
    halves = CA_HEAD_DIM // LANES
    per_key = halves * CA_HEADS

    def heads(ref, b):
        return jnp.concatenate(
            [ref[b, pl.ds(half * CA_HEADS + h, N_MEM, stride=per_key), :]
             for h in range(CA_HEADS) for half in range(halves)], axis=1).astype(BF16)

    for b in range(cq_ref.shape[0]):
        qe = jnp.concatenate([cq_ref[b]] * CA_HEADS, axis=0)
        row = lax.broadcasted_iota(jnp.int32, qe.shape, 0)
        lane = lax.broadcasted_iota(jnp.int32, qe.shape, 1)
        qe = jnp.where(lane // CA_HEAD_DIM == row // t_new, qe, 0.0).astype(BF16)
        s = _dot_nt(qe, heads(mk_ref, b))
        p = jnp.exp(s - jnp.max(s, axis=-1, keepdims=True))
        l = jnp.sum(p, axis=-1, keepdims=True)
        full = _dot((p / l).astype(BF16), heads(mv_ref, b))
        outs = [full[h * t_new:(h + 1) * t_new, CA_HEAD_DIM * h:CA_HEAD_DIM * (h + 1)]
                for h in range(CA_HEADS)]
        co_ref[b] = jnp.concatenate(outs, axis=-1)


def _mem_rows(mem):
    n = mem.shape[0]
    halves = CA_HEAD_DIM // LANES
    return (mem.reshape(n, N_MEM, CA_HEADS, halves, LANES).transpose(0, 1, 3, 2, 4)
            .reshape(n, N_MEM * halves * CA_HEADS, LANES))


def _cross_sample(cq3, mem_k, mem_v):
    n_dec, t_new, _ = cq3.shape
    per_step = 4
    tok = pl.BlockSpec((per_step, t_new, D_MODEL), lambda b: (b, 0, 0))
    mem = pl.BlockSpec((per_step,) + mem_k.shape[1:], lambda b: (b, 0, 0))
    return pl.pallas_call(
        functools.partial(_cross_sample_kernel, t_new=t_new),
        grid=(n_dec // per_step,),
        in_specs=[tok, mem, mem],
        out_specs=tok,
        out_shape=jax.ShapeDtypeStruct(cq3.shape, F32),
        compiler_params=_cparams(1),
        name="cross_sample",
    )(cq3, mem_k, mem_v)


def _ffn_kernel(x_ref, co_ref, wo_ref, g3_ref, wg_ref, wv_ref, cw_ref, cb_ref, wd_ref, prev_ref,
                y_ref, cs_ref, hbuf, *, tm, shift, pad):
    i = pl.program_id(1)

    @pl.when(i == 0)
    def _():
        hbuf[pad - 2 * shift:pad, :] = prev_ref[0]

    x2 = x_ref[...] + _dot(co_ref[...].astype(BF16), wo_ref[...])
    xn = _rms(x2, g3_ref[...]).astype(BF16)
    hbuf[pad:pad + tm, :] = _dot(xn, wg_ref[...])
    hv = _dot(xn, wv_ref[...])
    cw = cw_ref[...]
    conv = (cb_ref[...] + cw[0:1, :] * hbuf[pad - 2 * shift:pad - 2 * shift + tm, :]
            + cw[1:2, :] * hbuf[pad - shift:pad - shift + tm, :]
            + cw[2:3, :] * hbuf[pad:pad + tm, :])
    act = (conv * jax.nn.sigmoid(conv) * hv).astype(BF16)
    y_ref[...] = x2 + _dot(act, wd_ref[...])
    last = hbuf[pad + tm - 2 * shift:pad + tm, :]
    cs_ref[0] = last
    hbuf[pad - 2 * shift:pad, :] = last


def _ffn(x1, co, wo, g3, wg, wv, cw, cb, wd, prev, tm, shift):
    n = x1.shape[0]
    n_seq_blocks = prev.shape[0]
    per_b = n // n_seq_blocks // tm
    pad = -(-2 * shift // SUBLANES) * SUBLANES
    tok = pl.BlockSpec((tm, D_MODEL), lambda b, i: (b * per_b + i, 0))
    state = pl.BlockSpec((1, 2 * shift, FFN_HIDDEN), lambda b, i: (b, 0, 0))
    return pl.pallas_call(
        functools.partial(_ffn_kernel, tm=tm, shift=shift, pad=pad),
        grid=(n_seq_blocks, per_b),
        in_specs=[tok, tok, _const_spec((D_MODEL, D_MODEL)), _const_spec((1, D_MODEL)),
                  _const_spec((D_MODEL, FFN_HIDDEN)), _const_spec((D_MODEL, FFN_HIDDEN)),
                  _const_spec((CONV_WIDTH, FFN_HIDDEN)), _const_spec((1, FFN_HIDDEN)),
                  _const_spec((FFN_HIDDEN, D_MODEL)), state],
        out_specs=[tok, state],
        out_shape=[jax.ShapeDtypeStruct((n, D_MODEL), F32),
                   jax.ShapeDtypeStruct(prev.shape, F32)],
        scratch_shapes=[pltpu.VMEM((pad + tm, FFN_HIDDEN), F32)],
        compiler_params=_cparams(2),
        name="ffn",
    )(x1, co, wo, g3, wg, wv, cw, cb, wd, prev)


def _block_diag(blocks, n_outer):
    n, r, c = blocks.shape
    per = n // n_outer
    b = blocks.reshape(n_outer, per, r, c)
    eye = jnp.eye(per, dtype=blocks.dtype)
    full = b[:, :, :, None, :] * eye[None, :, None, :, None]
    return full.reshape(n_outer, per * r, per * c)


def _lam_groups(lb, n_seq):
    halves = lb.reshape(2, 1, N_SLABS, LANES)
    rows = jnp.broadcast_to(halves, (2, n_seq, N_SLABS, LANES))
    rows = rows.reshape(2 * n_seq // SUBLANES, SUBLANES, N_SLABS, LANES)
    return rows.transpose(0, 2, 1, 3)


def _layer(l, x_prompt, x_sample, mem_prompt, cache_k, cache_v, page_table,
           state_ssm_re, state_ssm_im, state_conv, cache_mem_k, cache_mem_v, p):
    bsz, seq, _ = x_prompt.shape
    n_dec, t_new, _ = x_sample.shape
    n_p = bsz * seq
    n_s = n_dec * t_new
    past_len = page_table.shape[1] * PAGE_SIZE
    lam0 = 0.8 - 0.6 * math.exp(-0.3 * l)
    out_scale = 1.0 - lam0
    row = lambda v: v.reshape(1, -1).astype(F32)
    bf = lambda w: w.astype(BF16)

    lb_re, lb_im, bb_re, bb_im, lam = _prep(
        p['ssm_a_re'], p['ssm_a_im'], p['ssm_log_dt'], p['ssm_b_re'], p['ssm_b_im'],
        p['lam_q1'], p['lam_k1'], p['lam_q2'], p['lam_k2'], lam0)

    def b_mats(bbt):
        blocks = bbt.reshape(SSM_GROUP_CH, SSM_GROUPS, SSM_STATE).transpose(1, 0, 2)
        return bf(_block_diag(blocks, 4))

    def c_mats(cm):
        return bf(_block_diag(cm.transpose(0, 2, 1), 4))

    mats = (b_mats(bb_re), b_mats(bb_im), c_mats(p['ssm_c_re']), c_mats(p['ssm_c_im']))
    d_row = row(p['ssm_d'])
    glu_b = bf(p['ssm_glu_w'])

    w_in = p['w_in']
    k0, k1 = SSM_WIDTH + ATTN_WIDTH, SSM_WIDTH + 2 * ATTN_WIDTH
    w_uqv = bf(jnp.concatenate([w_in[:, :k0], w_in[:, k1:]], axis=1))
    w_k = bf(w_in[:, k0:k1])
    seg = jnp.kron(jnp.eye(ATTN_WIDTH // ATTN_HEAD_DIM, dtype=F32),
                   jnp.ones((ATTN_HEAD_DIM, ATTN_HEAD_DIM), F32)).astype(BF16)
    qg = jnp.tile(p['q_norm_g'], ATTN_WIDTH // ATTN_HEAD_DIM).reshape(1, ATTN_WIDTH)
    kg = jnp.tile(p['k_norm_g'], ATTN_WIDTH // ATTN_HEAD_DIM).reshape(1, ATTN_WIDTH)
    sg = row(p['subln_g'])
    w_out_b = bf(p['w_out'])
    w1, w2 = w_out_b[:SSM_WIDTH], w_out_b[SSM_WIDTH:]
    wq_b, wk_b, wv_b, wo_b = bf(p['ca_wq']), bf(p['ca_wk']), bf(p['ca_wv']), bf(p['ca_wo'])
    wg_b, wvf_b, wd_b = bf(p['ffn_wg']), bf(p['ffn_wv']), bf(p['ffn_wd'])
    cw = p['ffn_conv_w'].astype(F32)
    cb = row(p['ffn_conv_b'])

    xp = x_prompt.reshape(n_p, D_MODEL)
    attn_bq, attn_bk = 512, 512
    kg_col = jnp.broadcast_to(p['k_norm_g'].reshape(ATTN_HEAD_DIM, 1), (ATTN_HEAD_DIM, 512))
    u_p, q_p, kt_p, v_p, kb_p, vb_p = _in_proj(xp, row(p['ln1_g']), w_uqv, w_k.T, seg, qg, kg_col,
                                               512, BF16, n_batch=bsz, blk=attn_bk)
    zeros_h = jnp.zeros((2 * bsz // SUBLANES, N_SLABS, SUBLANES, LANES), F32)
    ssm_p, hre_p, him_p = _ssm(u_p.reshape(bsz, seq, SSM_WIDTH), mats,
                               (_lam_groups(lb_re, bsz), _lam_groups(lb_im, bsz)),
                               d_row, glu_b, (zeros_h, zeros_h), bsz, 256)
    att_p = _attn_prompt(lam, q_p.reshape(bsz, seq, ATTN_WIDTH), kb_p,
                         vb_p.reshape(bsz, seq, ATTN_WIDTH), sg, attn_bq, out_scale)
    x1_p, cq_p = _out_proj(xp, ssm_p.reshape(n_p, SSM_WIDTH), att_p.reshape(n_p, ATTN_WIDTH),
                           w1, w2, row(p['ln2_g']), wq_b, row(p['ca_q_norm_g']), 512, BF16)
    mk, mv, mkb, mvb = _mem_kv(mem_prompt.reshape(bsz * N_MEM, D_MODEL), row(p['mem_norm_g']),
                               wk_b, wv_b, row(p['ca_k_norm_g']), 256)
    co_p = _cross_prompt(cq_p, mkb.reshape(bsz, N_MEM, D_MODEL), mvb.reshape(bsz, N_MEM, D_MODEL), 512)
    y_p, conv_p = _ffn(x1_p, co_p, wo_b, row(p['ln3_g']), wg_b, wvf_b, cw, cb, wd_b,
                       jnp.zeros((bsz, CONV_WIDTH - 1, FFN_HIDDEN), F32), 512, 1)

    xs = x_sample.reshape(n_s, D_MODEL)
    u_s, q_s, k_s, v_s = _in_proj(xs, row(p['ln1_g']), w_uqv, w_k, seg, qg, kg, n_s, F32)
    h0 = (_state_to_groups(state_ssm_re.reshape(n_dec, N_STATE), n_dec),
          _state_to_groups(state_ssm_im.reshape(n_dec, N_STATE), n_dec))
    ssm_s, hre_s, him_s = _ssm(u_s.reshape(1, n_s, SSM_WIDTH), mats,
                               (_lam_groups(lb_re, n_dec), _lam_groups(lb_im, n_dec)),
                               d_row, glu_b, h0, n_dec, t_new)
    n_phys = cache_k.shape[0]
    cache_kt = cache_k.transpose(0, 2, 3, 4, 1).reshape(n_phys, ATTN_WIDTH, PAGE_SIZE)
    att_s = _attn_sample(page_table, lam, q_s.reshape(n_dec, t_new, ATTN_WIDTH),
                         k_s.reshape(n_dec, t_new, ATTN_WIDTH), v_s.reshape(n_dec, t_new, ATTN_WIDTH),
                         sg, cache_kt, cache_v.reshape(n_phys, PAGE_SIZE * ATTN_HEADS, ATTN_V_DIM),
                         past_len, out_scale)
    x1_s, cq_s = _out_proj(xs, ssm_s.reshape(n_s, SSM_WIDTH), att_s.reshape(n_s, ATTN_WIDTH),
                           w1, w2, row(p['ln2_g']), wq_b, row(p['ca_q_norm_g']), n_s, F32)
    co_s = _cross_sample(cq_s.reshape(n_dec, t_new, D_MODEL),
                         _mem_rows(cache_mem_k), _mem_rows(cache_mem_v))
    tmaj = lambda a: a.reshape(n_dec, t_new, -1).transpose(1, 0, 2).reshape(n_s, -1)
    prev_s = state_conv.transpose(1, 0, 2).reshape(1, (CONV_WIDTH - 1) * n_dec, FFN_HIDDEN)
    y_s, conv_s = _ffn(tmaj(x1_s), tmaj(co_s), wo_b, row(p['ln3_g']), wg_b, wvf_b, cw, cb, wd_b,
                       prev_s, n_s, n_dec)
    y_s = y_s.reshape(t_new, n_dec, D_MODEL).transpose(1, 0, 2)
    conv_s = conv_s.reshape(CONV_WIDTH - 1, n_dec, FFN_HIDDEN).transpose(1, 0, 2)

    return (y_p.reshape(bsz, seq, D_MODEL), y_s,
            kt_p.reshape(bsz, ATTN_HEADS, 2, ATTN_HEAD_DIM, seq).transpose(0, 4, 1, 2, 3),
            v_p.reshape(bsz, seq, ATTN_HEADS, ATTN_V_DIM),
            k_s.reshape(n_dec, t_new, ATTN_HEADS, 2, ATTN_HEAD_DIM),
            v_s.reshape(n_dec, t_new, ATTN_HEADS, ATTN_V_DIM),
            _groups_to_state(hre_p, bsz), _groups_to_state(him_p, bsz),
            _groups_to_state(hre_s, n_dec), _groups_to_state(him_s, n_dec),
            conv_p, conv_s,
            mk.reshape(bsz, N_MEM, CA_HEADS, CA_HEAD_DIM), mv.reshape(bsz, N_MEM, CA_HEADS, CA_HEAD_DIM))


def kernel(x_prompt, x_sample, mem_prompt, cache_k, cache_v, page_table, state_ssm_re, state_ssm_im, state_conv, cache_mem_k, cache_mem_v, ln1_g, w_in, ssm_a_re, ssm_a_im, ssm_b_re, ssm_b_im, ssm_c_re, ssm_c_im, ssm_d, ssm_log_dt, ssm_glu_w, q_norm_g, k_norm_g, lam_q1, lam_k1, lam_q2, lam_k2, subln_g, w_out, ln2_g, mem_norm_g, ca_wq, ca_wk, ca_wv, ca_q_norm_g, ca_k_norm_g, ca_wo, ln3_g, ffn_wg, ffn_wv, ffn_conv_w, ffn_conv_b, ffn_wd):
    params = dict(
        ln1_g=ln1_g, w_in=w_in, ssm_a_re=ssm_a_re, ssm_a_im=ssm_a_im, ssm_b_re=ssm_b_re,
        ssm_b_im=ssm_b_im, ssm_c_re=ssm_c_re, ssm_c_im=ssm_c_im, ssm_d=ssm_d,
        ssm_log_dt=ssm_log_dt, ssm_glu_w=ssm_glu_w, q_norm_g=q_norm_g, k_norm_g=k_norm_g,
        lam_q1=lam_q1, lam_k1=lam_k1, lam_q2=lam_q2, lam_k2=lam_k2, subln_g=subln_g,
        w_out=w_out, ln2_g=ln2_g, mem_norm_g=mem_norm_g, ca_wq=ca_wq, ca_wk=ca_wk, ca_wv=ca_wv,
        ca_q_norm_g=ca_q_norm_g, ca_k_norm_g=ca_k_norm_g, ca_wo=ca_wo, ln3_g=ln3_g,
        ffn_wg=ffn_wg, ffn_wv=ffn_wv, ffn_conv_w=ffn_conv_w, ffn_conv_b=ffn_conv_b, ffn_wd=ffn_wd)
    y_p, y_s = x_prompt, x_sample
    per_layer = []
    for l in range(w_in.shape[0]):
        p = {k: v[l] for k, v in params.items()}
        outs = _layer(l, y_p, y_s, mem_prompt, cache_k[l], cache_v[l], page_table,
                      state_ssm_re[l], state_ssm_im[l], state_conv[l],
                      cache_mem_k[l], cache_mem_v[l], p)
        y_p, y_s = outs[0], outs[1]
        per_layer.append(outs[2:])
    return (y_p, y_s) + tuple(jnp.stack(leaf) for leaf in zip(*per_layer))
```

```python
import functools
import math

import jax
import jax.numpy as jnp
from jax import lax
from jax.experimental import pallas as pl
from jax.experimental.pallas import tpu as pltpu

F32 = jnp.float32
BF16 = jnp.bfloat16

D_MODEL = 1024
SSM_WIDTH = 512
SSM_GROUP_CH = 16
SSM_GROUPS = 32
SSM_STATE = 64
N_STATE = SSM_GROUPS * SSM_STATE
ATTN_WIDTH = 512
ATTN_HEAD_DIM = 64
ATTN_V_DIM = 128
ATTN_HEADS = 4
PAGE_SIZE = 128
N_MEM = 256
CA_HEADS = 4
CA_HEAD_DIM = 256
FFN_HIDDEN = 2816
CONV_WIDTH = 3
NORM_EPS = 1e-6

LANES = 128
SUBLANES = 8
VMEM_LIMIT = 56 * 1024 * 1024

NEG_BIG = -1e30
LOG2E = math.log2(math.e)
MAX_SHIFT_GAP = 100.0


def _cparams(n_axes):
    return pltpu.CompilerParams(
        dimension_semantics=("arbitrary",) * n_axes,
        vmem_limit_bytes=VMEM_LIMIT)


def _const_spec(shape):
    nd = len(shape)
    return pl.BlockSpec(shape, lambda *_: (0,) * nd, pipeline_mode=pl.Buffered(1))


def _rms(x, g):
    return x * lax.rsqrt(jnp.mean(x * x, axis=-1, keepdims=True) + NORM_EPS) * g


def _dot(a, b):
    return jnp.dot(a, b, preferred_element_type=F32)


def _dot_nt(a, b):
    return lax.dot_general(a, b, (((1,), (1,)), ((), ())), preferred_element_type=F32)


def _head_rms(x, g, width, scale):
    outs = []
    for h in range(x.shape[-1] // width):
        c = x[:, h * width:(h + 1) * width]
        outs.append(_rms(c, g) * scale)
    return jnp.concatenate(outs, axis=-1)


def _prep_kernel(are_ref, aim_ref, ldt_ref, bre_ref, bim_ref,
                 q1_ref, k1_ref, q2_ref, k2_ref,
                 lbre_ref, lbim_ref, bbre_ref, bbim_ref, lam_ref, *, lam0):
    a_re = are_ref[...]
    a_im = aim_ref[...]
    dt = jnp.exp(ldt_ref[...])
    mag = jnp.exp(a_re * dt)
    lb_re = mag * jnp.cos(a_im * dt)
    lb_im = mag * jnp.sin(a_im * dt)
    den = a_re * a_re + a_im * a_im
    n_re = lb_re - 1.0
    f_re = (n_re * a_re + lb_im * a_im) / den
    f_im = (lb_im * a_re - n_re * a_im) / den
    b_re = bre_ref[...]
    b_im = bim_ref[...]
    lbre_ref[...] = lb_re
    lbim_ref[...] = lb_im
    bbre_ref[...] = f_re * b_re - f_im * b_im
    bbim_ref[...] = f_re * b_im + f_im * b_re
    s1 = jnp.sum(q1_ref[...] * k1_ref[...], axis=-1, keepdims=True)
    s2 = jnp.sum(q2_ref[...] * k2_ref[...], axis=-1, keepdims=True)
    lam_ref[...] = jnp.exp(s1) - jnp.exp(s2) + lam0


def _prep(a_re, a_im, log_dt, b_re, b_im, q1, k1, q2, k2, lam0):
    are = a_re.reshape(1, N_STATE)
    aim = a_im.reshape(1, N_STATE)
    ldt = jnp.repeat(log_dt, SSM_STATE).reshape(1, N_STATE)
    bre = b_re.transpose(2, 0, 1).reshape(SSM_GROUP_CH, N_STATE)
    bim = b_im.transpose(2, 0, 1).reshape(SSM_GROUP_CH, N_STATE)
    vec = lambda v: v.reshape(1, ATTN_HEAD_DIM)
    row = jax.ShapeDtypeStruct((1, N_STATE), F32)
    mat = jax.ShapeDtypeStruct((SSM_GROUP_CH, N_STATE), F32)
    return pl.pallas_call(
        functools.partial(_prep_kernel, lam0=lam0),
        out_shape=(row, row, mat, mat, jax.ShapeDtypeStruct((1, 1), F32)),
        name="prep",
    )(are, aim, ldt, bre, bim, vec(q1), vec(k1), vec(q2), vec(k2))


def _in_proj_kernel(x_ref, g_ref, w_ref, wk_ref, seg_ref, qg_ref, kg_ref,
                    u_ref, q_ref, k_ref, v_ref, *bf_refs, key_major, blk):
    xn = _rms(x_ref[...], g_ref[...]).astype(BF16)
    proj = _dot(xn, w_ref[...])
    seg = seg_ref[...]

    def chunk_norm(z, g):
        ms = _dot((z * z).astype(BF16), seg) * (1.0 / ATTN_HEAD_DIM)
        return z * lax.rsqrt(ms + NORM_EPS) * g

    u_ref[...] = proj[:, :SSM_WIDTH]
    q = proj[:, SSM_WIDTH:SSM_WIDTH + ATTN_WIDTH]
    v = proj[:, SSM_WIDTH + ATTN_WIDTH:]
    q_ref[...] = (chunk_norm(q, qg_ref[...]) * (ATTN_HEAD_DIM ** -0.5 * LOG2E)).astype(q_ref.dtype)
    if key_major:
        kb_ref, vb_ref = bf_refs
        tm = x_ref.shape[0]
        z = _dot_nt(wk_ref[...], xn).reshape(ATTN_WIDTH // ATTN_HEAD_DIM, ATTN_HEAD_DIM, tm)
        ms = jnp.mean(z * z, axis=1, keepdims=True)
        kn = (z * lax.rsqrt(ms + NORM_EPS) * kg_ref[...]).reshape(ATTN_WIDTH, tm)
        k_ref[0] = kn
        for i in range(tm // blk):
            kb_ref[0, i] = kn[:, i * blk:(i + 1) * blk].astype(BF16)
        for h in range(ATTN_HEADS):
            v_ref[:, h, :] = v[:, ATTN_V_DIM * h:ATTN_V_DIM * (h + 1)]
        vb_ref[...] = v.astype(BF16)
    else:
        k_ref[...] = chunk_norm(_dot(xn, wk_ref[...]), kg_ref[...])
        v_ref[...] = v


def _in_proj(x, ln1_g, w_uqv, wk, seg, qg, kg, tm, q_dtype, n_batch=None, blk=None):
    n = x.shape[0]
    key_major = n_batch is not None
    bsz = n_batch if key_major else 1
    per_b = n // bsz // tm
    tok = lambda width: pl.BlockSpec((tm, width), lambda b, i: (b * per_b + i, 0))
    out_specs = [tok(SSM_WIDTH), tok(ATTN_WIDTH)]
    out_shape = [jax.ShapeDtypeStruct((n, SSM_WIDTH), F32),
                 jax.ShapeDtypeStruct((n, ATTN_WIDTH), q_dtype)]
    if key_major:
        t = n // bsz
        out_specs += [pl.BlockSpec((1, ATTN_WIDTH, tm), lambda b, i: (b, 0, i)),
                      pl.BlockSpec((tm, ATTN_HEADS, ATTN_V_DIM), lambda b, i: (b * per_b + i, 0, 0)),
                      pl.BlockSpec((1, tm // blk, ATTN_WIDTH, blk), lambda b, i: (b, i, 0, 0)),
                      tok(ATTN_WIDTH)]
        out_shape += [jax.ShapeDtypeStruct((bsz, ATTN_WIDTH, t), F32),
                      jax.ShapeDtypeStruct((n, ATTN_HEADS, ATTN_V_DIM), F32),
                      jax.ShapeDtypeStruct((bsz, t // blk, ATTN_WIDTH, blk), BF16),
                      jax.ShapeDtypeStruct((n, ATTN_WIDTH), BF16)]
    else:
        out_specs += [tok(ATTN_WIDTH), tok(ATTN_WIDTH)]
        out_shape += [jax.ShapeDtypeStruct((n, ATTN_WIDTH), F32)] * 2
    return pl.pallas_call(
        functools.partial(_in_proj_kernel, key_major=key_major, blk=blk),
        grid=(bsz, per_b),
        in_specs=[tok(D_MODEL), _const_spec((1, D_MODEL)), _const_spec(w_uqv.shape),
                  _const_spec(wk.shape), _const_spec((ATTN_WIDTH, ATTN_WIDTH)),
                  _const_spec((1, ATTN_WIDTH)), _const_spec(kg.shape)],
        out_specs=out_specs,
        out_shape=out_shape,
        compiler_params=_cparams(2),
        name="in_proj",
    )(x, ln1_g, w_uqv, wk, seg, qg, kg)


N_SLABS = N_STATE // 2 // LANES
HALF = N_STATE // 2


def _gelu_tanh(x):
    c = math.sqrt(2.0 / math.pi)
    return 0.5 * x * (1.0 + jnp.tanh(c * (x + 0.044715 * (x * x * x))))


def _ssm_kernel(u_ref, bre_ref, bim_ref, cre_ref, cim_ref, lre_ref, lim_ref,
                d_ref, glu_ref, h0re_ref, h0im_ref,
                out_ref, hre_ref, him_ref, sre, sim,
                *, n_seq, tc, pitch, n_bulk, rows_bulk, pitch_bulk):
    c = pl.program_id(0)
    rows = n_seq * tc
    n_groups = 2 * n_seq // SUBLANES

    @pl.when(c == 0)
    def _():
        hre_ref[...] = h0re_ref[...]
        him_ref[...] = h0im_ref[...]

    u = u_ref[...].reshape(rows, SSM_WIDTH)
    ub = u.astype(BF16)

    for j in range(4):
        hh, jj = divmod(j, 2)
        uj = ub[:, LANES * j:LANES * (j + 1)]
        for src, dst in ((bre_ref, sre), (bim_ref, sim)):
            bu = _dot(uj, src[j])
            for kk in range(4):
                slab = 4 * jj + kk
                for bb in range(n_bulk):
                    r0 = (hh * n_bulk + bb) * pitch_bulk
                    dst[slab, r0:r0 + rows_bulk, :] = (
                        bu[bb * rows_bulk:(bb + 1) * rows_bulk, LANES * kk:LANES * (kk + 1)])

    def group_body(g, carry):
        base = g * (SUBLANES * pitch)
        lr = [lre_ref[g, k] for k in range(N_SLABS)]
        li = [lim_ref[g, k] for k in range(N_SLABS)]
        hr0 = tuple(hre_ref[g, k] for k in range(N_SLABS))
        hi0 = tuple(him_ref[g, k] for k in range(N_SLABS))

        def step(t, hc):
            hr, hi = hc
            idx = pl.ds(base + t, SUBLANES, stride=pitch)
            nr, ni = [], []
            for k in range(N_SLABS):
                br = sre[k, idx, :]
                bi = sim[k, idx, :]
                r = lr[k] * hr[k] - li[k] * hi[k] + br
                i = lr[k] * hi[k] + li[k] * hr[k] + bi
                sre[k, idx, :] = r
                sim[k, idx, :] = i
                nr.append(r)
                ni.append(i)
            return tuple(nr), tuple(ni)

        hr, hi = lax.fori_loop(0, tc, step, (hr0, hi0), unroll=min(tc, 4))
        for k in range(N_SLABS):
            hre_ref[g, k] = hr[k]
            him_ref[g, k] = hi[k]
        return carry

    lax.fori_loop(0, n_groups, group_body, 0)

    y_rows = []
    for bb in range(n_bulk):
        y_cols = []
        for j in range(4):
            hh, jj = divmod(j, 2)
            r0 = (hh * n_bulk + bb) * pitch_bulk
            hr = jnp.concatenate(
                [sre[4 * jj + kk, r0:r0 + rows_bulk, :] for kk in range(4)], axis=1).astype(BF16)
            hi = jnp.concatenate(
                [sim[4 * jj + kk, r0:r0 + rows_bulk, :] for kk in range(4)], axis=1).astype(BF16)
            y_cols.append(_dot(hr, cre_ref[j]) - _dot(hi, cim_ref[j]))
        y_rows.append(jnp.concatenate(y_cols, axis=1))
    y = jnp.concatenate(y_rows, axis=0) if n_bulk > 1 else y_rows[0]
    y = y + d_ref[...] * u
    gl = _gelu_tanh(y)
    gate = jax.nn.sigmoid(_dot(gl.astype(BF16), glu_ref[...]))
    out_ref[...] = (gl * gate).astype(out_ref.dtype).reshape(out_ref.shape)


def _ssm(u3, mats, lam_g, d_row, glu_b, h0, n_seq, tc):
    bre, bim, cre, cim = mats
    lre_g, lim_g = lam_g
    h0re, h0im = h0
    n_groups = 2 * n_seq // SUBLANES
    t_total = u3.shape[0] * u3.shape[1] // n_seq
    n_chunks = t_total // tc
    if tc % SUBLANES == 0:
        pitch = tc + SUBLANES
        n_bulk, rows_bulk, pitch_bulk = n_seq, tc, pitch
        ublock = (n_seq, tc, SSM_WIDTH)
    else:
        assert n_chunks == 1
        pitch = tc
        n_bulk, rows_bulk, pitch_bulk = 1, n_seq * tc, n_seq * tc
        ublock = (1, n_seq * tc, SSM_WIDTH)
    scr_rows = 2 * n_bulk * pitch_bulk
    hshape = (n_groups, N_SLABS, SUBLANES, LANES)
    kern = functools.partial(_ssm_kernel, n_seq=n_seq, tc=tc, pitch=pitch,
                             n_bulk=n_bulk, rows_bulk=rows_bulk, pitch_bulk=pitch_bulk)
    return pl.pallas_call(
        kern,
        grid=(n_chunks,),
        in_specs=[pl.BlockSpec(ublock, lambda c: (0, c, 0)),
                  _const_spec(bre.shape), _const_spec(bim.shape),
                  _const_spec(cre.shape), _const_spec(cim.shape),
                  _const_spec(hshape), _const_spec(hshape),
                  _const_spec((1, SSM_WIDTH)), _const_spec((SSM_WIDTH, SSM_WIDTH)),
                  _const_spec(hshape), _const_spec(hshape)],
        out_specs=[pl.BlockSpec(ublock, lambda c: (0, c, 0)),
                   _const_spec(hshape), _const_spec(hshape)],
        out_shape=[jax.ShapeDtypeStruct(u3.shape, BF16),
                   jax.ShapeDtypeStruct(hshape, F32),
                   jax.ShapeDtypeStruct(hshape, F32)],
        scratch_shapes=[pltpu.VMEM((N_SLABS, scr_rows, LANES), F32),
                        pltpu.VMEM((N_SLABS, scr_rows, LANES), F32)],
        compiler_params=_cparams(1),
        name="ssm",
    )(u3, bre, bim, cre, cim, lre_g, lim_g, d_row, glu_b, h0re, h0im)


def _state_to_groups(h, n_seq):
    x = h.reshape(n_seq, 2, N_SLABS, LANES).transpose(1, 0, 2, 3)
    x = x.reshape(2 * n_seq // SUBLANES, SUBLANES, N_SLABS, LANES)
    return x.transpose(0, 2, 1, 3)


def _groups_to_state(x, n_seq):
    x = x.transpose(0, 2, 1, 3).reshape(2, n_seq, N_SLABS, LANES)
    return x.transpose(1, 0, 2, 3).reshape(n_seq, SSM_GROUPS, SSM_STATE)


def _head_slope(h):
    return jnp.where(h == 0, 2.0 ** -2, jnp.where(h == 1, 2.0 ** -4,
                     jnp.where(h == 2, 2.0 ** -6, 2.0 ** -8))).astype(F32)


def _attn_prompt_kernel(lam_ref, q_ref, k_ref, v_ref, sg_ref, o_ref,
                        q_scr, s_scr, m_scr, l_scr, acc_scr, kn_scr, *, bq, bk, out_scale):
    h = pl.program_id(1)
    qi = pl.program_id(2)
    slope = _head_slope(h) * LOG2E
    lam = lam_ref[0, 0]
    n_maps = 2

    q = q_ref[0]
    lane = lax.broadcasted_iota(jnp.int32, q.shape, 1)
    zero = jnp.zeros_like(q)
    q_scr[0] = jnp.where(lane < ATTN_HEAD_DIM, q, zero)
    q_scr[1] = jnp.where(lane >= ATTN_HEAD_DIM, q, zero)

    col = lax.broadcasted_iota(jnp.int32, (1, bk), 1)
    q0 = qi * bq

    def lane_tiles(x):
        return [x[:, LANES * i:LANES * (i + 1)] for i in range(bk // LANES)]

    n_full = q0 // bk

    def run_pairs(first, count, step):
        odd = lax.rem(count, 2)

        @pl.when(odd == 1)
        def _():
            step(first)

        def body(t, carry):
            j = first + odd + 2 * t
            step(j)
            step(j + 1)
            return carry

        lax.fori_loop(0, count // 2, body, 0)

    def scores(j, i, masked):
        kpos = col + j * bk
        s = _dot(q_scr[i], k_ref[0, j]) + slope * (kpos - q0).astype(F32)
        if masked:
            qpos = lax.broadcasted_iota(jnp.int32, (bq, bk), 0) + q0
            s = jnp.where(qpos >= kpos, s, -jnp.inf)
        return s

    def accumulate(j, i, p):
        start = pl.multiple_of(j * bk, bk)
        lpart = l_scr[i]
        for piece in lane_tiles(p):
            lpart = lpart + piece
        l_scr[i] = lpart
        acc_scr[i] += _dot(p.astype(BF16), v_ref[0, pl.ds(start, bk), :])

    @pl.when(qi == 0)
    def _():
        best = [jnp.zeros((1, bk), F32)] * n_maps
        for j in range(k_ref.shape[1]):
            kf = k_ref[0, j].astype(F32)
            sq = kf * kf
            for i in range(n_maps):
                norm2 = jnp.sum(sq[ATTN_HEAD_DIM * i:ATTN_HEAD_DIM * (i + 1)], axis=0, keepdims=True)
                best[i] = jnp.maximum(best[i], norm2)
        for i in range(n_maps):
            kn_scr[i] = jnp.broadcast_to(jnp.max(best[i], axis=-1, keepdims=True), kn_scr.shape[1:])

    row = lax.broadcasted_iota(jnp.int32, (bq, 1), 0).astype(F32)
    shifts = []
    gap = None
    for i in range(n_maps):
        qf = q_scr[i].astype(F32)
        reach = jnp.sqrt(jnp.sum(qf * qf, axis=-1, keepdims=True) * kn_scr[i][0:1, 0:1])
        shifts.append(reach + slope * row)
        gap = jnp.max(reach) if gap is None else jnp.maximum(gap, jnp.max(reach))
    one_pass = 2.0 * gap <= MAX_SHIFT_GAP

    l_scr[...] = jnp.zeros(l_scr.shape, F32)
    acc_scr[...] = jnp.zeros(acc_scr.shape, F32)

    @pl.when(one_pass)
    def _():
        def step(j, masked=False):
            for i in range(n_maps):
                accumulate(j, i, jnp.exp2(scores(j, i, masked) - shifts[i]))

        step(n_full, masked=True)
        run_pairs(0, n_full, step)

    @pl.when(jnp.logical_not(one_pass))
    def _():
        def score_step(j, masked=False):
            for i in range(n_maps):
                s = scores(j, i, masked)
                s_scr[j, i] = s
                mpart = m_scr[i]
                for piece in lane_tiles(s):
                    mpart = jnp.maximum(mpart, piece)
                m_scr[i] = mpart

        m_scr[...] = jnp.full(m_scr.shape, -jnp.inf, F32)
        score_step(n_full, masked=True)
        run_pairs(0, n_full, score_step)
        ms = [jnp.max(m_scr[i], axis=-1, keepdims=True) for i in range(n_maps)]

        def sum_step(j):
            for i in range(n_maps):
                accumulate(j, i, jnp.exp2(s_scr[j, i] - ms[i]))

        run_pairs(0, n_full + 1, sum_step)

    outs = [acc_scr[i] / jnp.sum(l_scr[i], axis=-1, keepdims=True) for i in range(n_maps)]
    o = outs[0] - lam * outs[1]
    o_ref[0] = (_rms(o, sg_ref[...]) * out_scale).astype(o_ref.dtype)


def _attn_prompt(lam, qb, kb, vb, sg, bq, out_scale):
    bsz, t, _ = qb.shape
    bk = kb.shape[-1]
    return pl.pallas_call(
        functools.partial(_attn_prompt_kernel, bq=bq, bk=bk, out_scale=out_scale),
        grid=(bsz, ATTN_HEADS, t // bq),
        in_specs=[pl.BlockSpec(memory_space=pltpu.SMEM),
                  pl.BlockSpec((1, bq, ATTN_V_DIM), lambda b, h, i: (b, i, h)),
                  pl.BlockSpec((1, t // bk, ATTN_V_DIM, bk), lambda b, h, i: (b, 0, h, 0)),
                  pl.BlockSpec((1, t, ATTN_V_DIM), lambda b, h, i: (b, 0, h)),
                  _const_spec((1, ATTN_V_DIM))],
        out_specs=pl.BlockSpec((1, bq, ATTN_V_DIM), lambda b, h, i: (b, i, h)),
        out_shape=jax.ShapeDtypeStruct(qb.shape, BF16),
        scratch_shapes=[pltpu.VMEM((2, bq, ATTN_V_DIM), BF16),
                        pltpu.VMEM((t // bk, 2, bq, bk), F32),
                        pltpu.VMEM((2, bq, LANES), F32),
                        pltpu.VMEM((2, bq, LANES), F32),
                        pltpu.VMEM((2, bq, ATTN_V_DIM), F32),
                        pltpu.VMEM((2, SUBLANES, LANES), F32)],
        compiler_params=_cparams(3),
        name="attn_prompt",
    )(lam, qb, kb, vb, sg)


def _attn_sample_kernel(pt_ref, lam_ref, q_ref, kn_ref, vn_ref, sg_ref, *refs,
                        n_pages, t_new, past_len, out_scale):
    k_refs = refs[:n_pages]
    v_refs = refs[n_pages:2 * n_pages]
    o_ref = refs[2 * n_pages]
    del pt_ref
    lam = lam_ref[0, 0]
    n_rows = 2 * ATTN_HEADS * t_new

    q4 = q_ref[0]
    qe = jnp.concatenate([q4] * (2 * ATTN_HEADS), axis=0)
    row = lax.broadcasted_iota(jnp.int32, qe.shape, 0)
    lane = lax.broadcasted_iota(jnp.int32, qe.shape, 1)
    qe = jnp.where(lane // ATTN_HEAD_DIM == row // t_new, qe, 0.0)
    qeb = qe.astype(BF16)

    rcol = lax.broadcasted_iota(jnp.int32, (n_rows, 1), 0)
    head_c = rcol // (2 * t_new)
    tok_c = rcol % t_new
    slope = _head_slope(head_c) * LOG2E
    rows_h = 2 * t_new

    kt = jnp.concatenate([k_refs[p][0].astype(BF16) for p in range(n_pages)], axis=1)
    kpos = lax.broadcasted_iota(jnp.int32, (1, n_pages * PAGE_SIZE), 1) - past_len
    s = _dot(qeb, kt) + slope * kpos.astype(F32)

    kn = kn_ref[0]
    vn = vn_ref[0]
    s_new = []
    for c in range(t_new):
        sc = jnp.sum(qe * kn[c:c + 1, :], axis=-1, keepdims=True) + slope * float(c)
        s_new.append(jnp.where(tok_c >= c, sc, -jnp.inf))

    m = jnp.max(s, axis=-1, keepdims=True)
    for sc in s_new:
        m = jnp.maximum(m, sc)
    pe = jnp.exp2(s - m)
    l = jnp.sum(pe, axis=-1, keepdims=True)

    pv = []
    for h in range(ATTN_HEADS):
        vh = jnp.concatenate(
            [v_refs[p][0, pl.ds(h, PAGE_SIZE, stride=ATTN_HEADS), :].astype(BF16)
             for p in range(n_pages)], axis=0)
        pv.append(_dot(pe[rows_h * h:rows_h * (h + 1)].astype(BF16), vh))
    acc = jnp.concatenate(pv, axis=0)

    for c in range(t_new):
        pc = jnp.exp2(s_new[c] - m)
        l = l + pc
        vc = jnp.concatenate(
            [jnp.broadcast_to(vn[c:c + 1, ATTN_V_DIM * h:ATTN_V_DIM * (h + 1)], (rows_h, ATTN_V_DIM))
             for h in range(ATTN_HEADS)], axis=0)
        acc = acc + pc * vc

    acc = acc / l
    outs = []
    for h in range(ATTN_HEADS):
        a0 = acc[rows_h * h:rows_h * h + t_new]
        a1 = acc[rows_h * h + t_new:rows_h * (h + 1)]
        outs.append(_rms(a0 - lam * a1, sg_ref[...]) * out_scale)
    o_ref[0] = jnp.concatenate(outs, axis=-1)


def _attn_sample(page_table, lam, q3, kn3, vn3, sg, cache_kt, cache_v3, past_len, out_scale):
    n_dec, t_new, _ = q3.shape
    n_pages = page_table.shape[1]
    pt = page_table.reshape(-1)
    tok_spec = pl.BlockSpec((1, t_new, ATTN_WIDTH), lambda b, pt: (b, 0, 0))

    def k_spec(p):
        return pl.BlockSpec((1, ATTN_WIDTH, PAGE_SIZE),
                            lambda b, pt, p=p: (pt[b * n_pages + p], 0, 0))

    def v_spec(p):
        return pl.BlockSpec((1, PAGE_SIZE * ATTN_HEADS, ATTN_V_DIM),
                            lambda b, pt, p=p: (pt[b * n_pages + p], 0, 0))

    grid_spec = pltpu.PrefetchScalarGridSpec(
        num_scalar_prefetch=1,
        grid=(n_dec,),
        in_specs=[pl.BlockSpec(memory_space=pltpu.SMEM), tok_spec, tok_spec, tok_spec,
                  pl.BlockSpec((1, ATTN_V_DIM), lambda b, pt: (0, 0))]
                 + [k_spec(p) for p in range(n_pages)] + [v_spec(p) for p in range(n_pages)],
        out_specs=tok_spec)
    kern = functools.partial(_attn_sample_kernel, n_pages=n_pages, t_new=t_new,
                             past_len=past_len, out_scale=out_scale)
    return pl.pallas_call(
        kern,
        grid_spec=grid_spec,
        out_shape=jax.ShapeDtypeStruct(q3.shape, F32),
        compiler_params=_cparams(1),
        name="attn_sample",
    )(pt, lam, q3, kn3, vn3, sg, *([cache_kt] * n_pages), *([cache_v3] * n_pages))


def _out_proj_kernel(x_ref, s_ref, o_ref, w1_ref, w2_ref, g2_ref, wq_ref, qg_ref,
                     x1_ref, cq_ref):
    x1 = (x_ref[...] + _dot(s_ref[...].astype(BF16), w1_ref[...])
          + _dot(o_ref[...].astype(BF16), w2_ref[...]))
    x1_ref[...] = x1
    xn = _rms(x1, g2_ref[...]).astype(BF16)
    cq = _dot(xn, wq_ref[...])
    cq_ref[...] = _head_rms(cq, qg_ref[...], CA_HEAD_DIM, CA_HEAD_DIM ** -0.5).astype(cq_ref.dtype)


def _out_proj(x, ssm_o, att_o, w1, w2, g2, wq, qg, tm, cq_dtype):
    n = x.shape[0]
    tok = lambda width: pl.BlockSpec((tm, width), lambda i: (i, 0))
    return pl.pallas_call(
        _out_proj_kernel,
        grid=(n // tm,),
        in_specs=[tok(D_MODEL), tok(SSM_WIDTH), tok(ATTN_WIDTH),
                  _const_spec((SSM_WIDTH, D_MODEL)), _const_spec((ATTN_WIDTH, D_MODEL)),
                  _const_spec((1, D_MODEL)), _const_spec((D_MODEL, D_MODEL)),
                  _const_spec((1, CA_HEAD_DIM))],
        out_specs=[tok(D_MODEL), tok(D_MODEL)],
        out_shape=[jax.ShapeDtypeStruct((n, D_MODEL), F32),
                   jax.ShapeDtypeStruct((n, D_MODEL), cq_dtype)],
        compiler_params=_cparams(1),
        name="out_proj",
    )(x, ssm_o, att_o, w1, w2, g2, wq, qg)


def _mem_kv_kernel(m_ref, g_ref, wk_ref, wv_ref, kg_ref, mk_ref, mv_ref, mkb_ref, mvb_ref):
    mn = _rms(m_ref[...], g_ref[...]).astype(BF16)
    mk = _head_rms(_dot(mn, wk_ref[...]), kg_ref[...], CA_HEAD_DIM, 1.0)
    mv = _dot(mn, wv_ref[...])
    mk_ref[...] = mk
    mv_ref[...] = mv
    mkb_ref[...] = mk.astype(BF16)
    mvb_ref[...] = mv.astype(BF16)


def _mem_kv(mem, g, wk, wv, kg, tm):
    n = mem.shape[0]
    tok = pl.BlockSpec((tm, D_MODEL), lambda i: (i, 0))
    wspec = _const_spec((D_MODEL, D_MODEL))
    return pl.pallas_call(
        _mem_kv_kernel,
        grid=(n // tm,),
        in_specs=[tok, _const_spec((1, D_MODEL)), wspec, wspec, _const_spec((1, CA_HEAD_DIM))],
        out_specs=[tok, tok, tok, tok],
        out_shape=[jax.ShapeDtypeStruct((n, D_MODEL), F32)] * 2
                  + [jax.ShapeDtypeStruct((n, D_MODEL), BF16)] * 2,
        compiler_params=_cparams(1),
        name="mem_kv",
    )(mem, g, wk, wv, kg)


def _cross_prompt_kernel(cq_ref, mk_ref, mv_ref, co_ref):
    cq = cq_ref[...]
    outs = []
    for h in range(CA_HEADS):
        sl = slice(CA_HEAD_DIM * h, CA_HEAD_DIM * (h + 1))
        s = _dot_nt(cq[:, sl], mk_ref[0, :, sl])
        p = jnp.exp(s - jnp.max(s, axis=-1, keepdims=True))
        l = jnp.sum(p, axis=-1, keepdims=True)
        outs.append(_dot((p / l).astype(BF16), mv_ref[0, :, sl]))
    co_ref[...] = jnp.concatenate(outs, axis=-1).astype(co_ref.dtype)


def _cross_prompt(cq, mkb, mvb, tm):
    n = cq.shape[0]
    bsz = mkb.shape[0]
    per_b = n // bsz // tm
    tok = pl.BlockSpec((tm, D_MODEL), lambda b, i: (b * per_b + i, 0))
    mem = pl.BlockSpec((1, N_MEM, D_MODEL), lambda b, i: (b, 0, 0))
    return pl.pallas_call(
        _cross_prompt_kernel,
        grid=(bsz, per_b),
        in_specs=[tok, mem, mem],
        out_specs=tok,
        out_shape=jax.ShapeDtypeStruct((n, D_MODEL), BF16),
        compiler_params=_cparams(2),
        name="cross_prompt",
    )(cq, mkb, mvb)


def _cross_sample_kernel(cq_ref, mk_ref, mv_ref, co_ref, *, t_new):
    halves = CA_HEAD_DIM // LANES
    per_key = halves * CA_HEADS

    def heads(ref, b):
        return jnp.concatenate(
            [ref[b, pl.ds(half * CA_HEADS + h, N_MEM, stride=per_key), :]
             for h in range(CA_HEADS) for half in range(halves)], axis=1).astype(BF16)

    for b in range(cq_ref.shape[0]):
        qe = jnp.concatenate([cq_ref[b]] * CA_HEADS, axis=0)
        row = lax.broadcasted_iota(jnp.int32, qe.shape, 0)
        lane = lax.broadcasted_iota(jnp.int32, qe.shape, 1)
        qe = jnp.where(lane // CA_HEAD_DIM == row // t_new, qe, 0.0).astype(BF16)
        s = _dot_nt(qe, heads(mk_ref, b))
        p = jnp.exp(s - jnp.max(s, axis=-1, keepdims=True))
        l = jnp.sum(p, axis=-1, keepdims=True)
        full = _dot((p / l).astype(BF16), heads(mv_ref, b))
        outs = [full[h * t_new:(h + 1) * t_new, CA_HEAD_DIM * h:CA_HEAD_DIM * (h + 1)]
                for h in range(CA_HEADS)]
        co_ref[b] = jnp.concatenate(outs, axis=-1)


def _mem_rows(mem):
    n = mem.shape[0]
    halves = CA_HEAD_DIM // LANES
    return (mem.reshape(n, N_MEM, CA_HEADS, halves, LANES).transpose(0, 1, 3, 2, 4)
            .reshape(n, N_MEM * halves * CA_HEADS, LANES))


def _cross_sample(cq3, mem_k, mem_v):
    n_dec, t_new, _ = cq3.shape
    per_step = 4
    tok = pl.BlockSpec((per_step, t_new, D_MODEL), lambda b: (b, 0, 0))
    mem = pl.BlockSpec((per_step,) + mem_k.shape[1:], lambda b: (b, 0, 0))
    return pl.pallas_call(
        functools.partial(_cross_sample_kernel, t_new=t_new),
        grid=(n_dec // per_step,),
        in_specs=[tok, mem, mem],
        out_specs=tok,
        out_shape=jax.ShapeDtypeStruct(cq3.shape, F32),
        compiler_params=_cparams(1),
        name="cross_sample",
    )(cq3, mem_k, mem_v)


def _ffn_kernel(x_ref, co_ref, wo_ref, g3_ref, wg_ref, wv_ref, cw_ref, cb_ref, wd_ref, prev_ref,
                y_ref, cs_ref, hbuf, *, tm, shift, pad):
    i = pl.program_id(1)

    @pl.when(i == 0)
    def _():
        hbuf[pad - 2 * shift:pad, :] = prev_ref[0]

    x2 = x_ref[...] + _dot(co_ref[...].astype(BF16), wo_ref[...])
    xn = _rms(x2, g3_ref[...]).astype(BF16)
    hbuf[pad:pad + tm, :] = _dot(xn, wg_ref[...])
    hv = _dot(xn, wv_ref[...])
    cw = cw_ref[...]
    conv = (cb_ref[...] + cw[0:1, :] * hbuf[pad - 2 * shift:pad - 2 * shift + tm, :]
            + cw[1:2, :] * hbuf[pad - shift:pad - shift + tm, :]
            + cw[2:3, :] * hbuf[pad:pad + tm, :])
    act = (conv * jax.nn.sigmoid(conv) * hv).astype(BF16)
    y_ref[...] = x2 + _dot(act, wd_ref[...])
    last = hbuf[pad + tm - 2 * shift:pad + tm, :]
    cs_ref[0] = last
    hbuf[pad - 2 * shift:pad, :] = last


def _ffn(x1, co, wo, g3, wg, wv, cw, cb, wd, prev, tm, shift):
    n = x1.shape[0]
    n_seq_blocks = prev.shape[0]
    per_b = n // n_seq_blocks // tm
    pad = -(-2 * shift // SUBLANES) * SUBLANES
    tok = pl.BlockSpec((tm, D_MODEL), lambda b, i: (b * per_b + i, 0))
    state = pl.BlockSpec((1, 2 * shift, FFN_HIDDEN), lambda b, i: (b, 0, 0))
    return pl.pallas_call(
        functools.partial(_ffn_kernel, tm=tm, shift=shift, pad=pad),
        grid=(n_seq_blocks, per_b),
        in_specs=[tok, tok, _const_spec((D_MODEL, D_MODEL)), _const_spec((1, D_MODEL)),
                  _const_spec((D_MODEL, FFN_HIDDEN)), _const_spec((D_MODEL, FFN_HIDDEN)),
                  _const_spec((CONV_WIDTH, FFN_HIDDEN)), _const_spec((1, FFN_HIDDEN)),
                  _const_spec((FFN_HIDDEN, D_MODEL)), state],
        out_specs=[tok, state],
        out_shape=[jax.ShapeDtypeStruct((n, D_MODEL), F32),
                   jax.ShapeDtypeStruct(prev.shape, F32)],
        scratch_shapes=[pltpu.VMEM((pad + tm, FFN_HIDDEN), F32)],
        compiler_params=_cparams(2),
        name="ffn",
    )(x1, co, wo, g3, wg, wv, cw, cb, wd, prev)


def _block_diag(blocks, n_outer):
    n, r, c = blocks.shape
    per = n // n_outer
    b = blocks.reshape(n_outer, per, r, c)
    eye = jnp.eye(per, dtype=blocks.dtype)
    full = b[:, :, :, None, :] * eye[None, :, None, :, None]
    return full.reshape(n_outer, per * r, per * c)


def _lam_groups(lb, n_seq):
    halves = lb.reshape(2, 1, N_SLABS, LANES)
    rows = jnp.broadcast_to(halves, (2, n_seq, N_SLABS, LANES))
    rows = rows.reshape(2 * n_seq // SUBLANES, SUBLANES, N_SLABS, LANES)
    return rows.transpose(0, 2, 1, 3)


def _layer(l, x_prompt, x_sample, mem_prompt, cache_k, cache_v, page_table,
           state_ssm_re, state_ssm_im, state_conv, cache_mem_k, cache_mem_v, p):
    bsz, seq, _ = x_prompt.shape
    n_dec, t_new, _ = x_sample.shape
    n_p = bsz * seq
    n_s = n_dec * t_new
    past_len = page_table.shape[1] * PAGE_SIZE
    lam0 = 0.8 - 0.6 * math.exp(-0.3 * l)
    out_scale = 1.0 - lam0
    row = lambda v: v.reshape(1, -1).astype(F32)
    bf = lambda w: w.astype(BF16)

    lb_re, lb_im, bb_re, bb_im, lam = _prep(
        p['ssm_a_re'], p['ssm_a_im'], p['ssm_log_dt'], p['ssm_b_re'], p['ssm_b_im'],
        p['lam_q1'], p['lam_k1'], p['lam_q2'], p['lam_k2'], lam0)

    def b_mats(bbt):
        blocks = bbt.reshape(SSM_GROUP_CH, SSM_GROUPS, SSM_STATE).transpose(1, 0, 2)
        return bf(_block_diag(blocks, 4))

    def c_mats(cm):
        return bf(_block_diag(cm.transpose(0, 2, 1), 4))

    mats = (b_mats(bb_re), b_mats(bb_im), c_mats(p['ssm_c_re']), c_mats(p['ssm_c_im']))
    d_row = row(p['ssm_d'])
    glu_b = bf(p['ssm_glu_w'])

    w_in = p['w_in']
    k0, k1 = SSM_WIDTH + ATTN_WIDTH, SSM_WIDTH + 2 * ATTN_WIDTH
    w_uqv = bf(jnp.concatenate([w_in[:, :k0], w_in[:, k1:]], axis=1))
    w_k = bf(w_in[:, k0:k1])
    seg = jnp.kron(jnp.eye(ATTN_WIDTH // ATTN_HEAD_DIM, dtype=F32),
                   jnp.ones((ATTN_HEAD_DIM, ATTN_HEAD_DIM), F32)).astype(BF16)
    qg = jnp.tile(p['q_norm_g'], ATTN_WIDTH // ATTN_HEAD_DIM).reshape(1, ATTN_WIDTH)
    kg = jnp.tile(p['k_norm_g'], ATTN_WIDTH // ATTN_HEAD_DIM).reshape(1, ATTN_WIDTH)
    sg = row(p['subln_g'])
    w_out_b = bf(p['w_out'])
    w1, w2 = w_out_b[:SSM_WIDTH], w_out_b[SSM_WIDTH:]
    wq_b, wk_b, wv_b, wo_b = bf(p['ca_wq']), bf(p['ca_wk']), bf(p['ca_wv']), bf(p['ca_wo'])
    wg_b, wvf_b, wd_b = bf(p['ffn_wg']), bf(p['ffn_wv']), bf(p['ffn_wd'])
    cw = p['ffn_conv_w'].astype(F32)
    cb = row(p['ffn_conv_b'])

    xp = x_prompt.reshape(n_p, D_MODEL)
    attn_bq, attn_bk = 512, 512
    kg_col = jnp.broadcast_to(p['k_norm_g'].reshape(ATTN_HEAD_DIM, 1), (ATTN_HEAD_DIM, 512))
    u_p, q_p, kt_p, v_p, kb_p, vb_p = _in_proj(xp, row(p['ln1_g']), w_uqv, w_k.T, seg, qg, kg_col,
                                               512, BF16, n_batch=bsz, blk=attn_bk)
    zeros_h = jnp.zeros((2 * bsz // SUBLANES, N_SLABS, SUBLANES, LANES), F32)
    ssm_p, hre_p, him_p = _ssm(u_p.reshape(bsz, seq, SSM_WIDTH), mats,
                               (_lam_groups(lb_re, bsz), _lam_groups(lb_im, bsz)),
                               d_row, glu_b, (zeros_h, zeros_h), bsz, 256)
    att_p = _attn_prompt(lam, q_p.reshape(bsz, seq, ATTN_WIDTH), kb_p,
                         vb_p.reshape(bsz, seq, ATTN_WIDTH), sg, attn_bq, out_scale)
    x1_p, cq_p = _out_proj(xp, ssm_p.reshape(n_p, SSM_WIDTH), att_p.reshape(n_p, ATTN_WIDTH),
                           w1, w2, row(p['ln2_g']), wq_b, row(p['ca_q_norm_g']), 512, BF16)
    mk, mv, mkb, mvb = _mem_kv(mem_prompt.reshape(bsz * N_MEM, D_MODEL), row(p['mem_norm_g']),
                               wk_b, wv_b, row(p['ca_k_norm_g']), 256)
    co_p = _cross_prompt(cq_p, mkb.reshape(bsz, N_MEM, D_MODEL), mvb.reshape(bsz, N_MEM, D_MODEL), 512)
    y_p, conv_p = _ffn(x1_p, co_p, wo_b, row(p['ln3_g']), wg_b, wvf_b, cw, cb, wd_b,
                       jnp.zeros((bsz, CONV_WIDTH - 1, FFN_HIDDEN), F32), 512, 1)

    xs = x_sample.reshape(n_s, D_MODEL)
    u_s, q_s, k_s, v_s = _in_proj(xs, row(p['ln1_g']), w_uqv, w_k, seg, qg, kg, n_s, F32)
    h0 = (_state_to_groups(state_ssm_re.reshape(n_dec, N_STATE), n_dec),
          _state_to_groups(state_ssm_im.reshape(n_dec, N_STATE), n_dec))
    ssm_s, hre_s, him_s = _ssm(u_s.reshape(1, n_s, SSM_WIDTH), mats,
                               (_lam_groups(lb_re, n_dec), _lam_groups(lb_im, n_dec)),
                               d_row, glu_b, h0, n_dec, t_new)
    n_phys = cache_k.shape[0]
    cache_kt = cache_k.transpose(0, 2, 3, 4, 1).reshape(n_phys, ATTN_WIDTH, PAGE_SIZE)
    att_s = _attn_sample(page_table, lam, q_s.reshape(n_dec, t_new, ATTN_WIDTH),
                         k_s.reshape(n_dec, t_new, ATTN_WIDTH), v_s.reshape(n_dec, t_new, ATTN_WIDTH),
                         sg, cache_kt, cache_v.reshape(n_phys, PAGE_SIZE * ATTN_HEADS, ATTN_V_DIM),
                         past_len, out_scale)
    x1_s, cq_s = _out_proj(xs, ssm_s.reshape(n_s, SSM_WIDTH), att_s.reshape(n_s, ATTN_WIDTH),
                           w1, w2, row(p['ln2_g']), wq_b, row(p['ca_q_norm_g']), n_s, F32)
    co_s = _cross_sample(cq_s.reshape(n_dec, t_new, D_MODEL),
                         _mem_rows(cache_mem_k), _mem_rows(cache_mem_v))
    tmaj = lambda a: a.reshape(n_dec, t_new, -1).transpose(1, 0, 2).reshape(n_s, -1)
    prev_s = state_conv.transpose(1, 0, 2).reshape(1, (CONV_WIDTH - 1) * n_dec, FFN_HIDDEN)
    y_s, conv_s = _ffn(tmaj(x1_s), tmaj(co_s), wo_b, row(p['ln3_g']), wg_b, wvf_b, cw, cb, wd_b,
                       prev_s, n_s, n_dec)
    y_s = y_s.reshape(t_new, n_dec, D_MODEL).transpose(1, 0, 2)
    conv_s = conv_s.reshape(CONV_WIDTH - 1, n_dec, FFN_HIDDEN).transpose(1, 0, 2)

    return (y_p.reshape(bsz, seq, D_MODEL), y_s,
            kt_p.reshape(bsz, ATTN_HEADS, 2, ATTN_HEAD_DIM, seq).transpose(0, 4, 1, 2, 3),
            v_p.reshape(bsz, seq, ATTN_HEADS, ATTN_V_DIM),
            k_s.reshape(n_dec, t_new, ATTN_HEADS, 2, ATTN_HEAD_DIM),
            v_s.reshape(n_dec, t_new, ATTN_HEADS, ATTN_V_DIM),
            _groups_to_state(hre_p, bsz), _groups_to_state(him_p, bsz),
            _groups_to_state(hre_s, n_dec), _groups_to_state(him_s, n_dec),
            conv_p, conv_s,
            mk.reshape(bsz, N_MEM, CA_HEADS, CA_HEAD_DIM), mv.reshape(bsz, N_MEM, CA_HEADS, CA_HEAD_DIM))


def kernel(x_prompt, x_sample, mem_prompt, cache_k, cache_v, page_table, state_ssm_re, state_ssm_im, state_conv, cache_mem_k, cache_mem_v, ln1_g, w_in, ssm_a_re, ssm_a_im, ssm_b_re, ssm_b_im, ssm_c_re, ssm_c_im, ssm_d, ssm_log_dt, ssm_glu_w, q_norm_g, k_norm_g, lam_q1, lam_k1, lam_q2, lam_k2, subln_g, w_out, ln2_g, mem_norm_g, ca_wq, ca_wk, ca_wv, ca_q_norm_g, ca_k_norm_g, ca_wo, ln3_g, ffn_wg, ffn_wv, ffn_conv_w, ffn_conv_b, ffn_wd):
    params = dict(
        ln1_g=ln1_g, w_in=w_in, ssm_a_re=ssm_a_re, ssm_a_im=ssm_a_im, ssm_b_re=ssm_b_re,
        ssm_b_im=ssm_b_im, ssm_c_re=ssm_c_re, ssm_c_im=ssm_c_im, ssm_d=ssm_d,
        ssm_log_dt=ssm_log_dt, ssm_glu_w=ssm_glu_w, q_norm_g=q_norm_g, k_norm_g=k_norm_g,
        lam_q1=lam_q1, lam_k1=lam_k1, lam_q2=lam_q2, lam_k2=lam_k2, subln_g=subln_g,
        w_out=w_out, ln2_g=ln2_g, mem_norm_g=mem_norm_g, ca_wq=ca_wq, ca_wk=ca_wk, ca_wv=ca_wv,
        ca_q_norm_g=ca_q_norm_g, ca_k_norm_g=ca_k_norm_g, ca_wo=ca_wo, ln3_g=ln3_g,
        ffn_wg=ffn_wg, ffn_wv=ffn_wv, ffn_conv_w=ffn_conv_w, ffn_conv_b=ffn_conv_b, ffn_wd=ffn_wd)
    y_p, y_s = x_prompt, x_sample
    per_layer = []
    for l in range(w_in.shape[0]):
        p = {k: v[l] for k, v in params.items()}
        outs = _layer(l, y_p, y_s, mem_prompt, cache_k[l], cache_v[l], page_table,
                      state_ssm_re[l], state_ssm_im[l], state_conv[l],
                      cache_mem_k[l], cache_mem_v[l], p)
        y_p, y_s = outs[0], outs[1]
        per_layer.append(outs[2:])
    return (y_p, y_s) + tuple(jnp.stack(leaf) for leaf in zip(*per_layer))
```

```python
import functools
import math

import jax
import jax.numpy as jnp
from jax import lax
from jax.experimental import pallas as pl
from jax.experimental.pallas import tpu as pltpu

F32 = jnp.float32
BF16 = jnp.bfloat16

D_MODEL = 1024
SSM_WIDTH = 512
SSM_GROUP_CH = 16
SSM_GROUPS = 32
SSM_STATE = 64
N_STATE = SSM_GROUPS * SSM_STATE
ATTN_WIDTH = 512
ATTN_HEAD_DIM = 64
ATTN_V_DIM = 128
ATTN_HEADS = 4
PAGE_SIZE = 128
N_MEM = 256
CA_HEADS = 4
CA_HEAD_DIM = 256
FFN_HIDDEN = 2816
CONV_WIDTH = 3
NORM_EPS = 1e-6

LANES = 128
SUBLANES = 8
VMEM_LIMIT = 56 * 1024 * 1024

NEG_BIG = -1e30
LOG2E = math.log2(math.e)

def _cparams(n_axes):
    return pltpu.CompilerParams(
        dimension_semantics=("arbitrary",) * n_axes,
        vmem_limit_bytes=VMEM_LIMIT)


def _const_spec(shape):
    nd = len(shape)
    return pl.BlockSpec(shape, lambda *_: (0,) * nd, pipeline_mode=pl.Buffered(1))


def _rms(x, g):
    return x * lax.rsqrt(jnp.mean(x * x, axis=-1, keepdims=True) + NORM_EPS) * g


def _dot(a, b):
    return jnp.dot(a, b, preferred_element_type=F32)


def _dot_nt(a, b):
    return lax.dot_general(a, b, (((1,), (1,)), ((), ())), preferred_element_type=F32)


def _head_rms(x, g, width, scale):
    outs = []
    for h in range(x.shape[-1] // width):
        c = x[:, h * width:(h + 1) * width]
        outs.append(_rms(c, g) * scale)
    return jnp.concatenate(outs, axis=-1)


def _prep_kernel(are_ref, aim_ref, ldt_ref, bre_ref, bim_ref,
                 q1_ref, k1_ref, q2_ref, k2_ref,
                 lbre_ref, lbim_ref, bbre_ref, bbim_ref, lam_ref, *, lam0):
    a_re = are_ref[...]
    a_im = aim_ref[...]
    dt = jnp.exp(ldt_ref[...])
    mag = jnp.exp(a_re * dt)
    lb_re = mag * jnp.cos(a_im * dt)
    lb_im = mag * jnp.sin(a_im * dt)
    den = a_re * a_re + a_im * a_im
    n_re = lb_re - 1.0
    f_re = (n_re * a_re + lb_im * a_im) / den
    f_im = (lb_im * a_re - n_re * a_im) / den
    b_re = bre_ref[...]
    b_im = bim_ref[...]
    lbre_ref[...] = lb_re
    lbim_ref[...] = lb_im
    bbre_ref[...] = f_re * b_re - f_im * b_im
    bbim_ref[...] = f_re * b_im + f_im * b_re
    s1 = jnp.sum(q1_ref[...] * k1_ref[...], axis=-1, keepdims=True)
    s2 = jnp.sum(q2_ref[...] * k2_ref[...], axis=-1, keepdims=True)
    lam_ref[...] = jnp.exp(s1) - jnp.exp(s2) + lam0


def _prep(a_re, a_im, log_dt, b_re, b_im, q1, k1, q2, k2, lam0):
    are = a_re.reshape(1, N_STATE)
    aim = a_im.reshape(1, N_STATE)
    ldt = jnp.repeat(log_dt, SSM_STATE).reshape(1, N_STATE)
    bre = b_re.transpose(2, 0, 1).reshape(SSM_GROUP_CH, N_STATE)
    bim = b_im.transpose(2, 0, 1).reshape(SSM_GROUP_CH, N_STATE)
    vec = lambda v: v.reshape(1, ATTN_HEAD_DIM)
    row = jax.ShapeDtypeStruct((1, N_STATE), F32)
    mat = jax.ShapeDtypeStruct((SSM_GROUP_CH, N_STATE), F32)
    return pl.pallas_call(
        functools.partial(_prep_kernel, lam0=lam0),
        out_shape=(row, row, mat, mat, jax.ShapeDtypeStruct((1, 1), F32)),
        name="prep",
    )(are, aim, ldt, bre, bim, vec(q1), vec(k1), vec(q2), vec(k2))


def _in_proj_kernel(x_ref, g_ref, w_ref, wk_ref, seg_ref, qg_ref, kg_ref,
                    u_ref, q_ref, k_ref, v_ref, *bf_refs, key_major, blk):
    xn = _rms(x_ref[...], g_ref[...]).astype(BF16)
    proj = _dot(xn, w_ref[...])
    seg = seg_ref[...]

    def chunk_norm(z, g):
        ms = _dot((z * z).astype(BF16), seg) * (1.0 / ATTN_HEAD_DIM)
        return z * lax.rsqrt(ms + NORM_EPS) * g

    u_ref[...] = proj[:, :SSM_WIDTH]
    q = proj[:, SSM_WIDTH:SSM_WIDTH + ATTN_WIDTH]
    v = proj[:, SSM_WIDTH + ATTN_WIDTH:]
    q_ref[...] = (chunk_norm(q, qg_ref[...]) * (ATTN_HEAD_DIM ** -0.5 * LOG2E)).astype(q_ref.dtype)
    if key_major:
        kb_ref, vb_ref = bf_refs
        tm = x_ref.shape[0]
        z = _dot_nt(wk_ref[...], xn).reshape(ATTN_WIDTH // ATTN_HEAD_DIM, ATTN_HEAD_DIM, tm)
        ms = jnp.mean(z * z, axis=1, keepdims=True)
        kn = (z * lax.rsqrt(ms + NORM_EPS) * kg_ref[...]).reshape(ATTN_WIDTH, tm)
        k_ref[0] = kn
        for i in range(tm // blk):
            kb_ref[0, i] = kn[:, i * blk:(i + 1) * blk].astype(BF16)
        for h in range(ATTN_HEADS):
            v_ref[:, h, :] = v[:, ATTN_V_DIM * h:ATTN_V_DIM * (h + 1)]
        vb_ref[...] = v.astype(BF16)
    else:
        k_ref[...] = chunk_norm(_dot(xn, wk_ref[...]), kg_ref[...])
        v_ref[...] = v


def _in_proj(x, ln1_g, w_uqv, wk, seg, qg, kg, tm, q_dtype, n_batch=None, blk=None):
    n = x.shape[0]
    key_major = n_batch is not None
    bsz = n_batch if key_major else 1
    per_b = n // bsz // tm
    tok = lambda width: pl.BlockSpec((tm, width), lambda b, i: (b * per_b + i, 0))
    out_specs = [tok(SSM_WIDTH), tok(ATTN_WIDTH)]
    out_shape = [jax.ShapeDtypeStruct((n, SSM_WIDTH), F32),
                 jax.ShapeDtypeStruct((n, ATTN_WIDTH), q_dtype)]
    if key_major:
        t = n // bsz
        out_specs += [pl.BlockSpec((1, ATTN_WIDTH, tm), lambda b, i: (b, 0, i)),
                      pl.BlockSpec((tm, ATTN_HEADS, ATTN_V_DIM), lambda b, i: (b * per_b + i, 0, 0)),
                      pl.BlockSpec((1, tm // blk, ATTN_WIDTH, blk), lambda b, i: (b, i, 0, 0)),
                      tok(ATTN_WIDTH)]
        out_shape += [jax.ShapeDtypeStruct((bsz, ATTN_WIDTH, t), F32),
                      jax.ShapeDtypeStruct((n, ATTN_HEADS, ATTN_V_DIM), F32),
                      jax.ShapeDtypeStruct((bsz, t // blk, ATTN_WIDTH, blk), BF16),
                      jax.ShapeDtypeStruct((n, ATTN_WIDTH), BF16)]
    else:
        out_specs += [tok(ATTN_WIDTH), tok(ATTN_WIDTH)]
        out_shape += [jax.ShapeDtypeStruct((n, ATTN_WIDTH), F32)] * 2
    return pl.pallas_call(
        functools.partial(_in_proj_kernel, key_major=key_major, blk=blk),
        grid=(bsz, per_b),
        in_specs=[tok(D_MODEL), _const_spec((1, D_MODEL)), _const_spec(w_uqv.shape),
                  _const_spec(wk.shape), _const_spec((ATTN_WIDTH, ATTN_WIDTH)),
                  _const_spec((1, ATTN_WIDTH)), _const_spec(kg.shape)],
        out_specs=out_specs,
        out_shape=out_shape,
        compiler_params=_cparams(2),
        name="in_proj",
    )(x, ln1_g, w_uqv, wk, seg, qg, kg)


N_SLABS = N_STATE // 2 // LANES
HALF = N_STATE // 2


def _gelu_tanh(x):
    c = math.sqrt(2.0 / math.pi)
    return 0.5 * x * (1.0 + jnp.tanh(c * (x + 0.044715 * (x * x * x))))


def _ssm_kernel(u_ref, bre_ref, bim_ref, cre_ref, cim_ref, lre_ref, lim_ref,
                d_ref, glu_ref, h0re_ref, h0im_ref,
                out_ref, hre_ref, him_ref, sre, sim,
                *, n_seq, tc, pitch, n_bulk, rows_bulk, pitch_bulk):
    c = pl.program_id(0)
    rows = n_seq * tc
    n_groups = 2 * n_seq // SUBLANES

    @pl.when(c == 0)
    def _():
        hre_ref[...] = h0re_ref[...]
        him_ref[...] = h0im_ref[...]

    u = u_ref[...].reshape(rows, SSM_WIDTH)
    ub = u.astype(BF16)

    for j in range(4):
        hh, jj = divmod(j, 2)
        uj = ub[:, LANES * j:LANES * (j + 1)]
        for src, dst in ((bre_ref, sre), (bim_ref, sim)):
            bu = _dot(uj, src[j])
            for kk in range(4):
                slab = 4 * jj + kk
                for bb in range(n_bulk):
                    r0 = (hh * n_bulk + bb) * pitch_bulk
                    dst[slab, r0:r0 + rows_bulk, :] = (
                        bu[bb * rows_bulk:(bb + 1) * rows_bulk, LANES * kk:LANES * (kk + 1)])

    def group_body(g, carry):
        base = g * (SUBLANES * pitch)
        lr = [lre_ref[g, k] for k in range(N_SLABS)]
        li = [lim_ref[g, k] for k in range(N_SLABS)]
        hr0 = tuple(hre_ref[g, k] for k in range(N_SLABS))
        hi0 = tuple(him_ref[g, k] for k in range(N_SLABS))

        def step(t, hc):
            hr, hi = hc
            idx = pl.ds(base + t, SUBLANES, stride=pitch)
            nr, ni = [], []
            for k in range(N_SLABS):
                br = sre[k, idx, :]
                bi = sim[k, idx, :]
                r = lr[k] * hr[k] - li[k] * hi[k] + br
                i = lr[k] * hi[k] + li[k] * hr[k] + bi
                sre[k, idx, :] = r
                sim[k, idx, :] = i
                nr.append(r)
                ni.append(i)
            return tuple(nr), tuple(ni)

        hr, hi = lax.fori_loop(0, tc, step, (hr0, hi0), unroll=min(tc, 4))
        for k in range(N_SLABS):
            hre_ref[g, k] = hr[k]
            him_ref[g, k] = hi[k]
        return carry

    lax.fori_loop(0, n_groups, group_body, 0)

    y_rows = []
    for bb in range(n_bulk):
        y_cols = []
        for j in range(4):
            hh, jj = divmod(j, 2)
            r0 = (hh * n_bulk + bb) * pitch_bulk
            hr = jnp.concatenate(
                [sre[4 * jj + kk, r0:r0 + rows_bulk, :] for kk in range(4)], axis=1).astype(BF16)
            hi = jnp.concatenate(
                [sim[4 * jj + kk, r0:r0 + rows_bulk, :] for kk in range(4)], axis=1).astype(BF16)
            y_cols.append(_dot(hr, cre_ref[j]) - _dot(hi, cim_ref[j]))
        y_rows.append(jnp.concatenate(y_cols, axis=1))
    y = jnp.concatenate(y_rows, axis=0) if n_bulk > 1 else y_rows[0]
    y = y + d_ref[...] * u
    gl = _gelu_tanh(y)
    gate = jax.nn.sigmoid(_dot(gl.astype(BF16), glu_ref[...]))
    out_ref[...] = (gl * gate).astype(out_ref.dtype).reshape(out_ref.shape)


def _ssm(u3, mats, lam_g, d_row, glu_b, h0, n_seq, tc):
    bre, bim, cre, cim = mats
    lre_g, lim_g = lam_g
    h0re, h0im = h0
    n_groups = 2 * n_seq // SUBLANES
    t_total = u3.shape[0] * u3.shape[1] // n_seq
    n_chunks = t_total // tc
    if tc % SUBLANES == 0:
        pitch = tc + SUBLANES
        n_bulk, rows_bulk, pitch_bulk = n_seq, tc, pitch
        ublock = (n_seq, tc, SSM_WIDTH)
    else:
        assert n_chunks == 1
        pitch = tc
        n_bulk, rows_bulk, pitch_bulk = 1, n_seq * tc, n_seq * tc
        ublock = (1, n_seq * tc, SSM_WIDTH)
    scr_rows = 2 * n_bulk * pitch_bulk
    hshape = (n_groups, N_SLABS, SUBLANES, LANES)
    kern = functools.partial(_ssm_kernel, n_seq=n_seq, tc=tc, pitch=pitch,
                             n_bulk=n_bulk, rows_bulk=rows_bulk, pitch_bulk=pitch_bulk)
    return pl.pallas_call(
        kern,
        grid=(n_chunks,),
        in_specs=[pl.BlockSpec(ublock, lambda c: (0, c, 0)),
                  _const_spec(bre.shape), _const_spec(bim.shape),
                  _const_spec(cre.shape), _const_spec(cim.shape),
                  _const_spec(hshape), _const_spec(hshape),
                  _const_spec((1, SSM_WIDTH)), _const_spec((SSM_WIDTH, SSM_WIDTH)),
                  _const_spec(hshape), _const_spec(hshape)],
        out_specs=[pl.BlockSpec(ublock, lambda c: (0, c, 0)),
                   _const_spec(hshape), _const_spec(hshape)],
        out_shape=[jax.ShapeDtypeStruct(u3.shape, BF16),
                   jax.ShapeDtypeStruct(hshape, F32),
                   jax.ShapeDtypeStruct(hshape, F32)],
        scratch_shapes=[pltpu.VMEM((N_SLABS, scr_rows, LANES), F32),
                        pltpu.VMEM((N_SLABS, scr_rows, LANES), F32)],
        compiler_params=_cparams(1),
        name="ssm",
    )(u3, bre, bim, cre, cim, lre_g, lim_g, d_row, glu_b, h0re, h0im)


def _state_to_groups(h, n_seq):
    x = h.reshape(n_seq, 2, N_SLABS, LANES).transpose(1, 0, 2, 3)
    x = x.reshape(2 * n_seq // SUBLANES, SUBLANES, N_SLABS, LANES)
    return x.transpose(0, 2, 1, 3)


def _groups_to_state(x, n_seq):
    x = x.transpose(0, 2, 1, 3).reshape(2, n_seq, N_SLABS, LANES)
    return x.transpose(1, 0, 2, 3).reshape(n_seq, SSM_GROUPS, SSM_STATE)


def _head_slope(h):
    return jnp.where(h == 0, 2.0 ** -2, jnp.where(h == 1, 2.0 ** -4,
                     jnp.where(h == 2, 2.0 ** -6, 2.0 ** -8))).astype(F32)


def _attn_prompt_kernel(lam_ref, q_ref, k_ref, v_ref, sg_ref, o_ref,
                        q_scr, s_scr, m_scr, l_scr, acc_scr, *, bq, bk, out_scale):
    h = pl.program_id(1)
    qi = pl.program_id(2)
    slope = _head_slope(h) * LOG2E
    lam = lam_ref[0, 0]
    n_maps = 2

    q = q_ref[0]
    lane = lax.broadcasted_iota(jnp.int32, q.shape, 1)
    zero = jnp.zeros_like(q)
    q_scr[0] = jnp.where(lane < ATTN_HEAD_DIM, q, zero)
    q_scr[1] = jnp.where(lane >= ATTN_HEAD_DIM, q, zero)

    col = lax.broadcasted_iota(jnp.int32, (1, bk), 1)
    q0 = qi * bq

    def lane_tiles(x):
        return [x[:, LANES * i:LANES * (i + 1)] for i in range(bk // LANES)]

    n_full = q0 // bk

    def run_pairs(first, count, step):
        odd = lax.rem(count, 2)

        @pl.when(odd == 1)
        def _():
            step(first)

        def body(t, carry):
            j = first + odd + 2 * t
            step(j)
            step(j + 1)
            return carry

        lax.fori_loop(0, count // 2, body, 0)

    def scores(j, i, masked):
        kpos = col + j * bk
        s = _dot(q_scr[i], k_ref[0, j]) + slope * (kpos - q0).astype(F32)
        if masked:
            qpos = lax.broadcasted_iota(jnp.int32, (bq, bk), 0) + q0
            s = jnp.where(qpos >= kpos, s, -jnp.inf)
        return s

    def accumulate(j, i, p):
        start = pl.multiple_of(j * bk, bk)
        lpart = l_scr[i]
        for piece in lane_tiles(p):
            lpart = lpart + piece
        l_scr[i] = lpart
        acc_scr[i] += _dot(p.astype(BF16), v_ref[0, pl.ds(start, bk), :])

    def score_step(j, masked=False):
        for i in range(n_maps):
            s = scores(j, i, masked)
            s_scr[j, i] = s
            mpart = m_scr[i]
            for piece in lane_tiles(s):
                mpart = jnp.maximum(mpart, piece)
            m_scr[i] = mpart

    m_scr[...] = jnp.full(m_scr.shape, -jnp.inf, F32)
    score_step(n_full, masked=True)
    run_pairs(0, n_full, score_step)
    ms = [jnp.max(m_scr[i], axis=-1, keepdims=True) for i in range(n_maps)]

    def sum_step(j):
        for i in range(n_maps):
            accumulate(j, i, jnp.exp2(s_scr[j, i] - ms[i]))

    l_scr[...] = jnp.zeros(l_scr.shape, F32)
    acc_scr[...] = jnp.zeros(acc_scr.shape, F32)
    run_pairs(0, n_full + 1, sum_step)

    outs = [acc_scr[i] / jnp.sum(l_scr[i], axis=-1, keepdims=True) for i in range(n_maps)]
    o = outs[0] - lam * outs[1]
    o_ref[0] = (_rms(o, sg_ref[...]) * out_scale).astype(o_ref.dtype)


def _attn_sample_kernel(pt_ref, lam_ref, q_ref, kn_ref, vn_ref, sg_ref, *refs,
                        n_pages, t_new, past_len, out_scale):
    k_refs = refs[:n_pages]
    v_refs = refs[n_pages:2 * n_pages]
    o_ref = refs[2 * n_pages]
    del pt_ref
    lam = lam_ref[0, 0]
    n_rows = 2 * ATTN_HEADS * t_new

    q4 = q_ref[0]
    qe = jnp.concatenate([q4] * (2 * ATTN_HEADS), axis=0)
    row = lax.broadcasted_iota(jnp.int32, qe.shape, 0)
    lane = lax.broadcasted_iota(jnp.int32, qe.shape, 1)
    qe = jnp.where(lane // ATTN_HEAD_DIM == row // t_new, qe, 0.0)
    qeb = qe.astype(BF16)

    rcol = lax.broadcasted_iota(jnp.int32, (n_rows, 1), 0)
    head_c = rcol // (2 * t_new)
    tok_c = rcol % t_new
    slope = _head_slope(head_c) * LOG2E
    rows_h = 2 * t_new

    kt = jnp.concatenate([k_refs[p][0].astype(BF16) for p in range(n_pages)], axis=1)
    kpos = lax.broadcasted_iota(jnp.int32, (1, n_pages * PAGE_SIZE), 1) - past_len
    s = _dot(qeb, kt) + slope * kpos.astype(F32)

    kn = kn_ref[0]
    vn = vn_ref[0]
    s_new = []
    for c in range(t_new):
        sc = jnp.sum(qe * kn[c:c + 1, :], axis=-1, keepdims=True) + slope * float(c)
        s_new.append(jnp.where(tok_c >= c, sc, -jnp.inf))

    m = jnp.max(s, axis=-1, keepdims=True)
    for sc in s_new:
        m = jnp.maximum(m, sc)
    pe = jnp.exp2(s - m)
    l = jnp.sum(pe, axis=-1, keepdims=True)

    pv = []
    for h in range(ATTN_HEADS):
        vh = jnp.concatenate(
            [v_refs[p][0, pl.ds(h, PAGE_SIZE, stride=ATTN_HEADS), :].astype(BF16)
             for p in range(n_pages)], axis=0)
        pv.append(_dot(pe[rows_h * h:rows_h * (h + 1)].astype(BF16), vh))
    acc = jnp.concatenate(pv, axis=0)

    for c in range(t_new):
        pc = jnp.exp2(s_new[c] - m)
        l = l + pc
        vc = jnp.concatenate(
            [jnp.broadcast_to(vn[c:c + 1, ATTN_V_DIM * h:ATTN_V_DIM * (h + 1)], (rows_h, ATTN_V_DIM))
             for h in range(ATTN_HEADS)], axis=0)
        acc = acc + pc * vc

    acc = acc / l
    outs = []
    for h in range(ATTN_HEADS):
        a0 = acc[rows_h * h:rows_h * h + t_new]
        a1 = acc[rows_h * h + t_new:rows_h * (h + 1)]
        outs.append(_rms(a0 - lam * a1, sg_ref[...]) * out_scale)
    o_ref[0] = jnp.concatenate(outs, axis=-1)


def _attention_kernel(pt_ref, lam_ref, sg_ref, q_ref, k_ref, v_ref, qs_ref, kn_ref, vn_ref, *refs,
                      n_pages, t_new, past_len, bq, bk, out_scale):
    page_refs = refs[:2 * n_pages]
    o_ref, os_ref = refs[2 * n_pages:2 * n_pages + 2]
    scratch = refs[2 * n_pages + 2:]
    _attn_sample_kernel(pt_ref, lam_ref, qs_ref, kn_ref, vn_ref, sg_ref, *page_refs, os_ref,
                        n_pages=n_pages, t_new=t_new, past_len=past_len, out_scale=out_scale)
    _attn_prompt_kernel(lam_ref, q_ref, k_ref, v_ref, sg_ref, o_ref, *scratch,
                        bq=bq, bk=bk, out_scale=out_scale)


def _attention(page_table, lam, sg, qb, kb, vb, q3, kn3, vn3, cache_kt, cache_v3,
               bq, past_len, out_scale):
    bsz, t, _ = qb.shape
    bk = kb.shape[-1]
    nq = t // bq
    n_dec, t_new, _ = q3.shape
    n_pages = page_table.shape[1]
    assert n_dec == bsz * ATTN_HEADS * nq, "one sample sequence per prompt grid step"
    pt = page_table.reshape(-1)

    def seq(b, h, i):
        return (b * ATTN_HEADS + h) * nq + i

    tok_spec = pl.BlockSpec((1, t_new, ATTN_WIDTH), lambda b, h, i, pt: (seq(b, h, i), 0, 0))

    def k_spec(p):
        return pl.BlockSpec((1, ATTN_WIDTH, PAGE_SIZE),
                            lambda b, h, i, pt, p=p: (pt[seq(b, h, i) * n_pages + p], 0, 0))

    def v_spec(p):
        return pl.BlockSpec((1, PAGE_SIZE * ATTN_HEADS, ATTN_V_DIM),
                            lambda b, h, i, pt, p=p: (pt[seq(b, h, i) * n_pages + p], 0, 0))

    head_spec = pl.BlockSpec((1, bq, ATTN_V_DIM), lambda b, h, i, pt: (b, i, h))
    grid_spec = pltpu.PrefetchScalarGridSpec(
        num_scalar_prefetch=1,
        grid=(bsz, ATTN_HEADS, nq),
        in_specs=[pl.BlockSpec(memory_space=pltpu.SMEM), _const_spec((1, ATTN_V_DIM)),
                  head_spec,
                  pl.BlockSpec((1, t // bk, ATTN_V_DIM, bk), lambda b, h, i, pt: (b, 0, h, 0)),
                  pl.BlockSpec((1, t, ATTN_V_DIM), lambda b, h, i, pt: (b, 0, h)),
                  tok_spec, tok_spec, tok_spec]
                 + [k_spec(p) for p in range(n_pages)] + [v_spec(p) for p in range(n_pages)],
        out_specs=[head_spec, tok_spec],
        scratch_shapes=[pltpu.VMEM((2, bq, ATTN_V_DIM), BF16),
                        pltpu.VMEM((t // bk, 2, bq, bk), F32),
                        pltpu.VMEM((2, bq, LANES), F32),
                        pltpu.VMEM((2, bq, LANES), F32),
                        pltpu.VMEM((2, bq, ATTN_V_DIM), F32)])
    kern = functools.partial(_attention_kernel, n_pages=n_pages, t_new=t_new, past_len=past_len,
                             bq=bq, bk=bk, out_scale=out_scale)
    return pl.pallas_call(
        kern,
        grid_spec=grid_spec,
        out_shape=[jax.ShapeDtypeStruct(qb.shape, BF16), jax.ShapeDtypeStruct(q3.shape, F32)],
        compiler_params=_cparams(3),
        name="attention",
    )(pt, lam, sg, qb, kb, vb, q3, kn3, vn3, *([cache_kt] * n_pages), *([cache_v3] * n_pages))


def _out_proj_kernel(x_ref, s_ref, o_ref, w1_ref, w2_ref, g2_ref, wq_ref, qg_ref,
                     x1_ref, cq_ref):
    x1 = (x_ref[...] + _dot(s_ref[...].astype(BF16), w1_ref[...])
          + _dot(o_ref[...].astype(BF16), w2_ref[...]))
    x1_ref[...] = x1
    xn = _rms(x1, g2_ref[...]).astype(BF16)
    cq = _dot(xn, wq_ref[...])
    cq_ref[...] = _head_rms(cq, qg_ref[...], CA_HEAD_DIM, CA_HEAD_DIM ** -0.5).astype(cq_ref.dtype)


def _out_proj(x, ssm_o, att_o, w1, w2, g2, wq, qg, tm, cq_dtype):
    n = x.shape[0]
    tok = lambda width: pl.BlockSpec((tm, width), lambda i: (i, 0))
    return pl.pallas_call(
        _out_proj_kernel,
        grid=(n // tm,),
        in_specs=[tok(D_MODEL), tok(SSM_WIDTH), tok(ATTN_WIDTH),
                  _const_spec((SSM_WIDTH, D_MODEL)), _const_spec((ATTN_WIDTH, D_MODEL)),
                  _const_spec((1, D_MODEL)), _const_spec((D_MODEL, D_MODEL)),
                  _const_spec((1, CA_HEAD_DIM))],
        out_specs=[tok(D_MODEL), tok(D_MODEL)],
        out_shape=[jax.ShapeDtypeStruct((n, D_MODEL), F32),
                   jax.ShapeDtypeStruct((n, D_MODEL), cq_dtype)],
        compiler_params=_cparams(1),
        name="out_proj",
    )(x, ssm_o, att_o, w1, w2, g2, wq, qg)


def _mem_kv_kernel(m_ref, g_ref, wk_ref, wv_ref, kg_ref, mk_ref, mv_ref, mkb_ref, mvb_ref):
    mn = _rms(m_ref[...], g_ref[...]).astype(BF16)
    mk = _head_rms(_dot(mn, wk_ref[...]), kg_ref[...], CA_HEAD_DIM, 1.0)
    mv = _dot(mn, wv_ref[...])
    mk_ref[...] = mk
    mv_ref[...] = mv
    mkb_ref[...] = mk.astype(BF16)
    mvb_ref[...] = mv.astype(BF16)


def _mem_kv(mem, g, wk, wv, kg, tm):
    n = mem.shape[0]
    tok = pl.BlockSpec((tm, D_MODEL), lambda i: (i, 0))
    wspec = _const_spec((D_MODEL, D_MODEL))
    return pl.pallas_call(
        _mem_kv_kernel,
        grid=(n // tm,),
        in_specs=[tok, _const_spec((1, D_MODEL)), wspec, wspec, _const_spec((1, CA_HEAD_DIM))],
        out_specs=[tok, tok, tok, tok],
        out_shape=[jax.ShapeDtypeStruct((n, D_MODEL), F32)] * 2
                  + [jax.ShapeDtypeStruct((n, D_MODEL), BF16)] * 2,
        compiler_params=_cparams(1),
        name="mem_kv",
    )(mem, g, wk, wv, kg)


def _cross_prompt_kernel(cq_ref, mk_ref, mv_ref, co_ref):
    cq = cq_ref[...]
    outs = []
    for h in range(CA_HEADS):
        sl = slice(CA_HEAD_DIM * h, CA_HEAD_DIM * (h + 1))
        s = _dot_nt(cq[:, sl], mk_ref[0, :, sl])
        p = jnp.exp(s - jnp.max(s, axis=-1, keepdims=True))
        l = jnp.sum(p, axis=-1, keepdims=True)
        outs.append(_dot((p / l).astype(BF16), mv_ref[0, :, sl]))
    co_ref[...] = jnp.concatenate(outs, axis=-1).astype(co_ref.dtype)


def _cross_prompt(cq, mkb, mvb, tm):
    n = cq.shape[0]
    bsz = mkb.shape[0]
    per_b = n // bsz // tm
    tok = pl.BlockSpec((tm, D_MODEL), lambda b, i: (b * per_b + i, 0))
    mem = pl.BlockSpec((1, N_MEM, D_MODEL), lambda b, i: (b, 0, 0))
    return pl.pallas_call(
        _cross_prompt_kernel,
        grid=(bsz, per_b),
        in_specs=[tok, mem, mem],
        out_specs=tok,
        out_shape=jax.ShapeDtypeStruct((n, D_MODEL), BF16),
        compiler_params=_cparams(2),
        name="cross_prompt",
    )(cq, mkb, mvb)


def _cross_sample_kernel(cq_ref, mk_ref, mv_ref, co_ref, *, t_new):
    halves = CA_HEAD_DIM // LANES
    per_key = halves * CA_HEADS

    def heads(ref, b):
        return jnp.concatenate(
            [ref[b, pl.ds(half * CA_HEADS + h, N_MEM, stride=per_key), :]
             for h in range(CA_HEADS) for half in range(halves)], axis=1).astype(BF16)

    for b in range(cq_ref.shape[0]):
        qe = jnp.concatenate([cq_ref[b]] * CA_HEADS, axis=0)
        row = lax.broadcasted_iota(jnp.int32, qe.shape, 0)
        lane = lax.broadcasted_iota(jnp.int32, qe.shape, 1)
        qe = jnp.where(lane // CA_HEAD_DIM == row // t_new, qe, 0.0).astype(BF16)
        s = _dot_nt(qe, heads(mk_ref, b))
        p = jnp.exp(s - jnp.max(s, axis=-1, keepdims=True))
        l = jnp.sum(p, axis=-1, keepdims=True)
        full = _dot((p / l).astype(BF16), heads(mv_ref, b))
        outs = [full[h * t_new:(h + 1) * t_new, CA_HEAD_DIM * h:CA_HEAD_DIM * (h + 1)]
                for h in range(CA_HEADS)]
        co_ref[b] = jnp.concatenate(outs, axis=-1)


def _mem_rows(mem):
    n = mem.shape[0]
    halves = CA_HEAD_DIM // LANES
    return (mem.reshape(n, N_MEM, CA_HEADS, halves, LANES).transpose(0, 1, 3, 2, 4)
            .reshape(n, N_MEM * halves * CA_HEADS, LANES))


def _cross_sample(cq3, mem_k, mem_v):
    n_dec, t_new, _ = cq3.shape
    per_step = 4
    tok = pl.BlockSpec((per_step, t_new, D_MODEL), lambda b: (b, 0, 0))
    mem = pl.BlockSpec((per_step,) + mem_k.shape[1:], lambda b: (b, 0, 0))
    return pl.pallas_call(
        functools.partial(_cross_sample_kernel, t_new=t_new),
        grid=(n_dec // per_step,),
        in_specs=[tok, mem, mem],
        out_specs=tok,
        out_shape=jax.ShapeDtypeStruct(cq3.shape, F32),
        compiler_params=_cparams(1),
        name="cross_sample",
    )(cq3, mem_k, mem_v)


def _ffn_kernel(x_ref, co_ref, wo_ref, g3_ref, wg_ref, wv_ref, cw_ref, cb_ref, wd_ref, prev_ref,
                y_ref, cs_ref, hbuf, *, tm, shift, pad):
    i = pl.program_id(1)

    @pl.when(i == 0)
    def _():
        hbuf[pad - 2 * shift:pad, :] = prev_ref[0]

    x2 = x_ref[...] + _dot(co_ref[...].astype(BF16), wo_ref[...])
    xn = _rms(x2, g3_ref[...]).astype(BF16)
    hbuf[pad:pad + tm, :] = _dot(xn, wg_ref[...])
    hv = _dot(xn, wv_ref[...])
    cw = cw_ref[...]
    conv = (cb_ref[...] + cw[0:1, :] * hbuf[pad - 2 * shift:pad - 2 * shift + tm, :]
            + cw[1:2, :] * hbuf[pad - shift:pad - shift + tm, :]
            + cw[2:3, :] * hbuf[pad:pad + tm, :])
    act = (conv * jax.nn.sigmoid(conv) * hv).astype(BF16)
    y_ref[...] = x2 + _dot(act, wd_ref[...])
    last = hbuf[pad + tm - 2 * shift:pad + tm, :]
    cs_ref[0] = last
    hbuf[pad - 2 * shift:pad, :] = last


def _ffn(x1, co, wo, g3, wg, wv, cw, cb, wd, prev, tm, shift):
    n = x1.shape[0]
    n_seq_blocks = prev.shape[0]
    per_b = n // n_seq_blocks // tm
    pad = -(-2 * shift // SUBLANES) * SUBLANES
    tok = pl.BlockSpec((tm, D_MODEL), lambda b, i: (b * per_b + i, 0))
    state = pl.BlockSpec((1, 2 * shift, FFN_HIDDEN), lambda b, i: (b, 0, 0))
    return pl.pallas_call(
        functools.partial(_ffn_kernel, tm=tm, shift=shift, pad=pad),
        grid=(n_seq_blocks, per_b),
        in_specs=[tok, tok, _const_spec((D_MODEL, D_MODEL)), _const_spec((1, D_MODEL)),
                  _const_spec((D_MODEL, FFN_HIDDEN)), _const_spec((D_MODEL, FFN_HIDDEN)),
                  _const_spec((CONV_WIDTH, FFN_HIDDEN)), _const_spec((1, FFN_HIDDEN)),
                  _const_spec((FFN_HIDDEN, D_MODEL)), state],
        out_specs=[tok, state],
        out_shape=[jax.ShapeDtypeStruct((n, D_MODEL), F32),
                   jax.ShapeDtypeStruct(prev.shape, F32)],
        scratch_shapes=[pltpu.VMEM((pad + tm, FFN_HIDDEN), F32)],
        compiler_params=_cparams(2),
        name="ffn",
    )(x1, co, wo, g3, wg, wv, cw, cb, wd, prev)


def _block_diag(blocks, n_outer):
    n, r, c = blocks.shape
    per = n // n_outer
    b = blocks.reshape(n_outer, per, r, c)
    eye = jnp.eye(per, dtype=blocks.dtype)
    full = b[:, :, :, None, :] * eye[None, :, None, :, None]
    return full.reshape(n_outer, per * r, per * c)


def _lam_groups(lb, n_seq):
    halves = lb.reshape(2, 1, N_SLABS, LANES)
    rows = jnp.broadcast_to(halves, (2, n_seq, N_SLABS, LANES))
    rows = rows.reshape(2 * n_seq // SUBLANES, SUBLANES, N_SLABS, LANES)
    return rows.transpose(0, 2, 1, 3)


def _layer(l, x_prompt, x_sample, mem_prompt, cache_k, cache_v, page_table,
           state_ssm_re, state_ssm_im, state_conv, cache_mem_k, cache_mem_v, p):
    bsz, seq, _ = x_prompt.shape
    n_dec, t_new, _ = x_sample.shape
    n_p = bsz * seq
    n_s = n_dec * t_new
    past_len = page_table.shape[1] * PAGE_SIZE
    lam0 = 0.8 - 0.6 * math.exp(-0.3 * l)
    out_scale = 1.0 - lam0
    row = lambda v: v.reshape(1, -1).astype(F32)
    bf = lambda w: w.astype(BF16)

    lb_re, lb_im, bb_re, bb_im, lam = _prep(
        p['ssm_a_re'], p['ssm_a_im'], p['ssm_log_dt'], p['ssm_b_re'], p['ssm_b_im'],
        p['lam_q1'], p['lam_k1'], p['lam_q2'], p['lam_k2'], lam0)

    def b_mats(bbt):
        blocks = bbt.reshape(SSM_GROUP_CH, SSM_GROUPS, SSM_STATE).transpose(1, 0, 2)
        return bf(_block_diag(blocks, 4))

    def c_mats(cm):
        return bf(_block_diag(cm.transpose(0, 2, 1), 4))

    mats = (b_mats(bb_re), b_mats(bb_im), c_mats(p['ssm_c_re']), c_mats(p['ssm_c_im']))
    d_row = row(p['ssm_d'])
    glu_b = bf(p['ssm_glu_w'])

    w_in = p['w_in']
    k0, k1 = SSM_WIDTH + ATTN_WIDTH, SSM_WIDTH + 2 * ATTN_WIDTH
    w_uqv = bf(jnp.concatenate([w_in[:, :k0], w_in[:, k1:]], axis=1))
    w_k = bf(w_in[:, k0:k1])
    seg = jnp.kron(jnp.eye(ATTN_WIDTH // ATTN_HEAD_DIM, dtype=F32),
                   jnp.ones((ATTN_HEAD_DIM, ATTN_HEAD_DIM), F32)).astype(BF16)
    qg = jnp.tile(p['q_norm_g'], ATTN_WIDTH // ATTN_HEAD_DIM).reshape(1, ATTN_WIDTH)
    kg = jnp.tile(p['k_norm_g'], ATTN_WIDTH // ATTN_HEAD_DIM).reshape(1, ATTN_WIDTH)
    sg = row(p['subln_g'])
    w_out_b = bf(p['w_out'])
    w1, w2 = w_out_b[:SSM_WIDTH], w_out_b[SSM_WIDTH:]
    wq_b, wk_b, wv_b, wo_b = bf(p['ca_wq']), bf(p['ca_wk']), bf(p['ca_wv']), bf(p['ca_wo'])
    wg_b, wvf_b, wd_b = bf(p['ffn_wg']), bf(p['ffn_wv']), bf(p['ffn_wd'])
    cw = p['ffn_conv_w'].astype(F32)
    cb = row(p['ffn_conv_b'])

    xp = x_prompt.reshape(n_p, D_MODEL)
    attn_bq, attn_bk = 512, 512
    kg_col = jnp.broadcast_to(p['k_norm_g'].reshape(ATTN_HEAD_DIM, 1), (ATTN_HEAD_DIM, 512))
    u_p, q_p, kt_p, v_p, kb_p, vb_p = _in_proj(xp, row(p['ln1_g']), w_uqv, w_k.T, seg, qg, kg_col,
                                               512, BF16, n_batch=bsz, blk=attn_bk)
    zeros_h = jnp.zeros((2 * bsz // SUBLANES, N_SLABS, SUBLANES, LANES), F32)
    ssm_p, hre_p, him_p = _ssm(u_p.reshape(bsz, seq, SSM_WIDTH), mats,
                               (_lam_groups(lb_re, bsz), _lam_groups(lb_im, bsz)),
                               d_row, glu_b, (zeros_h, zeros_h), bsz, 256)

    xs = x_sample.reshape(n_s, D_MODEL)
    u_s, q_s, k_s, v_s = _in_proj(xs, row(p['ln1_g']), w_uqv, w_k, seg, qg, kg, n_s, F32)
    h0 = (_state_to_groups(state_ssm_re.reshape(n_dec, N_STATE), n_dec),
          _state_to_groups(state_ssm_im.reshape(n_dec, N_STATE), n_dec))
    ssm_s, hre_s, him_s = _ssm(u_s.reshape(1, n_s, SSM_WIDTH), mats,
                               (_lam_groups(lb_re, n_dec), _lam_groups(lb_im, n_dec)),
                               d_row, glu_b, h0, n_dec, t_new)

    n_phys = cache_k.shape[0]
    cache_kt = cache_k.transpose(0, 2, 3, 4, 1).reshape(n_phys, ATTN_WIDTH, PAGE_SIZE)
    tok3 = lambda a: a.reshape(n_dec, t_new, ATTN_WIDTH)
    att_p, att_s = _attention(page_table, lam, sg, q_p.reshape(bsz, seq, ATTN_WIDTH), kb_p,
                              vb_p.reshape(bsz, seq, ATTN_WIDTH), tok3(q_s), tok3(k_s), tok3(v_s),
                              cache_kt, cache_v.reshape(n_phys, PAGE_SIZE * ATTN_HEADS, ATTN_V_DIM),
                              attn_bq, past_len, out_scale)

    x1_p, cq_p = _out_proj(xp, ssm_p.reshape(n_p, SSM_WIDTH), att_p.reshape(n_p, ATTN_WIDTH),
                           w1, w2, row(p['ln2_g']), wq_b, row(p['ca_q_norm_g']), 512, BF16)
    mk, mv, mkb, mvb = _mem_kv(mem_prompt.reshape(bsz * N_MEM, D_MODEL), row(p['mem_norm_g']),
                               wk_b, wv_b, row(p['ca_k_norm_g']), 256)
    co_p = _cross_prompt(cq_p, mkb.reshape(bsz, N_MEM, D_MODEL), mvb.reshape(bsz, N_MEM, D_MODEL), 512)
    y_p, conv_p = _ffn(x1_p, co_p, wo_b, row(p['ln3_g']), wg_b, wvf_b, cw, cb, wd_b,
                       jnp.zeros((bsz, CONV_WIDTH - 1, FFN_HIDDEN), F32), 512, 1)

    x1_s, cq_s = _out_proj(xs, ssm_s.reshape(n_s, SSM_WIDTH), att_s.reshape(n_s, ATTN_WIDTH),
                           w1, w2, row(p['ln2_g']), wq_b, row(p['ca_q_norm_g']), n_s, F32)
    co_s = _cross_sample(cq_s.reshape(n_dec, t_new, D_MODEL),
                         _mem_rows(cache_mem_k), _mem_rows(cache_mem_v))
    tmaj = lambda a: a.reshape(n_dec, t_new, -1).transpose(1, 0, 2).reshape(n_s, -1)
    prev_s = state_conv.transpose(1, 0, 2).reshape(1, (CONV_WIDTH - 1) * n_dec, FFN_HIDDEN)
    y_s, conv_s = _ffn(tmaj(x1_s), tmaj(co_s), wo_b, row(p['ln3_g']), wg_b, wvf_b, cw, cb, wd_b,
                       prev_s, n_s, n_dec)
    y_s = y_s.reshape(t_new, n_dec, D_MODEL).transpose(1, 0, 2)
    conv_s = conv_s.reshape(CONV_WIDTH - 1, n_dec, FFN_HIDDEN).transpose(1, 0, 2)

    return (y_p.reshape(bsz, seq, D_MODEL), y_s,
            kt_p.reshape(bsz, ATTN_HEADS, 2, ATTN_HEAD_DIM, seq).transpose(0, 4, 1, 2, 3),
            v_p.reshape(bsz, seq, ATTN_HEADS, ATTN_V_DIM),
            k_s.reshape(n_dec, t_new, ATTN_HEADS, 2, ATTN_HEAD_DIM),
            v_s.reshape(n_dec, t_new, ATTN_HEADS, ATTN_V_DIM),
            _groups_to_state(hre_p, bsz), _groups_to_state(him_p, bsz),
            _groups_to_state(hre_s, n_dec), _groups_to_state(him_s, n_dec),
            conv_p, conv_s,
            mk.reshape(bsz, N_MEM, CA_HEADS, CA_HEAD_DIM), mv.reshape(bsz, N_MEM, CA_HEADS, CA_HEAD_DIM))


def kernel(x_prompt, x_sample, mem_prompt, cache_k, cache_v, page_table, state_ssm_re, state_ssm_im, state_conv, cache_mem_k, cache_mem_v, ln1_g, w_in, ssm_a_re, ssm_a_im, ssm_b_re, ssm_b_im, ssm_c_re, ssm_c_im, ssm_d, ssm_log_dt, ssm_glu_w, q_norm_g, k_norm_g, lam_q1, lam_k1, lam_q2, lam_k2, subln_g, w_out, ln2_g, mem_norm_g, ca_wq, ca_wk, ca_wv, ca_q_norm_g, ca_k_norm_g, ca_wo, ln3_g, ffn_wg, ffn_wv, ffn_conv_w, ffn_conv_b, ffn_wd):
    params = dict(
        ln1_g=ln1_g, w_in=w_in, ssm_a_re=ssm_a_re, ssm_a_im=ssm_a_im, ssm_b_re=ssm_b_re,
        ssm_b_im=ssm_b_im, ssm_c_re=ssm_c_re, ssm_c_im=ssm_c_im, ssm_d=ssm_d,
        ssm_log_dt=ssm_log_dt, ssm_glu_w=ssm_glu_w, q_norm_g=q_norm_g, k_norm_g=k_norm_g,
        lam_q1=lam_q1, lam_k1=lam_k1, lam_q2=lam_q2, lam_k2=lam_k2, subln_g=subln_g,
        w_out=w_out, ln2_g=ln2_g, mem_norm_g=mem_norm_g, ca_wq=ca_wq, ca_wk=ca_wk, ca_wv=ca_wv,
        ca_q_norm_g=ca_q_norm_g, ca_k_norm_g=ca_k_norm_g, ca_wo=ca_wo, ln3_g=ln3_g,
        ffn_wg=ffn_wg, ffn_wv=ffn_wv, ffn_conv_w=ffn_conv_w, ffn_conv_b=ffn_conv_b, ffn_wd=ffn_wd)
    y_p, y_s = x_prompt, x_sample
    per_layer = []
    for l in range(w_in.shape[0]):
        p = {k: v[l] for k, v in params.items()}
        outs = _layer(l, y_p, y_s, mem_prompt, cache_k[l], cache_v[l], page_table,
                      state_ssm_re[l], state_ssm_im[l], state_conv[l],
                      cache_mem_k[l], cache_mem_v[l], p)
        y_p, y_s = outs[0], outs[1]
        per_layer.append(outs[2:])
    return (y_p, y_s) + tuple(jnp.stack(leaf) for leaf in zip(*per_layer))
```

```python
import functools
import math

import jax
import jax.numpy as jnp
from jax import lax
from jax.experimental import pallas as pl
from jax.experimental.pallas import tpu as pltpu

F32 = jnp.float32
BF16 = jnp.bfloat16

D_MODEL = 1024
SSM_WIDTH = 512
SSM_GROUP_CH = 16
SSM_GROUPS = 32
SSM_STATE = 64
N_STATE = SSM_GROUPS * SSM_STATE
ATTN_WIDTH = 512
ATTN_HEAD_DIM = 64
ATTN_V_DIM = 128
ATTN_HEADS = 4
PAGE_SIZE = 128
N_MEM = 256
CA_HEADS = 4
CA_HEAD_DIM = 256
FFN_HIDDEN = 2816
CONV_WIDTH = 3
NORM_EPS = 1e-6

LANES = 128
SUBLANES = 8
VMEM_LIMIT = 56 * 1024 * 1024

NEG_BIG = -1e30
LOG2E = math.log2(math.e)

def _cparams(n_axes):
    return pltpu.CompilerParams(
        dimension_semantics=("arbitrary",) * n_axes,
        vmem_limit_bytes=VMEM_LIMIT)


def _const_spec(shape):
    nd = len(shape)
    return pl.BlockSpec(shape, lambda *_: (0,) * nd, pipeline_mode=pl.Buffered(1))


def _rms(x, g):
    return x * lax.rsqrt(jnp.mean(x * x, axis=-1, keepdims=True) + NORM_EPS) * g


def _dot(a, b):
    return jnp.dot(a, b, preferred_element_type=F32)


def _dot_nt(a, b):
    return lax.dot_general(a, b, (((1,), (1,)), ((), ())), preferred_element_type=F32)


def _head_rms(x, g, width, scale):
    outs = []
    for h in range(x.shape[-1] // width):
        c = x[:, h * width:(h + 1) * width]
        outs.append(_rms(c, g) * scale)
    return jnp.concatenate(outs, axis=-1)


def _prep_kernel(are_ref, aim_ref, ldt_ref, bre_ref, bim_ref,
                 q1_ref, k1_ref, q2_ref, k2_ref,
                 lbre_ref, lbim_ref, bbre_ref, bbim_ref, lam_ref, *, lam0):
    a_re = are_ref[...]
    a_im = aim_ref[...]
    dt = jnp.exp(ldt_ref[...])
    mag = jnp.exp(a_re * dt)
    lb_re = mag * jnp.cos(a_im * dt)
    lb_im = mag * jnp.sin(a_im * dt)
    den = a_re * a_re + a_im * a_im
    n_re = lb_re - 1.0
    f_re = (n_re * a_re + lb_im * a_im) / den
    f_im = (lb_im * a_re - n_re * a_im) / den
    b_re = bre_ref[...]
    b_im = bim_ref[...]
    lbre_ref[...] = lb_re
    lbim_ref[...] = lb_im
    bbre_ref[...] = f_re * b_re - f_im * b_im
    bbim_ref[...] = f_re * b_im + f_im * b_re
    s1 = jnp.sum(q1_ref[...] * k1_ref[...], axis=-1, keepdims=True)
    s2 = jnp.sum(q2_ref[...] * k2_ref[...], axis=-1, keepdims=True)
    lam_ref[...] = jnp.exp(s1) - jnp.exp(s2) + lam0


def _prep(a_re, a_im, log_dt, b_re, b_im, q1, k1, q2, k2, lam0):
    are = a_re.reshape(1, N_STATE)
    aim = a_im.reshape(1, N_STATE)
    ldt = jnp.repeat(log_dt, SSM_STATE).reshape(1, N_STATE)
    bre = b_re.transpose(2, 0, 1).reshape(SSM_GROUP_CH, N_STATE)
    bim = b_im.transpose(2, 0, 1).reshape(SSM_GROUP_CH, N_STATE)
    vec = lambda v: v.reshape(1, ATTN_HEAD_DIM)
    row = jax.ShapeDtypeStruct((1, N_STATE), F32)
    mat = jax.ShapeDtypeStruct((SSM_GROUP_CH, N_STATE), F32)
    return pl.pallas_call(
        functools.partial(_prep_kernel, lam0=lam0),
        out_shape=(row, row, mat, mat, jax.ShapeDtypeStruct((1, 1), F32)),
        name="prep",
    )(are, aim, ldt, bre, bim, vec(q1), vec(k1), vec(q2), vec(k2))


def _in_proj_kernel(x_ref, g_ref, w_ref, wk_ref, seg_ref, qg_ref, kg_ref,
                    u_ref, q_ref, k_ref, v_ref, *bf_refs, key_major, blk):
    xn = _rms(x_ref[...], g_ref[...]).astype(BF16)
    proj = _dot(xn, w_ref[...])
    seg = seg_ref[...]

    def chunk_norm(z, g):
        ms = _dot((z * z).astype(BF16), seg) * (1.0 / ATTN_HEAD_DIM)
        return z * lax.rsqrt(ms + NORM_EPS) * g

    u_ref[...] = proj[:, :SSM_WIDTH]
    q = proj[:, SSM_WIDTH:SSM_WIDTH + ATTN_WIDTH]
    v = proj[:, SSM_WIDTH + ATTN_WIDTH:]
    q_ref[...] = (chunk_norm(q, qg_ref[...]) * (ATTN_HEAD_DIM ** -0.5 * LOG2E)).astype(q_ref.dtype)
    if key_major:
        kb_ref, vb_ref = bf_refs
        tm = x_ref.shape[0]
        z = _dot_nt(wk_ref[...], xn).reshape(ATTN_WIDTH // ATTN_HEAD_DIM, ATTN_HEAD_DIM, tm)
        ms = jnp.mean(z * z, axis=1, keepdims=True)
        kn = (z * lax.rsqrt(ms + NORM_EPS) * kg_ref[...]).reshape(ATTN_WIDTH, tm)
        k_ref[0] = kn
        for i in range(tm // blk):
            kb_ref[0, i] = kn[:, i * blk:(i + 1) * blk].astype(BF16)
        for h in range(ATTN_HEADS):
            v_ref[:, h, :] = v[:, ATTN_V_DIM * h:ATTN_V_DIM * (h + 1)]
        vb_ref[...] = v.astype(BF16)
    else:
        k_ref[...] = chunk_norm(_dot(xn, wk_ref[...]), kg_ref[...])
        v_ref[...] = v


def _in_proj(x, ln1_g, w_uqv, wk, seg, qg, kg, tm, q_dtype, n_batch=None, blk=None):
    n = x.shape[0]
    key_major = n_batch is not None
    bsz = n_batch if key_major else 1
    per_b = n // bsz // tm
    tok = lambda width: pl.BlockSpec((tm, width), lambda b, i: (b * per_b + i, 0))
    out_specs = [tok(SSM_WIDTH), tok(ATTN_WIDTH)]
    out_shape = [jax.ShapeDtypeStruct((n, SSM_WIDTH), F32),
                 jax.ShapeDtypeStruct((n, ATTN_WIDTH), q_dtype)]
    if key_major:
        t = n // bsz
        out_specs += [pl.BlockSpec((1, ATTN_WIDTH, tm), lambda b, i: (b, 0, i)),
                      pl.BlockSpec((tm, ATTN_HEADS, ATTN_V_DIM), lambda b, i: (b * per_b + i, 0, 0)),
                      pl.BlockSpec((1, tm // blk, ATTN_WIDTH, blk), lambda b, i: (b, i, 0, 0)),
                      tok(ATTN_WIDTH)]
        out_shape += [jax.ShapeDtypeStruct((bsz, ATTN_WIDTH, t), F32),
                      jax.ShapeDtypeStruct((n, ATTN_HEADS, ATTN_V_DIM), F32),
                      jax.ShapeDtypeStruct((bsz, t // blk, ATTN_WIDTH, blk), BF16),
                      jax.ShapeDtypeStruct((n, ATTN_WIDTH), BF16)]
    else:
        out_specs += [tok(ATTN_WIDTH), tok(ATTN_WIDTH)]
        out_shape += [jax.ShapeDtypeStruct((n, ATTN_WIDTH), F32)] * 2
    return pl.pallas_call(
        functools.partial(_in_proj_kernel, key_major=key_major, blk=blk),
        grid=(bsz, per_b),
        in_specs=[tok(D_MODEL), _const_spec((1, D_MODEL)), _const_spec(w_uqv.shape),
                  _const_spec(wk.shape), _const_spec((ATTN_WIDTH, ATTN_WIDTH)),
                  _const_spec((1, ATTN_WIDTH)), _const_spec(kg.shape)],
        out_specs=out_specs,
        out_shape=out_shape,
        compiler_params=_cparams(2),
        name="in_proj",
    )(x, ln1_g, w_uqv, wk, seg, qg, kg)


N_SLABS = N_STATE // 2 // LANES
HALF = N_STATE // 2


def _gelu_tanh(x):
    c = math.sqrt(2.0 / math.pi)
    return 0.5 * x * (1.0 + jnp.tanh(c * (x + 0.044715 * (x * x * x))))


def _ssm_kernel(u_ref, bre_ref, bim_ref, cre_ref, cim_ref, lre_ref, lim_ref,
                d_ref, glu_ref, h0re_ref, h0im_ref,
                out_ref, hre_ref, him_ref, sre, sim,
                *, n_seq, tc, pitch, n_bulk, rows_bulk, pitch_bulk):
    c = pl.program_id(0)
    rows = n_seq * tc
    n_groups = 2 * n_seq // SUBLANES

    @pl.when(c == 0)
    def _():
        hre_ref[...] = h0re_ref[...]
        him_ref[...] = h0im_ref[...]

    u = u_ref[...].reshape(rows, SSM_WIDTH)
    ub = u.astype(BF16)

    for j in range(4):
        hh, jj = divmod(j, 2)
        uj = ub[:, LANES * j:LANES * (j + 1)]
        for src, dst in ((bre_ref, sre), (bim_ref, sim)):
            bu = _dot(uj, src[j])
            for kk in range(4):
                slab = 4 * jj + kk
                for bb in range(n_bulk):
                    r0 = (hh * n_bulk + bb) * pitch_bulk
                    dst[slab, r0:r0 + rows_bulk, :] = (
                        bu[bb * rows_bulk:(bb + 1) * rows_bulk, LANES * kk:LANES * (kk + 1)])

    def group_body(g, carry):
        base = g * (SUBLANES * pitch)
        lr = [lre_ref[g, k] for k in range(N_SLABS)]
        li = [lim_ref[g, k] for k in range(N_SLABS)]
        hr0 = tuple(hre_ref[g, k] for k in range(N_SLABS))
        hi0 = tuple(him_ref[g, k] for k in range(N_SLABS))

        def step(t, hc):
            hr, hi = hc
            idx = pl.ds(base + t, SUBLANES, stride=pitch)
            nr, ni = [], []
            for k in range(N_SLABS):
                br = sre[k, idx, :]
                bi = sim[k, idx, :]
                r = lr[k] * hr[k] - li[k] * hi[k] + br
                i = lr[k] * hi[k] + li[k] * hr[k] + bi
                sre[k, idx, :] = r
                sim[k, idx, :] = i
                nr.append(r)
                ni.append(i)
            return tuple(nr), tuple(ni)

        hr, hi = lax.fori_loop(0, tc, step, (hr0, hi0), unroll=min(tc, 4))
        for k in range(N_SLABS):
            hre_ref[g, k] = hr[k]
            him_ref[g, k] = hi[k]
        return carry

    lax.fori_loop(0, n_groups, group_body, 0)

    y_rows = []
    for bb in range(n_bulk):
        y_cols = []
        for j in range(4):
            hh, jj = divmod(j, 2)
            r0 = (hh * n_bulk + bb) * pitch_bulk
            hr = jnp.concatenate(
                [sre[4 * jj + kk, r0:r0 + rows_bulk, :] for kk in range(4)], axis=1).astype(BF16)
            hi = jnp.concatenate(
                [sim[4 * jj + kk, r0:r0 + rows_bulk, :] for kk in range(4)], axis=1).astype(BF16)
            y_cols.append(_dot(hr, cre_ref[j]) - _dot(hi, cim_ref[j]))
        y_rows.append(jnp.concatenate(y_cols, axis=1))
    y = jnp.concatenate(y_rows, axis=0) if n_bulk > 1 else y_rows[0]
    y = y + d_ref[...] * u
    gl = _gelu_tanh(y)
    gate = jax.nn.sigmoid(_dot(gl.astype(BF16), glu_ref[...]))
    out_ref[...] = (gl * gate).astype(out_ref.dtype).reshape(out_ref.shape)


def _ssm(u3, mats, lam_g, d_row, glu_b, h0, n_seq, tc):
    bre, bim, cre, cim = mats
    lre_g, lim_g = lam_g
    h0re, h0im = h0
    n_groups = 2 * n_seq // SUBLANES
    t_total = u3.shape[0] * u3.shape[1] // n_seq
    n_chunks = t_total // tc
    if tc % SUBLANES == 0:
        pitch = tc + SUBLANES
        n_bulk, rows_bulk, pitch_bulk = n_seq, tc, pitch
        ublock = (n_seq, tc, SSM_WIDTH)
    else:
        assert n_chunks == 1
        pitch = tc
        n_bulk, rows_bulk, pitch_bulk = 1, n_seq * tc, n_seq * tc
        ublock = (1, n_seq * tc, SSM_WIDTH)
    scr_rows = 2 * n_bulk * pitch_bulk
    hshape = (n_groups, N_SLABS, SUBLANES, LANES)
    kern = functools.partial(_ssm_kernel, n_seq=n_seq, tc=tc, pitch=pitch,
                             n_bulk=n_bulk, rows_bulk=rows_bulk, pitch_bulk=pitch_bulk)
    return pl.pallas_call(
        kern,
        grid=(n_chunks,),
        in_specs=[pl.BlockSpec(ublock, lambda c: (0, c, 0)),
                  _const_spec(bre.shape), _const_spec(bim.shape),
                  _const_spec(cre.shape), _const_spec(cim.shape),
                  _const_spec(hshape), _const_spec(hshape),
                  _const_spec((1, SSM_WIDTH)), _const_spec((SSM_WIDTH, SSM_WIDTH)),
                  _const_spec(hshape), _const_spec(hshape)],
        out_specs=[pl.BlockSpec(ublock, lambda c: (0, c, 0)),
                   _const_spec(hshape), _const_spec(hshape)],
        out_shape=[jax.ShapeDtypeStruct(u3.shape, BF16),
                   jax.ShapeDtypeStruct(hshape, F32),
                   jax.ShapeDtypeStruct(hshape, F32)],
        scratch_shapes=[pltpu.VMEM((N_SLABS, scr_rows, LANES), F32),
                        pltpu.VMEM((N_SLABS, scr_rows, LANES), F32)],
        compiler_params=_cparams(1),
        name="ssm",
    )(u3, bre, bim, cre, cim, lre_g, lim_g, d_row, glu_b, h0re, h0im)


def _state_to_groups(h, n_seq):
    x = h.reshape(n_seq, 2, N_SLABS, LANES).transpose(1, 0, 2, 3)
    x = x.reshape(2 * n_seq // SUBLANES, SUBLANES, N_SLABS, LANES)
    return x.transpose(0, 2, 1, 3)


def _groups_to_state(x, n_seq):
    x = x.transpose(0, 2, 1, 3).reshape(2, n_seq, N_SLABS, LANES)
    return x.transpose(1, 0, 2, 3).reshape(n_seq, SSM_GROUPS, SSM_STATE)


def _head_slope(h):
    return jnp.where(h == 0, 2.0 ** -2, jnp.where(h == 1, 2.0 ** -4,
                     jnp.where(h == 2, 2.0 ** -6, 2.0 ** -8))).astype(F32)


def _attn_prompt_kernel(lam_ref, q_ref, k_ref, v_ref, sg_ref, o_ref,
                        q_scr, s_scr, m_scr, l_scr, acc_scr, *, bq, bk, out_scale):
    h = pl.program_id(1)
    qi = pl.program_id(2)
    slope = _head_slope(h) * LOG2E
    lam = lam_ref[0, 0]
    n_maps = 2

    q = q_ref[0]
    lane = lax.broadcasted_iota(jnp.int32, q.shape, 1)
    zero = jnp.zeros_like(q)
    q_scr[0] = jnp.where(lane < ATTN_HEAD_DIM, q, zero)
    q_scr[1] = jnp.where(lane >= ATTN_HEAD_DIM, q, zero)

    col = lax.broadcasted_iota(jnp.int32, (1, bk), 1)
    q0 = qi * bq

    def lane_tiles(x):
        return [x[:, LANES * i:LANES * (i + 1)] for i in range(bk // LANES)]

    n_full = q0 // bk

    def run_pairs(first, count, step):
        odd = lax.rem(count, 2)

        @pl.when(odd == 1)
        def _():
            step(first)

        def body(t, carry):
            j = first + odd + 2 * t
            step(j)
            step(j + 1)
            return carry

        lax.fori_loop(0, count // 2, body, 0)

    def scores(j, i, masked):
        kpos = col + j * bk
        s = _dot(q_scr[i], k_ref[0, j]) + slope * (kpos - q0).astype(F32)
        if masked:
            qpos = lax.broadcasted_iota(jnp.int32, (bq, bk), 0) + q0
            s = jnp.where(qpos >= kpos, s, -jnp.inf)
        return s

    def accumulate(j, i, p):
        start = pl.multiple_of(j * bk, bk)
        lpart = l_scr[i]
        for piece in lane_tiles(p):
            lpart = lpart + piece
        l_scr[i] = lpart
        acc_scr[i] += _dot(p.astype(BF16), v_ref[0, pl.ds(start, bk), :])

    def score_step(j, masked=False):
        for i in range(n_maps):
            s = scores(j, i, masked)
            s_scr[j, i] = s
            mpart = m_scr[i]
            for piece in lane_tiles(s):
                mpart = jnp.maximum(mpart, piece)
            m_scr[i] = mpart

    m_scr[...] = jnp.full(m_scr.shape, -jnp.inf, F32)
    score_step(n_full, masked=True)
    run_pairs(0, n_full, score_step)
    ms = [jnp.max(m_scr[i], axis=-1, keepdims=True) for i in range(n_maps)]

    def sum_step(j):
        for i in range(n_maps):
            accumulate(j, i, jnp.exp2(s_scr[j, i] - ms[i]))

    l_scr[...] = jnp.zeros(l_scr.shape, F32)
    acc_scr[...] = jnp.zeros(acc_scr.shape, F32)
    run_pairs(0, n_full + 1, sum_step)

    outs = [acc_scr[i] / jnp.sum(l_scr[i], axis=-1, keepdims=True) for i in range(n_maps)]
    o = outs[0] - lam * outs[1]
    o_ref[0] = (_rms(o, sg_ref[...]) * out_scale).astype(o_ref.dtype)


def _attn_sample_kernel(pt_ref, lam_ref, q_ref, kn_ref, vn_ref, sg_ref, *refs,
                        n_pages, t_new, past_len, out_scale):
    k_refs = refs[:n_pages]
    v_refs = refs[n_pages:2 * n_pages]
    o_ref = refs[2 * n_pages]
    del pt_ref
    lam = lam_ref[0, 0]
    n_rows = 2 * ATTN_HEADS * t_new

    q4 = q_ref[0]
    qe = jnp.concatenate([q4] * (2 * ATTN_HEADS), axis=0)
    row = lax.broadcasted_iota(jnp.int32, qe.shape, 0)
    lane = lax.broadcasted_iota(jnp.int32, qe.shape, 1)
    qe = jnp.where(lane // ATTN_HEAD_DIM == row // t_new, qe, 0.0)
    qeb = qe.astype(BF16)

    rcol = lax.broadcasted_iota(jnp.int32, (n_rows, 1), 0)
    head_c = rcol // (2 * t_new)
    tok_c = rcol % t_new
    slope = _head_slope(head_c) * LOG2E
    rows_h = 2 * t_new

    kt = jnp.concatenate([k_refs[p][0].astype(BF16) for p in range(n_pages)], axis=1)
    kpos = lax.broadcasted_iota(jnp.int32, (1, n_pages * PAGE_SIZE), 1) - past_len
    s = _dot(qeb, kt) + slope * kpos.astype(F32)

    kn = kn_ref[0]
    vn = vn_ref[0]
    s_new = []
    for c in range(t_new):
        sc = jnp.sum(qe * kn[c:c + 1, :], axis=-1, keepdims=True) + slope * float(c)
        s_new.append(jnp.where(tok_c >= c, sc, -jnp.inf))

    m = jnp.max(s, axis=-1, keepdims=True)
    for sc in s_new:
        m = jnp.maximum(m, sc)
    pe = jnp.exp2(s - m)
    l = jnp.sum(pe, axis=-1, keepdims=True)

    pv = []
    for h in range(ATTN_HEADS):
        vh = jnp.concatenate(
            [v_refs[p][0, pl.ds(h, PAGE_SIZE, stride=ATTN_HEADS), :].astype(BF16)
             for p in range(n_pages)], axis=0)
        pv.append(_dot(pe[rows_h * h:rows_h * (h + 1)].astype(BF16), vh))
    acc = jnp.concatenate(pv, axis=0)

    for c in range(t_new):
        pc = jnp.exp2(s_new[c] - m)
        l = l + pc
        vc = jnp.concatenate(
            [jnp.broadcast_to(vn[c:c + 1, ATTN_V_DIM * h:ATTN_V_DIM * (h + 1)], (rows_h, ATTN_V_DIM))
             for h in range(ATTN_HEADS)], axis=0)
        acc = acc + pc * vc

    acc = acc / l
    outs = []
    for h in range(ATTN_HEADS):
        a0 = acc[rows_h * h:rows_h * h + t_new]
        a1 = acc[rows_h * h + t_new:rows_h * (h + 1)]
        outs.append(_rms(a0 - lam * a1, sg_ref[...]) * out_scale)
    o_ref[0] = jnp.concatenate(outs, axis=-1)


def _attention_kernel(pt_ref, lam_ref, sg_ref, q_ref, k_ref, v_ref, qs_ref, kn_ref, vn_ref, *refs,
                      n_pages, t_new, past_len, bq, bk, out_scale):
    page_refs = refs[:2 * n_pages]
    o_ref, os_ref = refs[2 * n_pages:2 * n_pages + 2]
    scratch = refs[2 * n_pages + 2:]
    _attn_sample_kernel(pt_ref, lam_ref, qs_ref, kn_ref, vn_ref, sg_ref, *page_refs, os_ref,
                        n_pages=n_pages, t_new=t_new, past_len=past_len, out_scale=out_scale)
    _attn_prompt_kernel(lam_ref, q_ref, k_ref, v_ref, sg_ref, o_ref, *scratch,
                        bq=bq, bk=bk, out_scale=out_scale)


def _attention(page_table, lam, sg, qb, kb, vb, q3, kn3, vn3, cache_kt, cache_v3,
               bq, past_len, out_scale):
    bsz, t, _ = qb.shape
    bk = kb.shape[-1]
    nq = t // bq
    n_dec, t_new, _ = q3.shape
    n_pages = page_table.shape[1]
    assert n_dec == bsz * ATTN_HEADS * nq, "one sample sequence per prompt grid step"
    pt = page_table.reshape(-1)

    def seq(b, h, i):
        return (b * ATTN_HEADS + h) * nq + i

    tok_spec = pl.BlockSpec((1, t_new, ATTN_WIDTH), lambda b, h, i, pt: (seq(b, h, i), 0, 0))

    def k_spec(p):
        return pl.BlockSpec((1, ATTN_WIDTH, PAGE_SIZE),
                            lambda b, h, i, pt, p=p: (pt[seq(b, h, i) * n_pages + p], 0, 0))

    def v_spec(p):
        return pl.BlockSpec((1, PAGE_SIZE * ATTN_HEADS, ATTN_V_DIM),
                            lambda b, h, i, pt, p=p: (pt[seq(b, h, i) * n_pages + p], 0, 0))

    head_spec = pl.BlockSpec((1, bq, ATTN_V_DIM), lambda b, h, i, pt: (b, i, h))
    grid_spec = pltpu.PrefetchScalarGridSpec(
        num_scalar_prefetch=1,
        grid=(bsz, ATTN_HEADS, nq),
        in_specs=[pl.BlockSpec(memory_space=pltpu.SMEM), _const_spec((1, ATTN_V_DIM)),
                  head_spec,
                  pl.BlockSpec((1, t // bk, ATTN_V_DIM, bk), lambda b, h, i, pt: (b, 0, h, 0)),
                  pl.BlockSpec((1, t, ATTN_V_DIM), lambda b, h, i, pt: (b, 0, h)),
                  tok_spec, tok_spec, tok_spec]
                 + [k_spec(p) for p in range(n_pages)] + [v_spec(p) for p in range(n_pages)],
        out_specs=[head_spec, tok_spec],
        scratch_shapes=[pltpu.VMEM((2, bq, ATTN_V_DIM), BF16),
                        pltpu.VMEM((t // bk, 2, bq, bk), F32),
                        pltpu.VMEM((2, bq, LANES), F32),
                        pltpu.VMEM((2, bq, LANES), F32),
                        pltpu.VMEM((2, bq, ATTN_V_DIM), F32)])
    kern = functools.partial(_attention_kernel, n_pages=n_pages, t_new=t_new, past_len=past_len,
                             bq=bq, bk=bk, out_scale=out_scale)
    return pl.pallas_call(
        kern,
        grid_spec=grid_spec,
        out_shape=[jax.ShapeDtypeStruct(qb.shape, BF16), jax.ShapeDtypeStruct(q3.shape, F32)],
        compiler_params=_cparams(3),
        name="attention",
    )(pt, lam, sg, qb, kb, vb, q3, kn3, vn3, *([cache_kt] * n_pages), *([cache_v3] * n_pages))


def _memory_attention(cq, mk_ref, mv_ref):
    outs = []
    for h in range(CA_HEADS):
        sl = slice(CA_HEAD_DIM * h, CA_HEAD_DIM * (h + 1))
        s = _dot_nt(cq[:, sl], mk_ref[0, :, sl])
        p = jnp.exp(s - jnp.max(s, axis=-1, keepdims=True))
        l = jnp.sum(p, axis=-1, keepdims=True)
        outs.append(_dot((p / l).astype(BF16), mv_ref[0, :, sl]))
    return jnp.concatenate(outs, axis=-1)


def _out_proj_kernel(x_ref, s_ref, o_ref, w1_ref, w2_ref, g2_ref, wq_ref, qg_ref, *refs):
    x1 = (x_ref[...] + _dot(s_ref[...].astype(BF16), w1_ref[...])
          + _dot(o_ref[...].astype(BF16), w2_ref[...]))
    xn = _rms(x1, g2_ref[...]).astype(BF16)
    cq = _head_rms(_dot(xn, wq_ref[...]), qg_ref[...], CA_HEAD_DIM, CA_HEAD_DIM ** -0.5)
    if len(refs) == 4:
        mk_ref, mv_ref, x1_ref, out_ref = refs
        out_ref[...] = _memory_attention(cq.astype(BF16), mk_ref, mv_ref).astype(out_ref.dtype)
    else:
        x1_ref, out_ref = refs
        out_ref[...] = cq.astype(out_ref.dtype)
    x1_ref[...] = x1


def _out_proj(x, ssm_o, att_o, w1, w2, g2, wq, qg, tm, out_dtype, mem=None):
    n = x.shape[0]
    bsz = 1 if mem is None else mem[0].shape[0]
    per_b = n // bsz // tm
    tok = lambda width: pl.BlockSpec((tm, width), lambda b, i: (b * per_b + i, 0))
    mem_spec = pl.BlockSpec((1, N_MEM, D_MODEL), lambda b, i: (b, 0, 0))
    return pl.pallas_call(
        _out_proj_kernel,
        grid=(bsz, per_b),
        in_specs=[tok(D_MODEL), tok(SSM_WIDTH), tok(ATTN_WIDTH),
                  _const_spec((SSM_WIDTH, D_MODEL)), _const_spec((ATTN_WIDTH, D_MODEL)),
                  _const_spec((1, D_MODEL)), _const_spec((D_MODEL, D_MODEL)),
                  _const_spec((1, CA_HEAD_DIM))] + ([] if mem is None else [mem_spec, mem_spec]),
        out_specs=[tok(D_MODEL), tok(D_MODEL)],
        out_shape=[jax.ShapeDtypeStruct((n, D_MODEL), F32),
                   jax.ShapeDtypeStruct((n, D_MODEL), out_dtype)],
        compiler_params=_cparams(2),
        name="out_proj",
    )(x, ssm_o, att_o, w1, w2, g2, wq, qg, *(() if mem is None else mem))


def _mem_kv_kernel(m_ref, g_ref, wk_ref, wv_ref, kg_ref, mk_ref, mv_ref, mkb_ref, mvb_ref):
    mn = _rms(m_ref[...], g_ref[...]).astype(BF16)
    mk = _head_rms(_dot(mn, wk_ref[...]), kg_ref[...], CA_HEAD_DIM, 1.0)
    mv = _dot(mn, wv_ref[...])
    mk_ref[...] = mk
    mv_ref[...] = mv
    mkb_ref[...] = mk.astype(BF16)
    mvb_ref[...] = mv.astype(BF16)


def _mem_kv(mem, g, wk, wv, kg, tm):
    n = mem.shape[0]
    tok = pl.BlockSpec((tm, D_MODEL), lambda i: (i, 0))
    wspec = _const_spec((D_MODEL, D_MODEL))
    return pl.pallas_call(
        _mem_kv_kernel,
        grid=(n // tm,),
        in_specs=[tok, _const_spec((1, D_MODEL)), wspec, wspec, _const_spec((1, CA_HEAD_DIM))],
        out_specs=[tok, tok, tok, tok],
        out_shape=[jax.ShapeDtypeStruct((n, D_MODEL), F32)] * 2
                  + [jax.ShapeDtypeStruct((n, D_MODEL), BF16)] * 2,
        compiler_params=_cparams(1),
        name="mem_kv",
    )(mem, g, wk, wv, kg)


def _cross_sample_kernel(cq_ref, mk_ref, mv_ref, co_ref, *, t_new):
    halves = CA_HEAD_DIM // LANES
    per_key = halves * CA_HEADS

    def heads(ref, b):
        return jnp.concatenate(
            [ref[b, pl.ds(half * CA_HEADS + h, N_MEM, stride=per_key), :]
             for h in range(CA_HEADS) for half in range(halves)], axis=1).astype(BF16)

    for b in range(cq_ref.shape[0]):
        qe = jnp.concatenate([cq_ref[b]] * CA_HEADS, axis=0)
        row = lax.broadcasted_iota(jnp.int32, qe.shape, 0)
        lane = lax.broadcasted_iota(jnp.int32, qe.shape, 1)
        qe = jnp.where(lane // CA_HEAD_DIM == row // t_new, qe, 0.0).astype(BF16)
        s = _dot_nt(qe, heads(mk_ref, b))
        p = jnp.exp(s - jnp.max(s, axis=-1, keepdims=True))
        l = jnp.sum(p, axis=-1, keepdims=True)
        full = _dot((p / l).astype(BF16), heads(mv_ref, b))
        outs = [full[h * t_new:(h + 1) * t_new, CA_HEAD_DIM * h:CA_HEAD_DIM * (h + 1)]
                for h in range(CA_HEADS)]
        co_ref[b] = jnp.concatenate(outs, axis=-1)


def _mem_rows(mem):
    n = mem.shape[0]
    halves = CA_HEAD_DIM // LANES
    return (mem.reshape(n, N_MEM, CA_HEADS, halves, LANES).transpose(0, 1, 3, 2, 4)
            .reshape(n, N_MEM * halves * CA_HEADS, LANES))


def _cross_sample(cq3, mem_k, mem_v):
    n_dec, t_new, _ = cq3.shape
    per_step = 4
    tok = pl.BlockSpec((per_step, t_new, D_MODEL), lambda b: (b, 0, 0))
    mem = pl.BlockSpec((per_step,) + mem_k.shape[1:], lambda b: (b, 0, 0))
    return pl.pallas_call(
        functools.partial(_cross_sample_kernel, t_new=t_new),
        grid=(n_dec // per_step,),
        in_specs=[tok, mem, mem],
        out_specs=tok,
        out_shape=jax.ShapeDtypeStruct(cq3.shape, F32),
        compiler_params=_cparams(1),
        name="cross_sample",
    )(cq3, mem_k, mem_v)


def _ffn_kernel(x_ref, co_ref, wo_ref, g3_ref, wg_ref, wv_ref, cw_ref, cb_ref, wd_ref, prev_ref,
                y_ref, cs_ref, hbuf, *, tm, shift, pad):
    i = pl.program_id(1)

    @pl.when(i == 0)
    def _():
        hbuf[pad - 2 * shift:pad, :] = prev_ref[0]

    x2 = x_ref[...] + _dot(co_ref[...].astype(BF16), wo_ref[...])
    xn = _rms(x2, g3_ref[...]).astype(BF16)
    hbuf[pad:pad + tm, :] = _dot(xn, wg_ref[...])
    hv = _dot(xn, wv_ref[...])
    cw = cw_ref[...]
    conv = (cb_ref[...] + cw[0:1, :] * hbuf[pad - 2 * shift:pad - 2 * shift + tm, :]
            + cw[1:2, :] * hbuf[pad - shift:pad - shift + tm, :]
            + cw[2:3, :] * hbuf[pad:pad + tm, :])
    act = (conv * jax.nn.sigmoid(conv) * hv).astype(BF16)
    y_ref[...] = x2 + _dot(act, wd_ref[...])
    last = hbuf[pad + tm - 2 * shift:pad + tm, :]
    cs_ref[0] = last
    hbuf[pad - 2 * shift:pad, :] = last


def _ffn(x1, co, wo, g3, wg, wv, cw, cb, wd, prev, tm, shift):
    n = x1.shape[0]
    n_seq_blocks = prev.shape[0]
    per_b = n // n_seq_blocks // tm
    pad = -(-2 * shift // SUBLANES) * SUBLANES
    tok = pl.BlockSpec((tm, D_MODEL), lambda b, i: (b * per_b + i, 0))
    state = pl.BlockSpec((1, 2 * shift, FFN_HIDDEN), lambda b, i: (b, 0, 0))
    return pl.pallas_call(
        functools.partial(_ffn_kernel, tm=tm, shift=shift, pad=pad),
        grid=(n_seq_blocks, per_b),
        in_specs=[tok, tok, _const_spec((D_MODEL, D_MODEL)), _const_spec((1, D_MODEL)),
                  _const_spec((D_MODEL, FFN_HIDDEN)), _const_spec((D_MODEL, FFN_HIDDEN)),
                  _const_spec((CONV_WIDTH, FFN_HIDDEN)), _const_spec((1, FFN_HIDDEN)),
                  _const_spec((FFN_HIDDEN, D_MODEL)), state],
        out_specs=[tok, state],
        out_shape=[jax.ShapeDtypeStruct((n, D_MODEL), F32),
                   jax.ShapeDtypeStruct(prev.shape, F32)],
        scratch_shapes=[pltpu.VMEM((pad + tm, FFN_HIDDEN), F32)],
        compiler_params=_cparams(2),
        name="ffn",
    )(x1, co, wo, g3, wg, wv, cw, cb, wd, prev)


def _block_diag(blocks, n_outer):
    n, r, c = blocks.shape
    per = n // n_outer
    b = blocks.reshape(n_outer, per, r, c)
    eye = jnp.eye(per, dtype=blocks.dtype)
    full = b[:, :, :, None, :] * eye[None, :, None, :, None]
    return full.reshape(n_outer, per * r, per * c)


def _lam_groups(lb, n_seq):
    halves = lb.reshape(2, 1, N_SLABS, LANES)
    rows = jnp.broadcast_to(halves, (2, n_seq, N_SLABS, LANES))
    rows = rows.reshape(2 * n_seq // SUBLANES, SUBLANES, N_SLABS, LANES)
    return rows.transpose(0, 2, 1, 3)


def _layer(l, x_prompt, x_sample, mem_prompt, cache_k, cache_v, page_table,
           state_ssm_re, state_ssm_im, state_conv, cache_mem_k, cache_mem_v, p):
    bsz, seq, _ = x_prompt.shape
    n_dec, t_new, _ = x_sample.shape
    n_p = bsz * seq
    n_s = n_dec * t_new
    past_len = page_table.shape[1] * PAGE_SIZE
    lam0 = 0.8 - 0.6 * math.exp(-0.3 * l)
    out_scale = 1.0 - lam0
    row = lambda v: v.reshape(1, -1).astype(F32)
    bf = lambda w: w.astype(BF16)

    lb_re, lb_im, bb_re, bb_im, lam = _prep(
        p['ssm_a_re'], p['ssm_a_im'], p['ssm_log_dt'], p['ssm_b_re'], p['ssm_b_im'],
        p['lam_q1'], p['lam_k1'], p['lam_q2'], p['lam_k2'], lam0)

    def b_mats(bbt):
        blocks = bbt.reshape(SSM_GROUP_CH, SSM_GROUPS, SSM_STATE).transpose(1, 0, 2)
        return bf(_block_diag(blocks, 4))

    def c_mats(cm):
        return bf(_block_diag(cm.transpose(0, 2, 1), 4))

    mats = (b_mats(bb_re), b_mats(bb_im), c_mats(p['ssm_c_re']), c_mats(p['ssm_c_im']))
    d_row = row(p['ssm_d'])
    glu_b = bf(p['ssm_glu_w'])

    w_in = p['w_in']
    k0, k1 = SSM_WIDTH + ATTN_WIDTH, SSM_WIDTH + 2 * ATTN_WIDTH
    w_uqv = bf(jnp.concatenate([w_in[:, :k0], w_in[:, k1:]], axis=1))
    w_k = bf(w_in[:, k0:k1])
    seg = jnp.kron(jnp.eye(ATTN_WIDTH // ATTN_HEAD_DIM, dtype=F32),
                   jnp.ones((ATTN_HEAD_DIM, ATTN_HEAD_DIM), F32)).astype(BF16)
    qg = jnp.tile(p['q_norm_g'], ATTN_WIDTH // ATTN_HEAD_DIM).reshape(1, ATTN_WIDTH)
    kg = jnp.tile(p['k_norm_g'], ATTN_WIDTH // ATTN_HEAD_DIM).reshape(1, ATTN_WIDTH)
    sg = row(p['subln_g'])
    w_out_b = bf(p['w_out'])
    w1, w2 = w_out_b[:SSM_WIDTH], w_out_b[SSM_WIDTH:]
    wq_b, wk_b, wv_b, wo_b = bf(p['ca_wq']), bf(p['ca_wk']), bf(p['ca_wv']), bf(p['ca_wo'])
    wg_b, wvf_b, wd_b = bf(p['ffn_wg']), bf(p['ffn_wv']), bf(p['ffn_wd'])
    cw = p['ffn_conv_w'].astype(F32)
    cb = row(p['ffn_conv_b'])

    xp = x_prompt.reshape(n_p, D_MODEL)
    attn_bq, attn_bk = 512, 512
    kg_col = jnp.broadcast_to(p['k_norm_g'].reshape(ATTN_HEAD_DIM, 1), (ATTN_HEAD_DIM, 512))
    u_p, q_p, kt_p, v_p, kb_p, vb_p = _in_proj(xp, row(p['ln1_g']), w_uqv, w_k.T, seg, qg, kg_col,
                                               512, BF16, n_batch=bsz, blk=attn_bk)
    zeros_h = jnp.zeros((2 * bsz // SUBLANES, N_SLABS, SUBLANES, LANES), F32)
    ssm_p, hre_p, him_p = _ssm(u_p.reshape(bsz, seq, SSM_WIDTH), mats,
                               (_lam_groups(lb_re, bsz), _lam_groups(lb_im, bsz)),
                               d_row, glu_b, (zeros_h, zeros_h), bsz, 256)

    xs = x_sample.reshape(n_s, D_MODEL)
    u_s, q_s, k_s, v_s = _in_proj(xs, row(p['ln1_g']), w_uqv, w_k, seg, qg, kg, n_s, F32)
    h0 = (_state_to_groups(state_ssm_re.reshape(n_dec, N_STATE), n_dec),
          _state_to_groups(state_ssm_im.reshape(n_dec, N_STATE), n_dec))
    ssm_s, hre_s, him_s = _ssm(u_s.reshape(1, n_s, SSM_WIDTH), mats,
                               (_lam_groups(lb_re, n_dec), _lam_groups(lb_im, n_dec)),
                               d_row, glu_b, h0, n_dec, t_new)

    n_phys = cache_k.shape[0]
    cache_kt = cache_k.transpose(0, 2, 3, 4, 1).reshape(n_phys, ATTN_WIDTH, PAGE_SIZE)
    tok3 = lambda a: a.reshape(n_dec, t_new, ATTN_WIDTH)
    att_p, att_s = _attention(page_table, lam, sg, q_p.reshape(bsz, seq, ATTN_WIDTH), kb_p,
                              vb_p.reshape(bsz, seq, ATTN_WIDTH), tok3(q_s), tok3(k_s), tok3(v_s),
                              cache_kt, cache_v.reshape(n_phys, PAGE_SIZE * ATTN_HEADS, ATTN_V_DIM),
                              attn_bq, past_len, out_scale)

    mk, mv, mkb, mvb = _mem_kv(mem_prompt.reshape(bsz * N_MEM, D_MODEL), row(p['mem_norm_g']),
                               wk_b, wv_b, row(p['ca_k_norm_g']), 256)
    x1_p, co_p = _out_proj(xp, ssm_p.reshape(n_p, SSM_WIDTH), att_p.reshape(n_p, ATTN_WIDTH),
                           w1, w2, row(p['ln2_g']), wq_b, row(p['ca_q_norm_g']), 512, BF16,
                           mem=(mkb.reshape(bsz, N_MEM, D_MODEL), mvb.reshape(bsz, N_MEM, D_MODEL)))
    y_p, conv_p = _ffn(x1_p, co_p, wo_b, row(p['ln3_g']), wg_b, wvf_b, cw, cb, wd_b,
                       jnp.zeros((bsz, CONV_WIDTH - 1, FFN_HIDDEN), F32), 512, 1)

    x1_s, cq_s = _out_proj(xs, ssm_s.reshape(n_s, SSM_WIDTH), att_s.reshape(n_s, ATTN_WIDTH),
                           w1, w2, row(p['ln2_g']), wq_b, row(p['ca_q_norm_g']), n_s, F32)
    co_s = _cross_sample(cq_s.reshape(n_dec, t_new, D_MODEL),
                         _mem_rows(cache_mem_k), _mem_rows(cache_mem_v))
    tmaj = lambda a: a.reshape(n_dec, t_new, -1).transpose(1, 0, 2).reshape(n_s, -1)
    prev_s = state_conv.transpose(1, 0, 2).reshape(1, (CONV_WIDTH - 1) * n_dec, FFN_HIDDEN)
    y_s, conv_s = _ffn(tmaj(x1_s), tmaj(co_s), wo_b, row(p['ln3_g']), wg_b, wvf_b, cw, cb, wd_b,
                       prev_s, n_s, n_dec)
    y_s = y_s.reshape(t_new, n_dec, D_MODEL).transpose(1, 0, 2)
    conv_s = conv_s.reshape(CONV_WIDTH - 1, n_dec, FFN_HIDDEN).transpose(1, 0, 2)

    return (y_p.reshape(bsz, seq, D_MODEL), y_s,
            kt_p.reshape(bsz, ATTN_HEADS, 2, ATTN_HEAD_DIM, seq).transpose(0, 4, 1, 2, 3),
            v_p.reshape(bsz, seq, ATTN_HEADS, ATTN_V_DIM),
            k_s.reshape(n_dec, t_new, ATTN_HEADS, 2, ATTN_HEAD_DIM),
            v_s.reshape(n_dec, t_new, ATTN_HEADS, ATTN_V_DIM),
            _groups_to_state(hre_p, bsz), _groups_to_state(him_p, bsz),
            _groups_to_state(hre_s, n_dec), _groups_to_state(him_s, n_dec),
            conv_p, conv_s,
            mk.reshape(bsz, N_MEM, CA_HEADS, CA_HEAD_DIM), mv.reshape(bsz, N_MEM, CA_HEADS, CA_HEAD_DIM))


def kernel(x_prompt, x_sample, mem_prompt, cache_k, cache_v, page_table, state_ssm_re, state_ssm_im, state_conv, cache_mem_k, cache_mem_v, ln1_g, w_in, ssm_a_re, ssm_a_im, ssm_b_re, ssm_b_im, ssm_c_re, ssm_c_im, ssm_d, ssm_log_dt, ssm_glu_w, q_norm_g, k_norm_g, lam_q1, lam_k1, lam_q2, lam_k2, subln_g, w_out, ln2_g, mem_norm_g, ca_wq, ca_wk, ca_wv, ca_q_norm_g, ca_k_norm_g, ca_wo, ln3_g, ffn_wg, ffn_wv, ffn_conv_w, ffn_conv_b, ffn_wd):
    params = dict(
        ln1_g=ln1_g, w_in=w_in, ssm_a_re=ssm_a_re, ssm_a_im=ssm_a_im, ssm_b_re=ssm_b_re,
        ssm_b_im=ssm_b_im, ssm_c_re=ssm_c_re, ssm_c_im=ssm_c_im, ssm_d=ssm_d,
        ssm_log_dt=ssm_log_dt, ssm_glu_w=ssm_glu_w, q_norm_g=q_norm_g, k_norm_g=k_norm_g,
        lam_q1=lam_q1, lam_k1=lam_k1, lam_q2=lam_q2, lam_k2=lam_k2, subln_g=subln_g,
        w_out=w_out, ln2_g=ln2_g, mem_norm_g=mem_norm_g, ca_wq=ca_wq, ca_wk=ca_wk, ca_wv=ca_wv,
        ca_q_norm_g=ca_q_norm_g, ca_k_norm_g=ca_k_norm_g, ca_wo=ca_wo, ln3_g=ln3_g,
        ffn_wg=ffn_wg, ffn_wv=ffn_wv, ffn_conv_w=ffn_conv_w, ffn_conv_b=ffn_conv_b, ffn_wd=ffn_wd)
    y_p, y_s = x_prompt, x_sample
    per_layer = []
    for l in range(w_in.shape[0]):
        p = {k: v[l] for k, v in params.items()}
        outs = _layer(l, y_p, y_s, mem_prompt, cache_k[l], cache_v[l], page_table,
                      state_ssm_re[l], state_ssm_im[l], state_conv[l],
                      cache_mem_k[l], cache_mem_v[l], p)
        y_p, y_s = outs[0], outs[1]
        per_layer.append(outs[2:])
    return (y_p, y_s) + tuple(jnp.stack(leaf) for leaf in zip(*per_layer))
```

```python
import functools
import math

import jax
import jax.numpy as jnp
from jax import lax
from jax.experimental import pallas as pl
from jax.experimental.pallas import tpu as pltpu

F32 = jnp.float32
BF16 = jnp.bfloat16

D_MODEL = 1024
SSM_WIDTH = 512
SSM_GROUP_CH = 16
SSM_GROUPS = 32
SSM_STATE = 64
N_STATE = SSM_GROUPS * SSM_STATE
ATTN_WIDTH = 512
ATTN_HEAD_DIM = 64
ATTN_V_DIM = 128
ATTN_HEADS = 4
PAGE_SIZE = 128
N_MEM = 256
CA_HEADS = 4
CA_HEAD_DIM = 256
FFN_HIDDEN = 2816
CONV_WIDTH = 3
NORM_EPS = 1e-6

LANES = 128
SUBLANES = 8
VMEM_LIMIT = 56 * 1024 * 1024
VMEM_LIMIT_RIDER = 59 * 1024 * 1024

NEG_BIG = -1e30
LOG2E = math.log2(math.e)

def _cparams(n_axes, vmem_limit=VMEM_LIMIT):
    return pltpu.CompilerParams(
        dimension_semantics=("arbitrary",) * n_axes,
        vmem_limit_bytes=vmem_limit)


def _const_spec(shape):
    nd = len(shape)
    return pl.BlockSpec(shape, lambda *_: (0,) * nd, pipeline_mode=pl.Buffered(1))


def _rms(x, g):
    return x * lax.rsqrt(jnp.mean(x * x, axis=-1, keepdims=True) + NORM_EPS) * g


def _dot(a, b):
    return jnp.dot(a, b, preferred_element_type=F32)


def _dot_nt(a, b):
    return lax.dot_general(a, b, (((1,), (1,)), ((), ())), preferred_element_type=F32)


def _head_rms(x, g, width, scale):
    outs = []
    for h in range(x.shape[-1] // width):
        c = x[:, h * width:(h + 1) * width]
        outs.append(_rms(c, g) * scale)
    return jnp.concatenate(outs, axis=-1)


def _prep_kernel(are_ref, aim_ref, ldt_ref, bre_ref, bim_ref,
                 q1_ref, k1_ref, q2_ref, k2_ref,
                 lbre_ref, lbim_ref, bbre_ref, bbim_ref, lam_ref, *, lam0):
    a_re = are_ref[...]
    a_im = aim_ref[...]
    dt = jnp.exp(ldt_ref[...])
    mag = jnp.exp(a_re * dt)
    lb_re = mag * jnp.cos(a_im * dt)
    lb_im = mag * jnp.sin(a_im * dt)
    den = a_re * a_re + a_im * a_im
    n_re = lb_re - 1.0
    f_re = (n_re * a_re + lb_im * a_im) / den
    f_im = (lb_im * a_re - n_re * a_im) / den
    b_re = bre_ref[...]
    b_im = bim_ref[...]
    lbre_ref[...] = lb_re
    lbim_ref[...] = lb_im
    bbre_ref[...] = f_re * b_re - f_im * b_im
    bbim_ref[...] = f_re * b_im + f_im * b_re
    s1 = jnp.sum(q1_ref[...] * k1_ref[...], axis=-1, keepdims=True)
    s2 = jnp.sum(q2_ref[...] * k2_ref[...], axis=-1, keepdims=True)
    lam_ref[...] = jnp.exp(s1) - jnp.exp(s2) + lam0


def _prep(a_re, a_im, log_dt, b_re, b_im, q1, k1, q2, k2, lam0):
    are = a_re.reshape(1, N_STATE)
    aim = a_im.reshape(1, N_STATE)
    ldt = jnp.repeat(log_dt, SSM_STATE).reshape(1, N_STATE)
    bre = b_re.transpose(2, 0, 1).reshape(SSM_GROUP_CH, N_STATE)
    bim = b_im.transpose(2, 0, 1).reshape(SSM_GROUP_CH, N_STATE)
    vec = lambda v: v.reshape(1, ATTN_HEAD_DIM)
    row = jax.ShapeDtypeStruct((1, N_STATE), F32)
    mat = jax.ShapeDtypeStruct((SSM_GROUP_CH, N_STATE), F32)
    return pl.pallas_call(
        functools.partial(_prep_kernel, lam0=lam0),
        out_shape=(row, row, mat, mat, jax.ShapeDtypeStruct((1, 1), F32)),
        name="prep",
    )(are, aim, ldt, bre, bim, vec(q1), vec(k1), vec(q2), vec(k2))


def _in_proj_kernel(x_ref, g_ref, w_ref, wk_ref, seg_ref, qg_ref, kg_ref,
                    u_ref, q_ref, k_ref, v_ref, *bf_refs, key_major, blk):
    xn = _rms(x_ref[...], g_ref[...]).astype(BF16)
    proj = _dot(xn, w_ref[...])
    seg = seg_ref[...]

    def chunk_norm(z, g):
        ms = _dot((z * z).astype(BF16), seg) * (1.0 / ATTN_HEAD_DIM)
        return z * lax.rsqrt(ms + NORM_EPS) * g

    u_ref[...] = proj[:, :SSM_WIDTH]
    q = proj[:, SSM_WIDTH:SSM_WIDTH + ATTN_WIDTH]
    v = proj[:, SSM_WIDTH + ATTN_WIDTH:]
    q_ref[...] = (chunk_norm(q, qg_ref[...]) * (ATTN_HEAD_DIM ** -0.5 * LOG2E)).astype(q_ref.dtype)
    if key_major:
        kb_ref, vb_ref = bf_refs
        tm = x_ref.shape[0]
        z = _dot_nt(wk_ref[...], xn).reshape(ATTN_WIDTH // ATTN_HEAD_DIM, ATTN_HEAD_DIM, tm)
        ms = jnp.mean(z * z, axis=1, keepdims=True)
        kn = (z * lax.rsqrt(ms + NORM_EPS) * kg_ref[...]).reshape(ATTN_WIDTH, tm)
        k_ref[0] = kn
        for i in range(tm // blk):
            kb_ref[0, i] = kn[:, i * blk:(i + 1) * blk].astype(BF16)
        for h in range(ATTN_HEADS):
            v_ref[:, h, :] = v[:, ATTN_V_DIM * h:ATTN_V_DIM * (h + 1)]
        vb_ref[...] = v.astype(BF16)
    else:
        k_ref[...] = chunk_norm(_dot(xn, wk_ref[...]), kg_ref[...])
        v_ref[...] = v


def _in_proj(x, ln1_g, w_uqv, wk, seg, qg, kg, tm, q_dtype, n_batch=None, blk=None):
    n = x.shape[0]
    key_major = n_batch is not None
    bsz = n_batch if key_major else 1
    per_b = n // bsz // tm
    tok = lambda width: pl.BlockSpec((tm, width), lambda b, i: (b * per_b + i, 0))
    out_specs = [tok(SSM_WIDTH), tok(ATTN_WIDTH)]
    out_shape = [jax.ShapeDtypeStruct((n, SSM_WIDTH), F32),
                 jax.ShapeDtypeStruct((n, ATTN_WIDTH), q_dtype)]
    if key_major:
        t = n // bsz
        out_specs += [pl.BlockSpec((1, ATTN_WIDTH, tm), lambda b, i: (b, 0, i)),
                      pl.BlockSpec((tm, ATTN_HEADS, ATTN_V_DIM), lambda b, i: (b * per_b + i, 0, 0)),
                      pl.BlockSpec((1, tm // blk, ATTN_WIDTH, blk), lambda b, i: (b, i, 0, 0)),
                      tok(ATTN_WIDTH)]
        out_shape += [jax.ShapeDtypeStruct((bsz, ATTN_WIDTH, t), F32),
                      jax.ShapeDtypeStruct((n, ATTN_HEADS, ATTN_V_DIM), F32),
                      jax.ShapeDtypeStruct((bsz, t // blk, ATTN_WIDTH, blk), BF16),
                      jax.ShapeDtypeStruct((n, ATTN_WIDTH), BF16)]
    else:
        out_specs += [tok(ATTN_WIDTH), tok(ATTN_WIDTH)]
        out_shape += [jax.ShapeDtypeStruct((n, ATTN_WIDTH), F32)] * 2
    return pl.pallas_call(
        functools.partial(_in_proj_kernel, key_major=key_major, blk=blk),
        grid=(bsz, per_b),
        in_specs=[tok(D_MODEL), _const_spec((1, D_MODEL)), _const_spec(w_uqv.shape),
                  _const_spec(wk.shape), _const_spec((ATTN_WIDTH, ATTN_WIDTH)),
                  _const_spec((1, ATTN_WIDTH)), _const_spec(kg.shape)],
        out_specs=out_specs,
        out_shape=out_shape,
        compiler_params=_cparams(2),
        name="in_proj",
    )(x, ln1_g, w_uqv, wk, seg, qg, kg)


N_SLABS = N_STATE // 2 // LANES
HALF = N_STATE // 2


def _gelu_tanh(x):
    c = math.sqrt(2.0 / math.pi)
    return 0.5 * x * (1.0 + jnp.tanh(c * (x + 0.044715 * (x * x * x))))


def _ssm_kernel(u_ref, bre_ref, bim_ref, cre_ref, cim_ref, lre_ref, lim_ref,
                d_ref, glu_ref, h0re_ref, h0im_ref,
                out_ref, hre_ref, him_ref, sre, sim,
                *, n_seq, tc, pitch, n_bulk, rows_bulk, pitch_bulk):
    c = pl.program_id(0)
    rows = n_seq * tc
    n_groups = 2 * n_seq // SUBLANES

    @pl.when(c == 0)
    def _():
        hre_ref[...] = h0re_ref[...]
        him_ref[...] = h0im_ref[...]

    u = u_ref[...].reshape(rows, SSM_WIDTH)
    ub = u.astype(BF16)

    for j in range(4):
        hh, jj = divmod(j, 2)
        uj = ub[:, LANES * j:LANES * (j + 1)]
        for src, dst in ((bre_ref, sre), (bim_ref, sim)):
            bu = _dot(uj, src[j])
            for kk in range(4):
                slab = 4 * jj + kk
                for bb in range(n_bulk):
                    r0 = (hh * n_bulk + bb) * pitch_bulk
                    dst[slab, r0:r0 + rows_bulk, :] = (
                        bu[bb * rows_bulk:(bb + 1) * rows_bulk, LANES * kk:LANES * (kk + 1)])

    def group_body(g, carry):
        base = g * (SUBLANES * pitch)
        lr = [lre_ref[g, k] for k in range(N_SLABS)]
        li = [lim_ref[g, k] for k in range(N_SLABS)]
        hr0 = tuple(hre_ref[g, k] for k in range(N_SLABS))
        hi0 = tuple(him_ref[g, k] for k in range(N_SLABS))

        def step(t, hc):
            hr, hi = hc
            idx = pl.ds(base + t, SUBLANES, stride=pitch)
            nr, ni = [], []
            for k in range(N_SLABS):
                br = sre[k, idx, :]
                bi = sim[k, idx, :]
                r = lr[k] * hr[k] - li[k] * hi[k] + br
                i = lr[k] * hi[k] + li[k] * hr[k] + bi
                sre[k, idx, :] = r
                sim[k, idx, :] = i
                nr.append(r)
                ni.append(i)
            return tuple(nr), tuple(ni)

        hr, hi = lax.fori_loop(0, tc, step, (hr0, hi0), unroll=min(tc, 4))
        for k in range(N_SLABS):
            hre_ref[g, k] = hr[k]
            him_ref[g, k] = hi[k]
        return carry

    lax.fori_loop(0, n_groups, group_body, 0)

    y_rows = []
    for bb in range(n_bulk):
        y_cols = []
        for j in range(4):
            hh, jj = divmod(j, 2)
            r0 = (hh * n_bulk + bb) * pitch_bulk
            hr = jnp.concatenate(
                [sre[4 * jj + kk, r0:r0 + rows_bulk, :] for kk in range(4)], axis=1).astype(BF16)
            hi = jnp.concatenate(
                [sim[4 * jj + kk, r0:r0 + rows_bulk, :] for kk in range(4)], axis=1).astype(BF16)
            y_cols.append(_dot(hr, cre_ref[j]) - _dot(hi, cim_ref[j]))
        y_rows.append(jnp.concatenate(y_cols, axis=1))
    y = jnp.concatenate(y_rows, axis=0) if n_bulk > 1 else y_rows[0]
    y = y + d_ref[...] * u
    gl = _gelu_tanh(y)
    gate = jax.nn.sigmoid(_dot(gl.astype(BF16), glu_ref[...]))
    out_ref[...] = (gl * gate).astype(out_ref.dtype).reshape(out_ref.shape)


def _ssm(u3, mats, lam_g, d_row, glu_b, h0, n_seq, tc):
    bre, bim, cre, cim = mats
    lre_g, lim_g = lam_g
    h0re, h0im = h0
    n_groups = 2 * n_seq // SUBLANES
    t_total = u3.shape[0] * u3.shape[1] // n_seq
    n_chunks = t_total // tc
    if tc % SUBLANES == 0:
        pitch = tc + SUBLANES
        n_bulk, rows_bulk, pitch_bulk = n_seq, tc, pitch
        ublock = (n_seq, tc, SSM_WIDTH)
    else:
        assert n_chunks == 1
        pitch = tc
        n_bulk, rows_bulk, pitch_bulk = 1, n_seq * tc, n_seq * tc
        ublock = (1, n_seq * tc, SSM_WIDTH)
    scr_rows = 2 * n_bulk * pitch_bulk
    hshape = (n_groups, N_SLABS, SUBLANES, LANES)
    kern = functools.partial(_ssm_kernel, n_seq=n_seq, tc=tc, pitch=pitch,
                             n_bulk=n_bulk, rows_bulk=rows_bulk, pitch_bulk=pitch_bulk)
    return pl.pallas_call(
        kern,
        grid=(n_chunks,),
        in_specs=[pl.BlockSpec(ublock, lambda c: (0, c, 0)),
                  _const_spec(bre.shape), _const_spec(bim.shape),
                  _const_spec(cre.shape), _const_spec(cim.shape),
                  _const_spec(hshape), _const_spec(hshape),
                  _const_spec((1, SSM_WIDTH)), _const_spec((SSM_WIDTH, SSM_WIDTH)),
                  _const_spec(hshape), _const_spec(hshape)],
        out_specs=[pl.BlockSpec(ublock, lambda c: (0, c, 0)),
                   _const_spec(hshape), _const_spec(hshape)],
        out_shape=[jax.ShapeDtypeStruct(u3.shape, BF16),
                   jax.ShapeDtypeStruct(hshape, F32),
                   jax.ShapeDtypeStruct(hshape, F32)],
        scratch_shapes=[pltpu.VMEM((N_SLABS, scr_rows, LANES), F32),
                        pltpu.VMEM((N_SLABS, scr_rows, LANES), F32)],
        compiler_params=_cparams(1),
        name="ssm",
    )(u3, bre, bim, cre, cim, lre_g, lim_g, d_row, glu_b, h0re, h0im)


def _state_to_groups(h, n_seq):
    x = h.reshape(n_seq, 2, N_SLABS, LANES).transpose(1, 0, 2, 3)
    x = x.reshape(2 * n_seq // SUBLANES, SUBLANES, N_SLABS, LANES)
    return x.transpose(0, 2, 1, 3)


def _groups_to_state(x, n_seq):
    x = x.transpose(0, 2, 1, 3).reshape(2, n_seq, N_SLABS, LANES)
    return x.transpose(1, 0, 2, 3).reshape(n_seq, SSM_GROUPS, SSM_STATE)


def _head_slope(h):
    return jnp.where(h == 0, 2.0 ** -2, jnp.where(h == 1, 2.0 ** -4,
                     jnp.where(h == 2, 2.0 ** -6, 2.0 ** -8))).astype(F32)


def _attn_prompt_kernel(lam_ref, q_ref, k_ref, v_ref, sg_ref, o_ref,
                        q_scr, s_scr, m_scr, l_scr, acc_scr, *, bq, bk, out_scale):
    h = pl.program_id(1)
    qi = pl.program_id(2)
    slope = _head_slope(h) * LOG2E
    lam = lam_ref[0, 0]
    n_maps = 2

    q = q_ref[0]
    lane = lax.broadcasted_iota(jnp.int32, q.shape, 1)
    zero = jnp.zeros_like(q)
    q_scr[0] = jnp.where(lane < ATTN_HEAD_DIM, q, zero)
    q_scr[1] = jnp.where(lane >= ATTN_HEAD_DIM, q, zero)

    col = lax.broadcasted_iota(jnp.int32, (1, bk), 1)
    q0 = qi * bq

    def lane_tiles(x):
        return [x[:, LANES * i:LANES * (i + 1)] for i in range(bk // LANES)]

    n_full = q0 // bk

    def run_pairs(first, count, step):
        odd = lax.rem(count, 2)

        @pl.when(odd == 1)
        def _():
            step(first)

        def body(t, carry):
            j = first + odd + 2 * t
            step(j)
            step(j + 1)
            return carry

        lax.fori_loop(0, count // 2, body, 0)

    def scores(j, i, masked):
        kpos = col + j * bk
        s = _dot(q_scr[i], k_ref[0, j]) + slope * (kpos - q0).astype(F32)
        if masked:
            qpos = lax.broadcasted_iota(jnp.int32, (bq, bk), 0) + q0
            s = jnp.where(qpos >= kpos, s, -jnp.inf)
        return s

    def accumulate(j, i, p):
        start = pl.multiple_of(j * bk, bk)
        lpart = l_scr[i]
        for piece in lane_tiles(p):
            lpart = lpart + piece
        l_scr[i] = lpart
        acc_scr[i] += _dot(p.astype(BF16), v_ref[0, pl.ds(start, bk), :])

    def score_step(j, masked=False):
        for i in range(n_maps):
            s = scores(j, i, masked)
            s_scr[j, i] = s
            mpart = m_scr[i]
            for piece in lane_tiles(s):
                mpart = jnp.maximum(mpart, piece)
            m_scr[i] = mpart

    m_scr[...] = jnp.full(m_scr.shape, -jnp.inf, F32)
    score_step(n_full, masked=True)
    run_pairs(0, n_full, score_step)
    ms = [jnp.max(m_scr[i], axis=-1, keepdims=True) for i in range(n_maps)]

    def sum_step(j):
        for i in range(n_maps):
            accumulate(j, i, jnp.exp2(s_scr[j, i] - ms[i]))

    l_scr[...] = jnp.zeros(l_scr.shape, F32)
    acc_scr[...] = jnp.zeros(acc_scr.shape, F32)
    run_pairs(0, n_full + 1, sum_step)

    outs = [acc_scr[i] / jnp.sum(l_scr[i], axis=-1, keepdims=True) for i in range(n_maps)]
    o = outs[0] - lam * outs[1]
    o_ref[0] = (_rms(o, sg_ref[...]) * out_scale).astype(o_ref.dtype)


def _attn_sample_kernel(pt_ref, lam_ref, q_ref, kn_ref, vn_ref, sg_ref, *refs,
                        n_pages, t_new, past_len, out_scale):
    k_refs = refs[:n_pages]
    v_refs = refs[n_pages:2 * n_pages]
    o_ref = refs[2 * n_pages]
    del pt_ref
    lam = lam_ref[0, 0]
    n_rows = 2 * ATTN_HEADS * t_new

    q4 = q_ref[0]
    qe = jnp.concatenate([q4] * (2 * ATTN_HEADS), axis=0)
    row = lax.broadcasted_iota(jnp.int32, qe.shape, 0)
    lane = lax.broadcasted_iota(jnp.int32, qe.shape, 1)
    qe = jnp.where(lane // ATTN_HEAD_DIM == row // t_new, qe, 0.0)
    qeb = qe.astype(BF16)

    rcol = lax.broadcasted_iota(jnp.int32, (n_rows, 1), 0)
    head_c = rcol // (2 * t_new)
    tok_c = rcol % t_new
    slope = _head_slope(head_c) * LOG2E
    rows_h = 2 * t_new

    kt = jnp.concatenate([k_refs[p][0].astype(BF16) for p in range(n_pages)], axis=1)
    kpos = lax.broadcasted_iota(jnp.int32, (1, n_pages * PAGE_SIZE), 1) - past_len
    s = _dot(qeb, kt) + slope * kpos.astype(F32)

    kn = kn_ref[0]
    vn = vn_ref[0]
    s_new = []
    for c in range(t_new):
        sc = jnp.sum(qe * kn[c:c + 1, :], axis=-1, keepdims=True) + slope * float(c)
        s_new.append(jnp.where(tok_c >= c, sc, -jnp.inf))

    m = jnp.max(s, axis=-1, keepdims=True)
    for sc in s_new:
        m = jnp.maximum(m, sc)
    pe = jnp.exp2(s - m)
    l = jnp.sum(pe, axis=-1, keepdims=True)

    pv = []
    for h in range(ATTN_HEADS):
        vh = jnp.concatenate(
            [v_refs[p][0, pl.ds(h, PAGE_SIZE, stride=ATTN_HEADS), :].astype(BF16)
             for p in range(n_pages)], axis=0)
        pv.append(_dot(pe[rows_h * h:rows_h * (h + 1)].astype(BF16), vh))
    acc = jnp.concatenate(pv, axis=0)

    for c in range(t_new):
        pc = jnp.exp2(s_new[c] - m)
        l = l + pc
        vc = jnp.concatenate(
            [jnp.broadcast_to(vn[c:c + 1, ATTN_V_DIM * h:ATTN_V_DIM * (h + 1)], (rows_h, ATTN_V_DIM))
             for h in range(ATTN_HEADS)], axis=0)
        acc = acc + pc * vc

    acc = acc / l
    outs = []
    for h in range(ATTN_HEADS):
        a0 = acc[rows_h * h:rows_h * h + t_new]
        a1 = acc[rows_h * h + t_new:rows_h * (h + 1)]
        outs.append(_rms(a0 - lam * a1, sg_ref[...]) * out_scale)
    o_ref[0] = jnp.concatenate(outs, axis=-1)


def _attention_kernel(pt_ref, lam_ref, sg_ref, q_ref, k_ref, v_ref, qs_ref, kn_ref, vn_ref, *refs,
                      n_pages, t_new, past_len, bq, bk, out_scale):
    page_refs = refs[:2 * n_pages]
    o_ref, os_ref = refs[2 * n_pages:2 * n_pages + 2]
    scratch = refs[2 * n_pages + 2:]
    _attn_sample_kernel(pt_ref, lam_ref, qs_ref, kn_ref, vn_ref, sg_ref, *page_refs, os_ref,
                        n_pages=n_pages, t_new=t_new, past_len=past_len, out_scale=out_scale)
    _attn_prompt_kernel(lam_ref, q_ref, k_ref, v_ref, sg_ref, o_ref, *scratch,
                        bq=bq, bk=bk, out_scale=out_scale)


def _attention(page_table, lam, sg, qb, kb, vb, q3, kn3, vn3, cache_kt, cache_v3,
               bq, past_len, out_scale):
    bsz, t, _ = qb.shape
    bk = kb.shape[-1]
    nq = t // bq
    n_dec, t_new, _ = q3.shape
    n_pages = page_table.shape[1]
    assert n_dec == bsz * ATTN_HEADS * nq, "one sample sequence per prompt grid step"
    pt = page_table.reshape(-1)

    def seq(b, h, i):
        return (b * ATTN_HEADS + h) * nq + i

    tok_spec = pl.BlockSpec((1, t_new, ATTN_WIDTH), lambda b, h, i, pt: (seq(b, h, i), 0, 0))

    def k_spec(p):
        return pl.BlockSpec((1, ATTN_WIDTH, PAGE_SIZE),
                            lambda b, h, i, pt, p=p: (pt[seq(b, h, i) * n_pages + p], 0, 0))

    def v_spec(p):
        return pl.BlockSpec((1, PAGE_SIZE * ATTN_HEADS, ATTN_V_DIM),
                            lambda b, h, i, pt, p=p: (pt[seq(b, h, i) * n_pages + p], 0, 0))

    head_spec = pl.BlockSpec((1, bq, ATTN_V_DIM), lambda b, h, i, pt: (b, i, h))
    grid_spec = pltpu.PrefetchScalarGridSpec(
        num_scalar_prefetch=1,
        grid=(bsz, ATTN_HEADS, nq),
        in_specs=[pl.BlockSpec(memory_space=pltpu.SMEM), _const_spec((1, ATTN_V_DIM)),
                  head_spec,
                  pl.BlockSpec((1, t // bk, ATTN_V_DIM, bk), lambda b, h, i, pt: (b, 0, h, 0)),
                  pl.BlockSpec((1, t, ATTN_V_DIM), lambda b, h, i, pt: (b, 0, h)),
                  tok_spec, tok_spec, tok_spec]
                 + [k_spec(p) for p in range(n_pages)] + [v_spec(p) for p in range(n_pages)],
        out_specs=[head_spec, tok_spec],
        scratch_shapes=[pltpu.VMEM((2, bq, ATTN_V_DIM), BF16),
                        pltpu.VMEM((t // bk, 2, bq, bk), F32),
                        pltpu.VMEM((2, bq, LANES), F32),
                        pltpu.VMEM((2, bq, LANES), F32),
                        pltpu.VMEM((2, bq, ATTN_V_DIM), F32)])
    kern = functools.partial(_attention_kernel, n_pages=n_pages, t_new=t_new, past_len=past_len,
                             bq=bq, bk=bk, out_scale=out_scale)
    return pl.pallas_call(
        kern,
        grid_spec=grid_spec,
        out_shape=[jax.ShapeDtypeStruct(qb.shape, BF16), jax.ShapeDtypeStruct(q3.shape, F32)],
        compiler_params=_cparams(3),
        name="attention",
    )(pt, lam, sg, qb, kb, vb, q3, kn3, vn3, *([cache_kt] * n_pages), *([cache_v3] * n_pages))


def _memory_attention(cq, mk_ref, mv_ref):
    outs = []
    for h in range(CA_HEADS):
        sl = slice(CA_HEAD_DIM * h, CA_HEAD_DIM * (h + 1))
        s = _dot_nt(cq[:, sl], mk_ref[0, :, sl])
        p = jnp.exp(s - jnp.max(s, axis=-1, keepdims=True))
        l = jnp.sum(p, axis=-1, keepdims=True)
        outs.append(_dot((p / l).astype(BF16), mv_ref[0, :, sl]))
    return jnp.concatenate(outs, axis=-1)


def _out_proj_kernel(x_ref, s_ref, o_ref, w1_ref, w2_ref, g2_ref, wq_ref, qg_ref, *refs):
    x1 = (x_ref[...] + _dot(s_ref[...].astype(BF16), w1_ref[...])
          + _dot(o_ref[...].astype(BF16), w2_ref[...]))
    xn = _rms(x1, g2_ref[...]).astype(BF16)
    cq = _head_rms(_dot(xn, wq_ref[...]), qg_ref[...], CA_HEAD_DIM, CA_HEAD_DIM ** -0.5)
    if len(refs) == 4:
        mk_ref, mv_ref, x1_ref, out_ref = refs
        out_ref[...] = _memory_attention(cq.astype(BF16), mk_ref, mv_ref).astype(out_ref.dtype)
    else:
        x1_ref, out_ref = refs
        out_ref[...] = cq.astype(out_ref.dtype)
    x1_ref[...] = x1


def _out_proj(x, ssm_o, att_o, w1, w2, g2, wq, qg, tm, out_dtype, mem=None):
    n = x.shape[0]
    bsz = 1 if mem is None else mem[0].shape[0]
    per_b = n // bsz // tm
    tok = lambda width: pl.BlockSpec((tm, width), lambda b, i: (b * per_b + i, 0))
    mem_spec = pl.BlockSpec((1, N_MEM, D_MODEL), lambda b, i: (b, 0, 0))
    return pl.pallas_call(
        _out_proj_kernel,
        grid=(bsz, per_b),
        in_specs=[tok(D_MODEL), tok(SSM_WIDTH), tok(ATTN_WIDTH),
                  _const_spec((SSM_WIDTH, D_MODEL)), _const_spec((ATTN_WIDTH, D_MODEL)),
                  _const_spec((1, D_MODEL)), _const_spec((D_MODEL, D_MODEL)),
                  _const_spec((1, CA_HEAD_DIM))] + ([] if mem is None else [mem_spec, mem_spec]),
        out_specs=[tok(D_MODEL), tok(D_MODEL)],
        out_shape=[jax.ShapeDtypeStruct((n, D_MODEL), F32),
                   jax.ShapeDtypeStruct((n, D_MODEL), out_dtype)],
        compiler_params=_cparams(2),
        name="out_proj",
    )(x, ssm_o, att_o, w1, w2, g2, wq, qg, *(() if mem is None else mem))


def _mem_kv_kernel(m_ref, g_ref, wk_ref, wv_ref, kg_ref, mk_ref, mv_ref, mkb_ref, mvb_ref):
    mn = _rms(m_ref[...], g_ref[...]).astype(BF16)
    mk = _head_rms(_dot(mn, wk_ref[...]), kg_ref[...], CA_HEAD_DIM, 1.0)
    mv = _dot(mn, wv_ref[...])
    mk_ref[...] = mk
    mv_ref[...] = mv
    mkb_ref[...] = mk.astype(BF16)
    mvb_ref[...] = mv.astype(BF16)


def _mem_kv(mem, g, wk, wv, kg, tm):
    n = mem.shape[0]
    tok = pl.BlockSpec((tm, D_MODEL), lambda i: (i, 0))
    wspec = _const_spec((D_MODEL, D_MODEL))
    return pl.pallas_call(
        _mem_kv_kernel,
        grid=(n // tm,),
        in_specs=[tok, _const_spec((1, D_MODEL)), wspec, wspec, _const_spec((1, CA_HEAD_DIM))],
        out_specs=[tok, tok, tok, tok],
        out_shape=[jax.ShapeDtypeStruct((n, D_MODEL), F32)] * 2
                  + [jax.ShapeDtypeStruct((n, D_MODEL), BF16)] * 2,
        compiler_params=_cparams(1),
        name="mem_kv",
    )(mem, g, wk, wv, kg)


def _cross_sample_one(cq_ref, mk_ref, mv_ref, co_ref, b, t_new):
    halves = CA_HEAD_DIM // LANES
    per_key = halves * CA_HEADS

    def heads(ref):
        return jnp.concatenate(
            [ref[b, pl.ds(half * CA_HEADS + h, N_MEM, stride=per_key), :]
             for h in range(CA_HEADS) for half in range(halves)], axis=1).astype(BF16)

    qe = jnp.concatenate([cq_ref[b]] * CA_HEADS, axis=0)
    row = lax.broadcasted_iota(jnp.int32, qe.shape, 0)
    lane = lax.broadcasted_iota(jnp.int32, qe.shape, 1)
    qe = jnp.where(lane // CA_HEAD_DIM == row // t_new, qe, 0.0).astype(BF16)
    s = _dot_nt(qe, heads(mk_ref))
    yield
    p = jnp.exp(s - jnp.max(s, axis=-1, keepdims=True))
    l = jnp.sum(p, axis=-1, keepdims=True)
    full = _dot((p / l).astype(BF16), heads(mv_ref))
    outs = [full[h * t_new:(h + 1) * t_new, CA_HEAD_DIM * h:CA_HEAD_DIM * (h + 1)]
            for h in range(CA_HEADS)]
    co_ref[b] = jnp.concatenate(outs, axis=-1)


def _mem_rows(mem):
    n = mem.shape[0]
    halves = CA_HEAD_DIM // LANES
    return (mem.reshape(n, N_MEM, CA_HEADS, halves, LANES).transpose(0, 1, 3, 2, 4)
            .reshape(n, N_MEM * halves * CA_HEADS, LANES))


def _ffn_kernel(x_ref, co_ref, wo_ref, g3_ref, wg_ref, wv_ref, cw_ref, cb_ref, wd_ref, prev_ref,
                *refs, tm, shift, pad, t_new):
    if len(refs) == 7:
        cq_ref, mk_ref, mv_ref, y_ref, cs_ref, cos_ref, hbuf = refs
    else:
        y_ref, cs_ref, hbuf = refs
    i = pl.program_id(1)

    @pl.when(i == 0)
    def _():
        hbuf[pad - 2 * shift:pad, :] = prev_ref[0]

    riders = ([_cross_sample_one(cq_ref, mk_ref, mv_ref, cos_ref, b, t_new)
               for b in range(cq_ref.shape[0])] if len(refs) == 7 else [])
    started = []

    def ride():
        if started:
            next(started.pop(), None)
        if riders:
            started.append(riders.pop(0))
            next(started[-1])

    ride()
    x2 = x_ref[...] + _dot(co_ref[...].astype(BF16), wo_ref[...])
    xn = _rms(x2, g3_ref[...]).astype(BF16)
    ride()
    hbuf[pad:pad + tm, :] = _dot(xn, wg_ref[...])
    ride()
    hv = _dot(xn, wv_ref[...])
    ride()
    cw = cw_ref[...]
    conv = (cb_ref[...] + cw[0:1, :] * hbuf[pad - 2 * shift:pad - 2 * shift + tm, :]
            + cw[1:2, :] * hbuf[pad - shift:pad - shift + tm, :]
            + cw[2:3, :] * hbuf[pad:pad + tm, :])
    act = (conv * jax.nn.sigmoid(conv) * hv).astype(BF16)
    y_ref[...] = x2 + _dot(act, wd_ref[...])
    while started or riders:
        ride()
    last = hbuf[pad + tm - 2 * shift:pad + tm, :]
    cs_ref[0] = last
    hbuf[pad - 2 * shift:pad, :] = last


def _ffn(x1, co, wo, g3, wg, wv, cw, cb, wd, prev, tm, shift, rider=None):
    n = x1.shape[0]
    n_seq_blocks = prev.shape[0]
    per_b = n // n_seq_blocks // tm
    pad = -(-2 * shift // SUBLANES) * SUBLANES
    tok = pl.BlockSpec((tm, D_MODEL), lambda b, i: (b * per_b + i, 0))
    state = pl.BlockSpec((1, 2 * shift, FFN_HIDDEN), lambda b, i: (b, 0, 0))
    in_specs = [tok, tok, _const_spec((D_MODEL, D_MODEL)), _const_spec((1, D_MODEL)),
                _const_spec((D_MODEL, FFN_HIDDEN)), _const_spec((D_MODEL, FFN_HIDDEN)),
                _const_spec((CONV_WIDTH, FFN_HIDDEN)), _const_spec((1, FFN_HIDDEN)),
                _const_spec((FFN_HIDDEN, D_MODEL)), state]
    out_specs = [tok, state]
    out_shape = [jax.ShapeDtypeStruct((n, D_MODEL), F32), jax.ShapeDtypeStruct(prev.shape, F32)]
    operands = (x1, co, wo, g3, wg, wv, cw, cb, wd, prev)
    t_new = None
    if rider is not None:
        cq3, mem_k, mem_v = rider
        n_dec, t_new, _ = cq3.shape
        per_step = n_dec // (n_seq_blocks * per_b)
        assert per_step * n_seq_blocks * per_b == n_dec
        seqs = pl.BlockSpec((per_step, t_new, D_MODEL), lambda b, i: (b * per_b + i, 0, 0))
        mem = pl.BlockSpec((per_step,) + mem_k.shape[1:], lambda b, i: (b * per_b + i, 0, 0))
        in_specs += [seqs, mem, mem]
        out_specs.append(seqs)
        out_shape.append(jax.ShapeDtypeStruct(cq3.shape, F32))
        operands += rider
    return pl.pallas_call(
        functools.partial(_ffn_kernel, tm=tm, shift=shift, pad=pad, t_new=t_new),
        grid=(n_seq_blocks, per_b),
        in_specs=in_specs,
        out_specs=out_specs,
        out_shape=out_shape,
        scratch_shapes=[pltpu.VMEM((pad + tm, FFN_HIDDEN), F32)],
        compiler_params=_cparams(2, VMEM_LIMIT if rider is None else VMEM_LIMIT_RIDER),
        name="ffn",
    )(*operands)


def _block_diag(blocks, n_outer):
    n, r, c = blocks.shape
    per = n // n_outer
    b = blocks.reshape(n_outer, per, r, c)
    eye = jnp.eye(per, dtype=blocks.dtype)
    full = b[:, :, :, None, :] * eye[None, :, None, :, None]
    return full.reshape(n_outer, per * r, per * c)


def _lam_groups(lb, n_seq):
    halves = lb.reshape(2, 1, N_SLABS, LANES)
    rows = jnp.broadcast_to(halves, (2, n_seq, N_SLABS, LANES))
    rows = rows.reshape(2 * n_seq // SUBLANES, SUBLANES, N_SLABS, LANES)
    return rows.transpose(0, 2, 1, 3)


def _layer(l, x_prompt, x_sample, mem_prompt, cache_k, cache_v, page_table,
           state_ssm_re, state_ssm_im, state_conv, cache_mem_k, cache_mem_v, p):
    bsz, seq, _ = x_prompt.shape
    n_dec, t_new, _ = x_sample.shape
    n_p = bsz * seq
    n_s = n_dec * t_new
    past_len = page_table.shape[1] * PAGE_SIZE
    lam0 = 0.8 - 0.6 * math.exp(-0.3 * l)
    out_scale = 1.0 - lam0
    row = lambda v: v.reshape(1, -1).astype(F32)
    bf = lambda w: w.astype(BF16)

    lb_re, lb_im, bb_re, bb_im, lam = _prep(
        p['ssm_a_re'], p['ssm_a_im'], p['ssm_log_dt'], p['ssm_b_re'], p['ssm_b_im'],
        p['lam_q1'], p['lam_k1'], p['lam_q2'], p['lam_k2'], lam0)

    def b_mats(bbt):
        blocks = bbt.reshape(SSM_GROUP_CH, SSM_GROUPS, SSM_STATE).transpose(1, 0, 2)
        return bf(_block_diag(blocks, 4))

    def c_mats(cm):
        return bf(_block_diag(cm.transpose(0, 2, 1), 4))

    mats = (b_mats(bb_re), b_mats(bb_im), c_mats(p['ssm_c_re']), c_mats(p['ssm_c_im']))
    d_row = row(p['ssm_d'])
    glu_b = bf(p['ssm_glu_w'])

    w_in = p['w_in']
    k0, k1 = SSM_WIDTH + ATTN_WIDTH, SSM_WIDTH + 2 * ATTN_WIDTH
    w_uqv = bf(jnp.concatenate([w_in[:, :k0], w_in[:, k1:]], axis=1))
    w_k = bf(w_in[:, k0:k1])
    seg = jnp.kron(jnp.eye(ATTN_WIDTH // ATTN_HEAD_DIM, dtype=F32),
                   jnp.ones((ATTN_HEAD_DIM, ATTN_HEAD_DIM), F32)).astype(BF16)
    qg = jnp.tile(p['q_norm_g'], ATTN_WIDTH // ATTN_HEAD_DIM).reshape(1, ATTN_WIDTH)
    kg = jnp.tile(p['k_norm_g'], ATTN_WIDTH // ATTN_HEAD_DIM).reshape(1, ATTN_WIDTH)
    sg = row(p['subln_g'])
    w_out_b = bf(p['w_out'])
    w1, w2 = w_out_b[:SSM_WIDTH], w_out_b[SSM_WIDTH:]
    wq_b, wk_b, wv_b, wo_b = bf(p['ca_wq']), bf(p['ca_wk']), bf(p['ca_wv']), bf(p['ca_wo'])
    wg_b, wvf_b, wd_b = bf(p['ffn_wg']), bf(p['ffn_wv']), bf(p['ffn_wd'])
    cw = p['ffn_conv_w'].astype(F32)
    cb = row(p['ffn_conv_b'])

    xp = x_prompt.reshape(n_p, D_MODEL)
    attn_bq, attn_bk = 512, 512
    kg_col = jnp.broadcast_to(p['k_norm_g'].reshape(ATTN_HEAD_DIM, 1), (ATTN_HEAD_DIM, 512))
    u_p, q_p, kt_p, v_p, kb_p, vb_p = _in_proj(xp, row(p['ln1_g']), w_uqv, w_k.T, seg, qg, kg_col,
                                               512, BF16, n_batch=bsz, blk=attn_bk)
    zeros_h = jnp.zeros((2 * bsz // SUBLANES, N_SLABS, SUBLANES, LANES), F32)
    ssm_p, hre_p, him_p = _ssm(u_p.reshape(bsz, seq, SSM_WIDTH), mats,
                               (_lam_groups(lb_re, bsz), _lam_groups(lb_im, bsz)),
                               d_row, glu_b, (zeros_h, zeros_h), bsz, 256)

    xs = x_sample.reshape(n_s, D_MODEL)
    u_s, q_s, k_s, v_s = _in_proj(xs, row(p['ln1_g']), w_uqv, w_k, seg, qg, kg, n_s, F32)
    h0 = (_state_to_groups(state_ssm_re.reshape(n_dec, N_STATE), n_dec),
          _state_to_groups(state_ssm_im.reshape(n_dec, N_STATE), n_dec))
    ssm_s, hre_s, him_s = _ssm(u_s.reshape(1, n_s, SSM_WIDTH), mats,
                               (_lam_groups(lb_re, n_dec), _lam_groups(lb_im, n_dec)),
                               d_row, glu_b, h0, n_dec, t_new)

    n_phys = cache_k.shape[0]
    cache_kt = cache_k.transpose(0, 2, 3, 4, 1).reshape(n_phys, ATTN_WIDTH, PAGE_SIZE)
    tok3 = lambda a: a.reshape(n_dec, t_new, ATTN_WIDTH)
    att_p, att_s = _attention(page_table, lam, sg, q_p.reshape(bsz, seq, ATTN_WIDTH), kb_p,
                              vb_p.reshape(bsz, seq, ATTN_WIDTH), tok3(q_s), tok3(k_s), tok3(v_s),
                              cache_kt, cache_v.reshape(n_phys, PAGE_SIZE * ATTN_HEADS, ATTN_V_DIM),
                              attn_bq, past_len, out_scale)

    mk, mv, mkb, mvb = _mem_kv(mem_prompt.reshape(bsz * N_MEM, D_MODEL), row(p['mem_norm_g']),
                               wk_b, wv_b, row(p['ca_k_norm_g']), 256)
    x1_p, co_p = _out_proj(xp, ssm_p.reshape(n_p, SSM_WIDTH), att_p.reshape(n_p, ATTN_WIDTH),
                           w1, w2, row(p['ln2_g']), wq_b, row(p['ca_q_norm_g']), 512, BF16,
                           mem=(mkb.reshape(bsz, N_MEM, D_MODEL), mvb.reshape(bsz, N_MEM, D_MODEL)))

    x1_s, cq_s = _out_proj(xs, ssm_s.reshape(n_s, SSM_WIDTH), att_s.reshape(n_s, ATTN_WIDTH),
                           w1, w2, row(p['ln2_g']), wq_b, row(p['ca_q_norm_g']), n_s, F32)
    y_p, conv_p, co_s = _ffn(x1_p, co_p, wo_b, row(p['ln3_g']), wg_b, wvf_b, cw, cb, wd_b,
                             jnp.zeros((bsz, CONV_WIDTH - 1, FFN_HIDDEN), F32), 512, 1,
                             rider=(cq_s.reshape(n_dec, t_new, D_MODEL),
                                    _mem_rows(cache_mem_k), _mem_rows(cache_mem_v)))

    tmaj = lambda a: a.reshape(n_dec, t_new, -1).transpose(1, 0, 2).reshape(n_s, -1)
    prev_s = state_conv.transpose(1, 0, 2).reshape(1, (CONV_WIDTH - 1) * n_dec, FFN_HIDDEN)
    y_s, conv_s = _ffn(tmaj(x1_s), tmaj(co_s), wo_b, row(p['ln3_g']), wg_b, wvf_b, cw, cb, wd_b,
                       prev_s, n_s, n_dec)
    y_s = y_s.reshape(t_new, n_dec, D_MODEL).transpose(1, 0, 2)
    conv_s = conv_s.reshape(CONV_WIDTH - 1, n_dec, FFN_HIDDEN).transpose(1, 0, 2)

    return (y_p.reshape(bsz, seq, D_MODEL), y_s,
            kt_p.reshape(bsz, ATTN_HEADS, 2, ATTN_HEAD_DIM, seq).transpose(0, 4, 1, 2, 3),
            v_p.reshape(bsz, seq, ATTN_HEADS, ATTN_V_DIM),
            k_s.reshape(n_dec, t_new, ATTN_HEADS, 2, ATTN_HEAD_DIM),
            v_s.reshape(n_dec, t_new, ATTN_HEADS, ATTN_V_DIM),
            _groups_to_state(hre_p, bsz), _groups_to_state(him_p, bsz),
            _groups_to_state(hre_s, n_dec), _groups_to_state(him_s, n_dec),
            conv_p, conv_s,
            mk.reshape(bsz, N_MEM, CA_HEADS, CA_HEAD_DIM), mv.reshape(bsz, N_MEM, CA_HEADS, CA_HEAD_DIM))


def kernel(x_prompt, x_sample, mem_prompt, cache_k, cache_v, page_table, state_ssm_re, state_ssm_im, state_conv, cache_mem_k, cache_mem_v, ln1_g, w_in, ssm_a_re, ssm_a_im, ssm_b_re, ssm_b_im, ssm_c_re, ssm_c_im, ssm_d, ssm_log_dt, ssm_glu_w, q_norm_g, k_norm_g, lam_q1, lam_k1, lam_q2, lam_k2, subln_g, w_out, ln2_g, mem_norm_g, ca_wq, ca_wk, ca_wv, ca_q_norm_g, ca_k_norm_g, ca_wo, ln3_g, ffn_wg, ffn_wv, ffn_conv_w, ffn_conv_b, ffn_wd):
    params = dict(
        ln1_g=ln1_g, w_in=w_in, ssm_a_re=ssm_a_re, ssm_a_im=ssm_a_im, ssm_b_re=ssm_b_re,
        ssm_b_im=ssm_b_im, ssm_c_re=ssm_c_re, ssm_c_im=ssm_c_im, ssm_d=ssm_d,
        ssm_log_dt=ssm_log_dt, ssm_glu_w=ssm_glu_w, q_norm_g=q_norm_g, k_norm_g=k_norm_g,
        lam_q1=lam_q1, lam_k1=lam_k1, lam_q2=lam_q2, lam_k2=lam_k2, subln_g=subln_g,
        w_out=w_out, ln2_g=ln2_g, mem_norm_g=mem_norm_g, ca_wq=ca_wq, ca_wk=ca_wk, ca_wv=ca_wv,
        ca_q_norm_g=ca_q_norm_g, ca_k_norm_g=ca_k_norm_g, ca_wo=ca_wo, ln3_g=ln3_g,
        ffn_wg=ffn_wg, ffn_wv=ffn_wv, ffn_conv_w=ffn_conv_w, ffn_conv_b=ffn_conv_b, ffn_wd=ffn_wd)
    y_p, y_s = x_prompt, x_sample
    per_layer = []
    for l in range(w_in.shape[0]):
        p = {k: v[l] for k, v in params.items()}
        outs = _layer(l, y_p, y_s, mem_prompt, cache_k[l], cache_v[l], page_table,
                      state_ssm_re[l], state_ssm_im[l], state_conv[l],
                      cache_mem_k[l], cache_mem_v[l], p)
        y_p, y_s = outs[0], outs[1]
        per_layer.append(outs[2:])
    return (y_p, y_s) + tuple(jnp.stack(leaf) for leaf in zip(*per_layer))
```

```python
import functools
import math

import jax
import jax.numpy as jnp
from jax import lax
from jax.experimental import pallas as pl
from jax.experimental.pallas import tpu as pltpu

F32 = jnp.float32
BF16 = jnp.bfloat16

D_MODEL = 1024
SSM_WIDTH = 512
SSM_GROUP_CH = 16
SSM_GROUPS = 32
SSM_STATE = 64
N_STATE = SSM_GROUPS * SSM_STATE
ATTN_WIDTH = 512
ATTN_HEAD_DIM = 64
ATTN_V_DIM = 128
ATTN_HEADS = 4
PAGE_SIZE = 128
N_MEM = 256
CA_HEADS = 4
CA_HEAD_DIM = 256
FFN_HIDDEN = 2816
CONV_WIDTH = 3
NORM_EPS = 1e-6

LANES = 128
SUBLANES = 8
VMEM_LIMIT = 56 * 1024 * 1024
VMEM_LIMIT_RIDER = 59 * 1024 * 1024

NEG_BIG = -1e30
LOG2E = math.log2(math.e)

def _cparams(n_axes, vmem_limit=VMEM_LIMIT):
    return pltpu.CompilerParams(
        dimension_semantics=("arbitrary",) * n_axes,
        vmem_limit_bytes=vmem_limit)


def _const_spec(shape):
    nd = len(shape)
    return pl.BlockSpec(shape, lambda *_: (0,) * nd, pipeline_mode=pl.Buffered(1))


def _rms(x, g):
    return x * lax.rsqrt(jnp.mean(x * x, axis=-1, keepdims=True) + NORM_EPS) * g


def _dot(a, b):
    return jnp.dot(a, b, preferred_element_type=F32)


def _dot_nt(a, b):
    return lax.dot_general(a, b, (((1,), (1,)), ((), ())), preferred_element_type=F32)


def _head_rms(x, g, width, scale):
    outs = []
    for h in range(x.shape[-1] // width):
        c = x[:, h * width:(h + 1) * width]
        outs.append(_rms(c, g) * scale)
    return jnp.concatenate(outs, axis=-1)


def _prep_kernel(are_ref, aim_ref, ldt_ref, bre_ref, bim_ref,
                 q1_ref, k1_ref, q2_ref, k2_ref,
                 lbre_ref, lbim_ref, bbre_ref, bbim_ref, lam_ref, *, lam0):
    a_re = are_ref[...]
    a_im = aim_ref[...]
    dt = jnp.exp(ldt_ref[...])
    mag = jnp.exp(a_re * dt)
    lb_re = mag * jnp.cos(a_im * dt)
    lb_im = mag * jnp.sin(a_im * dt)
    den = a_re * a_re + a_im * a_im
    n_re = lb_re - 1.0
    f_re = (n_re * a_re + lb_im * a_im) / den
    f_im = (lb_im * a_re - n_re * a_im) / den
    b_re = bre_ref[...]
    b_im = bim_ref[...]
    lbre_ref[...] = lb_re
    lbim_ref[...] = lb_im
    bbre_ref[...] = f_re * b_re - f_im * b_im
    bbim_ref[...] = f_re * b_im + f_im * b_re
    s1 = jnp.sum(q1_ref[...] * k1_ref[...], axis=-1, keepdims=True)
    s2 = jnp.sum(q2_ref[...] * k2_ref[...], axis=-1, keepdims=True)
    lam_ref[...] = jnp.exp(s1) - jnp.exp(s2) + lam0


def _prep(a_re, a_im, log_dt, b_re, b_im, q1, k1, q2, k2, lam0):
    are = a_re.reshape(1, N_STATE)
    aim = a_im.reshape(1, N_STATE)
    ldt = jnp.repeat(log_dt, SSM_STATE).reshape(1, N_STATE)
    bre = b_re.transpose(2, 0, 1).reshape(SSM_GROUP_CH, N_STATE)
    bim = b_im.transpose(2, 0, 1).reshape(SSM_GROUP_CH, N_STATE)
    vec = lambda v: v.reshape(1, ATTN_HEAD_DIM)
    row = jax.ShapeDtypeStruct((1, N_STATE), F32)
    mat = jax.ShapeDtypeStruct((SSM_GROUP_CH, N_STATE), F32)
    return pl.pallas_call(
        functools.partial(_prep_kernel, lam0=lam0),
        out_shape=(row, row, mat, mat, jax.ShapeDtypeStruct((1, 1), F32)),
        name="prep",
    )(are, aim, ldt, bre, bim, vec(q1), vec(k1), vec(q2), vec(k2))


def _in_proj_kernel(x_ref, g_ref, w_ref, wk_ref, seg_ref, qg_ref, kg_ref,
                    u_ref, q_ref, k_ref, v_ref, *bf_refs, key_major, blk):
    xn = _rms(x_ref[...], g_ref[...]).astype(BF16)
    proj = _dot(xn, w_ref[...])
    seg = seg_ref[...]

    def chunk_norm(z, g):
        ms = _dot((z * z).astype(BF16), seg) * (1.0 / ATTN_HEAD_DIM)
        return z * lax.rsqrt(ms + NORM_EPS) * g

    u_ref[...] = proj[:, :SSM_WIDTH]
    q = proj[:, SSM_WIDTH:SSM_WIDTH + ATTN_WIDTH]
    v = proj[:, SSM_WIDTH + ATTN_WIDTH:]
    q_ref[...] = (chunk_norm(q, qg_ref[...]) * (ATTN_HEAD_DIM ** -0.5 * LOG2E)).astype(q_ref.dtype)
    if key_major:
        kb_ref, vb_ref = bf_refs
        tm = x_ref.shape[0]
        z = _dot_nt(wk_ref[...], xn).reshape(ATTN_WIDTH // ATTN_HEAD_DIM, ATTN_HEAD_DIM, tm)
        ms = jnp.mean(z * z, axis=1, keepdims=True)
        kn = (z * lax.rsqrt(ms + NORM_EPS) * kg_ref[...]).reshape(ATTN_WIDTH, tm)
        k_ref[0] = kn
        for i in range(tm // blk):
            kb_ref[0, i] = kn[:, i * blk:(i + 1) * blk].astype(BF16)
        for h in range(ATTN_HEADS):
            v_ref[:, h, :] = v[:, ATTN_V_DIM * h:ATTN_V_DIM * (h + 1)]
        vb_ref[...] = v.astype(BF16)
    else:
        k_ref[...] = chunk_norm(_dot(xn, wk_ref[...]), kg_ref[...])
        v_ref[...] = v


def _in_proj(x, ln1_g, w_uqv, wk, seg, qg, kg, tm, q_dtype, n_batch=None, blk=None):
    n = x.shape[0]
    key_major = n_batch is not None
    bsz = n_batch if key_major else 1
    per_b = n // bsz // tm
    tok = lambda width: pl.BlockSpec((tm, width), lambda b, i: (b * per_b + i, 0))
    out_specs = [tok(SSM_WIDTH), tok(ATTN_WIDTH)]
    out_shape = [jax.ShapeDtypeStruct((n, SSM_WIDTH), F32),
                 jax.ShapeDtypeStruct((n, ATTN_WIDTH), q_dtype)]
    if key_major:
        t = n // bsz
        out_specs += [pl.BlockSpec((1, ATTN_WIDTH, tm), lambda b, i: (b, 0, i)),
                      pl.BlockSpec((tm, ATTN_HEADS, ATTN_V_DIM), lambda b, i: (b * per_b + i, 0, 0)),
                      pl.BlockSpec((1, tm // blk, ATTN_WIDTH, blk), lambda b, i: (b, i, 0, 0)),
                      tok(ATTN_WIDTH)]
        out_shape += [jax.ShapeDtypeStruct((bsz, ATTN_WIDTH, t), F32),
                      jax.ShapeDtypeStruct((n, ATTN_HEADS, ATTN_V_DIM), F32),
                      jax.ShapeDtypeStruct((bsz, t // blk, ATTN_WIDTH, blk), BF16),
                      jax.ShapeDtypeStruct((n, ATTN_WIDTH), BF16)]
    else:
        out_specs += [tok(ATTN_WIDTH), tok(ATTN_WIDTH)]
        out_shape += [jax.ShapeDtypeStruct((n, ATTN_WIDTH), F32)] * 2
    return pl.pallas_call(
        functools.partial(_in_proj_kernel, key_major=key_major, blk=blk),
        grid=(bsz, per_b),
        in_specs=[tok(D_MODEL), _const_spec((1, D_MODEL)), _const_spec(w_uqv.shape),
                  _const_spec(wk.shape), _const_spec((ATTN_WIDTH, ATTN_WIDTH)),
                  _const_spec((1, ATTN_WIDTH)), _const_spec(kg.shape)],
        out_specs=out_specs,
        out_shape=out_shape,
        compiler_params=_cparams(2),
        name="in_proj",
    )(x, ln1_g, w_uqv, wk, seg, qg, kg)


N_SLABS = N_STATE // 2 // LANES
HALF = N_STATE // 2


def _gelu_tanh(x):
    c = math.sqrt(2.0 / math.pi)
    return 0.5 * x * (1.0 + jnp.tanh(c * (x + 0.044715 * (x * x * x))))


def _ssm_kernel(u_ref, bre_ref, bim_ref, cre_ref, cim_ref, lre_ref, lim_ref,
                d_ref, glu_ref, h0re_ref, h0im_ref,
                out_ref, hre_ref, him_ref, sre, sim,
                *, n_seq, tc, pitch, n_bulk, rows_bulk, pitch_bulk):
    c = pl.program_id(0)
    rows = n_seq * tc
    n_groups = 2 * n_seq // SUBLANES

    @pl.when(c == 0)
    def _():
        hre_ref[...] = h0re_ref[...]
        him_ref[...] = h0im_ref[...]

    u = u_ref[...].reshape(rows, SSM_WIDTH)
    ub = u.astype(BF16)

    for j in range(4):
        hh, jj = divmod(j, 2)
        uj = ub[:, LANES * j:LANES * (j + 1)]
        for src, dst in ((bre_ref, sre), (bim_ref, sim)):
            bu = _dot(uj, src[j])
            for kk in range(4):
                slab = 4 * jj + kk
                for bb in range(n_bulk):
                    r0 = (hh * n_bulk + bb) * pitch_bulk
                    dst[slab, r0:r0 + rows_bulk, :] = (
                        bu[bb * rows_bulk:(bb + 1) * rows_bulk, LANES * kk:LANES * (kk + 1)])

    def group_body(g, carry):
        base = g * (SUBLANES * pitch)
        lr = [lre_ref[g, k] for k in range(N_SLABS)]
        li = [lim_ref[g, k] for k in range(N_SLABS)]
        hr0 = tuple(hre_ref[g, k] for k in range(N_SLABS))
        hi0 = tuple(him_ref[g, k] for k in range(N_SLABS))

        def step(t, hc):
            hr, hi = hc
            idx = pl.ds(base + t, SUBLANES, stride=pitch)
            nr, ni = [], []
            for k in range(N_SLABS):
                br = sre[k, idx, :]
                bi = sim[k, idx, :]
                r = lr[k] * hr[k] - li[k] * hi[k] + br
                i = lr[k] * hi[k] + li[k] * hr[k] + bi
                sre[k, idx, :] = r
                sim[k, idx, :] = i
                nr.append(r)
                ni.append(i)
            return tuple(nr), tuple(ni)

        hr, hi = lax.fori_loop(0, tc, step, (hr0, hi0), unroll=min(tc, 4))
        for k in range(N_SLABS):
            hre_ref[g, k] = hr[k]
            him_ref[g, k] = hi[k]
        return carry

    lax.fori_loop(0, n_groups, group_body, 0)

    y_rows = []
    for bb in range(n_bulk):
        y_cols = []
        for j in range(4):
            hh, jj = divmod(j, 2)
            r0 = (hh * n_bulk + bb) * pitch_bulk
            hr = jnp.concatenate(
                [sre[4 * jj + kk, r0:r0 + rows_bulk, :] for kk in range(4)], axis=1).astype(BF16)
            hi = jnp.concatenate(
                [sim[4 * jj + kk, r0:r0 + rows_bulk, :] for kk in range(4)], axis=1).astype(BF16)
            y_cols.append(_dot(hr, cre_ref[j]) - _dot(hi, cim_ref[j]))
        y_rows.append(jnp.concatenate(y_cols, axis=1))
    y = jnp.concatenate(y_rows, axis=0) if n_bulk > 1 else y_rows[0]
    y = y + d_ref[...] * u
    gl = _gelu_tanh(y)
    gate = jax.nn.sigmoid(_dot(gl.astype(BF16), glu_ref[...]))
    out_ref[...] = (gl * gate).astype(out_ref.dtype).reshape(out_ref.shape)


def _ssm(u3, mats, lam_g, d_row, glu_b, h0, n_seq, tc):
    bre, bim, cre, cim = mats
    lre_g, lim_g = lam_g
    h0re, h0im = h0
    n_groups = 2 * n_seq // SUBLANES
    t_total = u3.shape[0] * u3.shape[1] // n_seq
    n_chunks = t_total // tc
    if tc % SUBLANES == 0:
        pitch = tc + SUBLANES
        n_bulk, rows_bulk, pitch_bulk = n_seq, tc, pitch
        ublock = (n_seq, tc, SSM_WIDTH)
    else:
        assert n_chunks == 1
        pitch = tc
        n_bulk, rows_bulk, pitch_bulk = 1, n_seq * tc, n_seq * tc
        ublock = (1, n_seq * tc, SSM_WIDTH)
    scr_rows = 2 * n_bulk * pitch_bulk
    hshape = (n_groups, N_SLABS, SUBLANES, LANES)
    kern = functools.partial(_ssm_kernel, n_seq=n_seq, tc=tc, pitch=pitch,
                             n_bulk=n_bulk, rows_bulk=rows_bulk, pitch_bulk=pitch_bulk)
    return pl.pallas_call(
        kern,
        grid=(n_chunks,),
        in_specs=[pl.BlockSpec(ublock, lambda c: (0, c, 0)),
                  _const_spec(bre.shape), _const_spec(bim.shape),
                  _const_spec(cre.shape), _const_spec(cim.shape),
                  _const_spec(hshape), _const_spec(hshape),
                  _const_spec((1, SSM_WIDTH)), _const_spec((SSM_WIDTH, SSM_WIDTH)),
                  _const_spec(hshape), _const_spec(hshape)],
        out_specs=[pl.BlockSpec(ublock, lambda c: (0, c, 0)),
                   _const_spec(hshape), _const_spec(hshape)],
        out_shape=[jax.ShapeDtypeStruct(u3.shape, BF16),
                   jax.ShapeDtypeStruct(hshape, F32),
                   jax.ShapeDtypeStruct(hshape, F32)],
        scratch_shapes=[pltpu.VMEM((N_SLABS, scr_rows, LANES), F32),
                        pltpu.VMEM((N_SLABS, scr_rows, LANES), F32)],
        compiler_params=_cparams(1),
        name="ssm",
    )(u3, bre, bim, cre, cim, lre_g, lim_g, d_row, glu_b, h0re, h0im)


def _state_to_groups(h, n_seq):
    x = h.reshape(n_seq, 2, N_SLABS, LANES).transpose(1, 0, 2, 3)
    x = x.reshape(2 * n_seq // SUBLANES, SUBLANES, N_SLABS, LANES)
    return x.transpose(0, 2, 1, 3)


def _groups_to_state(x, n_seq):
    x = x.transpose(0, 2, 1, 3).reshape(2, n_seq, N_SLABS, LANES)
    return x.transpose(1, 0, 2, 3).reshape(n_seq, SSM_GROUPS, SSM_STATE)


def _head_slope(h):
    return jnp.where(h == 0, 2.0 ** -2, jnp.where(h == 1, 2.0 ** -4,
                     jnp.where(h == 2, 2.0 ** -6, 2.0 ** -8))).astype(F32)


def _attn_prompt_kernel(lam_ref, q_ref, k_ref, v_ref, sg_ref, o_ref,
                        q_scr, s_scr, m_scr, l_scr, acc_scr, *, bq, bk, out_scale, rider=None):
    ride = lambda: next(rider, None) if rider is not None else None
    ride()
    h = pl.program_id(1)
    qi = pl.program_id(2)
    slope = _head_slope(h) * LOG2E
    lam = lam_ref[0, 0]
    n_maps = 2

    q = q_ref[0]
    lane = lax.broadcasted_iota(jnp.int32, q.shape, 1)
    zero = jnp.zeros_like(q)
    q_scr[0] = jnp.where(lane < ATTN_HEAD_DIM, q, zero)
    q_scr[1] = jnp.where(lane >= ATTN_HEAD_DIM, q, zero)

    col = lax.broadcasted_iota(jnp.int32, (1, bk), 1)
    q0 = qi * bq

    def lane_tiles(x):
        return [x[:, LANES * i:LANES * (i + 1)] for i in range(bk // LANES)]

    n_full = q0 // bk

    def run_pairs(first, count, step):
        odd = lax.rem(count, 2)

        @pl.when(odd == 1)
        def _():
            step(first)

        def body(t, carry):
            j = first + odd + 2 * t
            step(j)
            step(j + 1)
            return carry

        lax.fori_loop(0, count // 2, body, 0)

    def scores(j, i, masked):
        kpos = col + j * bk
        s = _dot(q_scr[i], k_ref[0, j]) + slope * (kpos - q0).astype(F32)
        if masked:
            qpos = lax.broadcasted_iota(jnp.int32, (bq, bk), 0) + q0
            s = jnp.where(qpos >= kpos, s, -jnp.inf)
        return s

    def accumulate(j, i, p):
        start = pl.multiple_of(j * bk, bk)
        lpart = l_scr[i]
        for piece in lane_tiles(p):
            lpart = lpart + piece
        l_scr[i] = lpart
        acc_scr[i] += _dot(p.astype(BF16), v_ref[0, pl.ds(start, bk), :])

    def score_step(j, masked=False):
        for i in range(n_maps):
            s = scores(j, i, masked)
            s_scr[j, i] = s
            mpart = m_scr[i]
            for piece in lane_tiles(s):
                mpart = jnp.maximum(mpart, piece)
            m_scr[i] = mpart

    m_scr[...] = jnp.full(m_scr.shape, -jnp.inf, F32)
    score_step(n_full, masked=True)
    ride()
    run_pairs(0, n_full, score_step)
    ms = [jnp.max(m_scr[i], axis=-1, keepdims=True) for i in range(n_maps)]
    ride()

    def sum_step(j):
        for i in range(n_maps):
            accumulate(j, i, jnp.exp2(s_scr[j, i] - ms[i]))

    l_scr[...] = jnp.zeros(l_scr.shape, F32)
    acc_scr[...] = jnp.zeros(acc_scr.shape, F32)
    run_pairs(0, n_full + 1, sum_step)

    outs = [acc_scr[i] / jnp.sum(l_scr[i], axis=-1, keepdims=True) for i in range(n_maps)]
    o = outs[0] - lam * outs[1]
    o_ref[0] = (_rms(o, sg_ref[...]) * out_scale).astype(o_ref.dtype)
    ride()


def _attn_sample_steps(lam_ref, q_ref, kn_ref, vn_ref, sg_ref, k_refs, v_refs, o_ref,
                       *, t_new, past_len, out_scale):
    n_pages = len(k_refs)
    lam = lam_ref[0, 0]
    n_rows = 2 * ATTN_HEADS * t_new

    q4 = q_ref[0]
    qe = jnp.concatenate([q4] * (2 * ATTN_HEADS), axis=0)
    row = lax.broadcasted_iota(jnp.int32, qe.shape, 0)
    lane = lax.broadcasted_iota(jnp.int32, qe.shape, 1)
    qe = jnp.where(lane // ATTN_HEAD_DIM == row // t_new, qe, 0.0)
    qeb = qe.astype(BF16)

    rcol = lax.broadcasted_iota(jnp.int32, (n_rows, 1), 0)
    head_c = rcol // (2 * t_new)
    tok_c = rcol % t_new
    slope = _head_slope(head_c) * LOG2E
    rows_h = 2 * t_new

    kt = jnp.concatenate([k_refs[p][...].astype(BF16) for p in range(n_pages)], axis=1)
    kpos = lax.broadcasted_iota(jnp.int32, (1, n_pages * PAGE_SIZE), 1) - past_len
    s = _dot(qeb, kt) + slope * kpos.astype(F32)

    kn = kn_ref[0]
    vn = vn_ref[0]
    s_new = []
    for c in range(t_new):
        sc = jnp.sum(qe * kn[c:c + 1, :], axis=-1, keepdims=True) + slope * float(c)
        s_new.append(jnp.where(tok_c >= c, sc, -jnp.inf))

    yield
    m = jnp.max(s, axis=-1, keepdims=True)
    for sc in s_new:
        m = jnp.maximum(m, sc)
    pe = jnp.exp2(s - m)
    l = jnp.sum(pe, axis=-1, keepdims=True)

    pv = []
    for h in range(ATTN_HEADS):
        vh = jnp.concatenate(
            [v_refs[p][pl.ds(h, PAGE_SIZE, stride=ATTN_HEADS), :].astype(BF16)
             for p in range(n_pages)], axis=0)
        pv.append(_dot(pe[rows_h * h:rows_h * (h + 1)].astype(BF16), vh))
    acc = jnp.concatenate(pv, axis=0)
    yield

    for c in range(t_new):
        pc = jnp.exp2(s_new[c] - m)
        l = l + pc
        vc = jnp.concatenate(
            [jnp.broadcast_to(vn[c:c + 1, ATTN_V_DIM * h:ATTN_V_DIM * (h + 1)], (rows_h, ATTN_V_DIM))
             for h in range(ATTN_HEADS)], axis=0)
        acc = acc + pc * vc

    acc = acc / l
    outs = []
    for h in range(ATTN_HEADS):
        a0 = acc[rows_h * h:rows_h * h + t_new]
        a1 = acc[rows_h * h + t_new:rows_h * (h + 1)]
        outs.append(_rms(a0 - lam * a1, sg_ref[...]) * out_scale)
    o_ref[0] = jnp.concatenate(outs, axis=-1)


def _attention_kernel(pt_ref, lam_ref, sg_ref, q_ref, k_ref, v_ref, qs_ref, kn_ref, vn_ref,
                      kc_ref, vc_ref, o_ref, os_ref, kbuf, vbuf, sem, *scratch,
                      n_pages, t_new, past_len, bq, bk, out_scale):
    n_steps = pl.num_programs(0) * pl.num_programs(1) * pl.num_programs(2)
    step = ((pl.program_id(0) * pl.num_programs(1) + pl.program_id(1)) * pl.num_programs(2)
            + pl.program_id(2))

    def page_copies(seq_idx, slot):
        copies = []
        for p in range(n_pages):
            page = pt_ref[seq_idx * n_pages + p]
            copies.append(pltpu.make_async_copy(kc_ref.at[page], kbuf.at[slot, p], sem.at[slot]))
            copies.append(pltpu.make_async_copy(vc_ref.at[page], vbuf.at[slot, p], sem.at[slot]))
        return copies

    @pl.when(step == 0)
    def _():
        for copy in page_copies(0, 0):
            copy.start()

    @pl.when(step + 1 < n_steps)
    def _():
        for copy in page_copies(step + 1, lax.rem(step + 1, 2)):
            copy.start()

    slot = lax.rem(step, 2)
    for copy in page_copies(step, slot):
        copy.wait()

    k_refs = [kbuf.at[slot, p] for p in range(n_pages)]
    v_refs = [vbuf.at[slot, p] for p in range(n_pages)]
    sample = _attn_sample_steps(lam_ref, qs_ref, kn_ref, vn_ref, sg_ref, k_refs, v_refs, os_ref,
                                t_new=t_new, past_len=past_len, out_scale=out_scale)
    _attn_prompt_kernel(lam_ref, q_ref, k_ref, v_ref, sg_ref, o_ref, *scratch,
                        bq=bq, bk=bk, out_scale=out_scale, rider=sample)


def _attention(page_table, lam, sg, qb, kb, vb, q3, kn3, vn3, cache_kt, cache_v3,
               bq, past_len, out_scale):
    bsz, t, _ = qb.shape
    bk = kb.shape[-1]
    nq = t // bq
    n_dec, t_new, _ = q3.shape
    n_pages = page_table.shape[1]
    assert n_dec == bsz * ATTN_HEADS * nq, "one sample sequence per prompt grid step"
    pt = page_table.reshape(-1)

    def seq(b, h, i):
        return (b * ATTN_HEADS + h) * nq + i

    tok_spec = pl.BlockSpec((1, t_new, ATTN_WIDTH), lambda b, h, i, pt: (seq(b, h, i), 0, 0))

    head_spec = pl.BlockSpec((1, bq, ATTN_V_DIM), lambda b, h, i, pt: (b, i, h))
    grid_spec = pltpu.PrefetchScalarGridSpec(
        num_scalar_prefetch=1,
        grid=(bsz, ATTN_HEADS, nq),
        in_specs=[pl.BlockSpec(memory_space=pltpu.SMEM), _const_spec((1, ATTN_V_DIM)),
                  head_spec,
                  pl.BlockSpec((1, t // bk, ATTN_V_DIM, bk), lambda b, h, i, pt: (b, 0, h, 0)),
                  pl.BlockSpec((1, t, ATTN_V_DIM), lambda b, h, i, pt: (b, 0, h)),
                  tok_spec, tok_spec, tok_spec,
                  pl.BlockSpec(memory_space=pl.ANY), pl.BlockSpec(memory_space=pl.ANY)],
        out_specs=[head_spec, tok_spec],
        scratch_shapes=[pltpu.VMEM((2, n_pages) + cache_kt.shape[1:], F32),
                        pltpu.VMEM((2, n_pages) + cache_v3.shape[1:], F32),
                        pltpu.SemaphoreType.DMA((2,)),
                        pltpu.VMEM((2, bq, ATTN_V_DIM), BF16),
                        pltpu.VMEM((t // bk, 2, bq, bk), F32),
                        pltpu.VMEM((2, bq, LANES), F32),
                        pltpu.VMEM((2, bq, LANES), F32),
                        pltpu.VMEM((2, bq, ATTN_V_DIM), F32)])
    kern = functools.partial(_attention_kernel, n_pages=n_pages, t_new=t_new, past_len=past_len,
                             bq=bq, bk=bk, out_scale=out_scale)
    return pl.pallas_call(
        kern,
        grid_spec=grid_spec,
        out_shape=[jax.ShapeDtypeStruct(qb.shape, BF16), jax.ShapeDtypeStruct(q3.shape, F32)],
        compiler_params=_cparams(3),
        name="attention",
    )(pt, lam, sg, qb, kb, vb, q3, kn3, vn3, cache_kt, cache_v3)


def _memory_attention(cq, mk_ref, mv_ref):
    outs = []
    for h in range(CA_HEADS):
        sl = slice(CA_HEAD_DIM * h, CA_HEAD_DIM * (h + 1))
        s = _dot_nt(cq[:, sl], mk_ref[0, :, sl])
        p = jnp.exp(s - jnp.max(s, axis=-1, keepdims=True))
        l = jnp.sum(p, axis=-1, keepdims=True)
        outs.append(_dot((p / l).astype(BF16), mv_ref[0, :, sl]))
    return jnp.concatenate(outs, axis=-1)


def _out_proj_kernel(x_ref, s_ref, o_ref, w1_ref, w2_ref, g2_ref, wq_ref, qg_ref, *refs):
    x1 = (x_ref[...] + _dot(s_ref[...].astype(BF16), w1_ref[...])
          + _dot(o_ref[...].astype(BF16), w2_ref[...]))
    xn = _rms(x1, g2_ref[...]).astype(BF16)
    cq = _head_rms(_dot(xn, wq_ref[...]), qg_ref[...], CA_HEAD_DIM, CA_HEAD_DIM ** -0.5)
    if len(refs) == 4:
        mk_ref, mv_ref, x1_ref, out_ref = refs
        out_ref[...] = _memory_attention(cq.astype(BF16), mk_ref, mv_ref).astype(out_ref.dtype)
    else:
        x1_ref, out_ref = refs
        out_ref[...] = cq.astype(out_ref.dtype)
    x1_ref[...] = x1


def _out_proj(x, ssm_o, att_o, w1, w2, g2, wq, qg, tm, out_dtype, mem=None):
    n = x.shape[0]
    bsz = 1 if mem is None else mem[0].shape[0]
    per_b = n // bsz // tm
    tok = lambda width: pl.BlockSpec((tm, width), lambda b, i: (b * per_b + i, 0))
    mem_spec = pl.BlockSpec((1, N_MEM, D_MODEL), lambda b, i: (b, 0, 0))
    return pl.pallas_call(
        _out_proj_kernel,
        grid=(bsz, per_b),
        in_specs=[tok(D_MODEL), tok(SSM_WIDTH), tok(ATTN_WIDTH),
                  _const_spec((SSM_WIDTH, D_MODEL)), _const_spec((ATTN_WIDTH, D_MODEL)),
                  _const_spec((1, D_MODEL)), _const_spec((D_MODEL, D_MODEL)),
                  _const_spec((1, CA_HEAD_DIM))] + ([] if mem is None else [mem_spec, mem_spec]),
        out_specs=[tok(D_MODEL), tok(D_MODEL)],
        out_shape=[jax.ShapeDtypeStruct((n, D_MODEL), F32),
                   jax.ShapeDtypeStruct((n, D_MODEL), out_dtype)],
        compiler_params=_cparams(2),
        name="out_proj",
    )(x, ssm_o, att_o, w1, w2, g2, wq, qg, *(() if mem is None else mem))


def _mem_kv_kernel(m_ref, g_ref, wk_ref, wv_ref, kg_ref, mk_ref, mv_ref, mkb_ref, mvb_ref):
    mn = _rms(m_ref[...], g_ref[...]).astype(BF16)
    mk = _head_rms(_dot(mn, wk_ref[...]), kg_ref[...], CA_HEAD_DIM, 1.0)
    mv = _dot(mn, wv_ref[...])
    mk_ref[...] = mk
    mv_ref[...] = mv
    mkb_ref[...] = mk.astype(BF16)
    mvb_ref[...] = mv.astype(BF16)


def _mem_kv(mem, g, wk, wv, kg, tm):
    n = mem.shape[0]
    tok = pl.BlockSpec((tm, D_MODEL), lambda i: (i, 0))
    wspec = _const_spec((D_MODEL, D_MODEL))
    return pl.pallas_call(
        _mem_kv_kernel,
        grid=(n // tm,),
        in_specs=[tok, _const_spec((1, D_MODEL)), wspec, wspec, _const_spec((1, CA_HEAD_DIM))],
        out_specs=[tok, tok, tok, tok],
        out_shape=[jax.ShapeDtypeStruct((n, D_MODEL), F32)] * 2
                  + [jax.ShapeDtypeStruct((n, D_MODEL), BF16)] * 2,
        compiler_params=_cparams(1),
        name="mem_kv",
    )(mem, g, wk, wv, kg)


def _cross_sample_one(cq_ref, mk_ref, mv_ref, co_ref, b, t_new):
    halves = CA_HEAD_DIM // LANES
    per_key = halves * CA_HEADS

    def heads(ref):
        return jnp.concatenate(
            [ref[b, pl.ds(half * CA_HEADS + h, N_MEM, stride=per_key), :]
             for h in range(CA_HEADS) for half in range(halves)], axis=1).astype(BF16)

    qe = jnp.concatenate([cq_ref[b]] * CA_HEADS, axis=0)
    row = lax.broadcasted_iota(jnp.int32, qe.shape, 0)
    lane = lax.broadcasted_iota(jnp.int32, qe.shape, 1)
    qe = jnp.where(lane // CA_HEAD_DIM == row // t_new, qe, 0.0).astype(BF16)
    s = _dot_nt(qe, heads(mk_ref))
    yield
    p = jnp.exp(s - jnp.max(s, axis=-1, keepdims=True))
    l = jnp.sum(p, axis=-1, keepdims=True)
    full = _dot((p / l).astype(BF16), heads(mv_ref))
    outs = [full[h * t_new:(h + 1) * t_new, CA_HEAD_DIM * h:CA_HEAD_DIM * (h + 1)]
            for h in range(CA_HEADS)]
    co_ref[b] = jnp.concatenate(outs, axis=-1)


def _mem_rows(mem):
    n = mem.shape[0]
    halves = CA_HEAD_DIM // LANES
    return (mem.reshape(n, N_MEM, CA_HEADS, halves, LANES).transpose(0, 1, 3, 2, 4)
            .reshape(n, N_MEM * halves * CA_HEADS, LANES))


def _ffn_kernel(x_ref, co_ref, wo_ref, g3_ref, wg_ref, wv_ref, cw_ref, cb_ref, wd_ref, prev_ref,
                *refs, tm, shift, pad, t_new):
    if len(refs) == 7:
        cq_ref, mk_ref, mv_ref, y_ref, cs_ref, cos_ref, hbuf = refs
    else:
        y_ref, cs_ref, hbuf = refs
    i = pl.program_id(1)

    @pl.when(i == 0)
    def _():
        hbuf[pad - 2 * shift:pad, :] = prev_ref[0]

    riders = ([_cross_sample_one(cq_ref, mk_ref, mv_ref, cos_ref, b, t_new)
               for b in range(cq_ref.shape[0])] if len(refs) == 7 else [])
    started = []

    def ride():
        if started:
            next(started.pop(), None)
        if riders:
            started.append(riders.pop(0))
            next(started[-1])

    ride()
    x2 = x_ref[...] + _dot(co_ref[...].astype(BF16), wo_ref[...])
    xn = _rms(x2, g3_ref[...]).astype(BF16)
    ride()
    hbuf[pad:pad + tm, :] = _dot(xn, wg_ref[...])
    ride()
    hv = _dot(xn, wv_ref[...])
    ride()
    cw = cw_ref[...]
    conv = (cb_ref[...] + cw[0:1, :] * hbuf[pad - 2 * shift:pad - 2 * shift + tm, :]
            + cw[1:2, :] * hbuf[pad - shift:pad - shift + tm, :]
            + cw[2:3, :] * hbuf[pad:pad + tm, :])
    act = (conv * jax.nn.sigmoid(conv) * hv).astype(BF16)
    y_ref[...] = x2 + _dot(act, wd_ref[...])
    while started or riders:
        ride()
    last = hbuf[pad + tm - 2 * shift:pad + tm, :]
    cs_ref[0] = last
    hbuf[pad - 2 * shift:pad, :] = last


def _ffn(x1, co, wo, g3, wg, wv, cw, cb, wd, prev, tm, shift, rider=None):
    n = x1.shape[0]
    n_seq_blocks = prev.shape[0]
    per_b = n // n_seq_blocks // tm
    pad = -(-2 * shift // SUBLANES) * SUBLANES
    tok = pl.BlockSpec((tm, D_MODEL), lambda b, i: (b * per_b + i, 0))
    state = pl.BlockSpec((1, 2 * shift, FFN_HIDDEN), lambda b, i: (b, 0, 0))
    in_specs = [tok, tok, _const_spec((D_MODEL, D_MODEL)), _const_spec((1, D_MODEL)),
                _const_spec((D_MODEL, FFN_HIDDEN)), _const_spec((D_MODEL, FFN_HIDDEN)),
                _const_spec((CONV_WIDTH, FFN_HIDDEN)), _const_spec((1, FFN_HIDDEN)),
                _const_spec((FFN_HIDDEN, D_MODEL)), state]
    out_specs = [tok, state]
    out_shape = [jax.ShapeDtypeStruct((n, D_MODEL), F32), jax.ShapeDtypeStruct(prev.shape, F32)]
    operands = (x1, co, wo, g3, wg, wv, cw, cb, wd, prev)
    t_new = None
    if rider is not None:
        cq3, mem_k, mem_v = rider
        n_dec, t_new, _ = cq3.shape
        per_step = n_dec // (n_seq_blocks * per_b)
        assert per_step * n_seq_blocks * per_b == n_dec
        seqs = pl.BlockSpec((per_step, t_new, D_MODEL), lambda b, i: (b * per_b + i, 0, 0))
        mem = pl.BlockSpec((per_step,) + mem_k.shape[1:], lambda b, i: (b * per_b + i, 0, 0))
        in_specs += [seqs, mem, mem]
        out_specs.append(seqs)
        out_shape.append(jax.ShapeDtypeStruct(cq3.shape, F32))
        operands += rider
    return pl.pallas_call(
        functools.partial(_ffn_kernel, tm=tm, shift=shift, pad=pad, t_new=t_new),
        grid=(n_seq_blocks, per_b),
        in_specs=in_specs,
        out_specs=out_specs,
        out_shape=out_shape,
        scratch_shapes=[pltpu.VMEM((pad + tm, FFN_HIDDEN), F32)],
        compiler_params=_cparams(2, VMEM_LIMIT if rider is None else VMEM_LIMIT_RIDER),
        name="ffn",
    )(*operands)


def _block_diag(blocks, n_outer):
    n, r, c = blocks.shape
    per = n // n_outer
    b = blocks.reshape(n_outer, per, r, c)
    eye = jnp.eye(per, dtype=blocks.dtype)
    full = b[:, :, :, None, :] * eye[None, :, None, :, None]
    return full.reshape(n_outer, per * r, per * c)


def _lam_groups(lb, n_seq):
    halves = lb.reshape(2, 1, N_SLABS, LANES)
    rows = jnp.broadcast_to(halves, (2, n_seq, N_SLABS, LANES))
    rows = rows.reshape(2 * n_seq // SUBLANES, SUBLANES, N_SLABS, LANES)
    return rows.transpose(0, 2, 1, 3)


def _layer(l, x_prompt, x_sample, mem_prompt, cache_k, cache_v, page_table,
           state_ssm_re, state_ssm_im, state_conv, cache_mem_k, cache_mem_v, p):
    bsz, seq, _ = x_prompt.shape
    n_dec, t_new, _ = x_sample.shape
    n_p = bsz * seq
    n_s = n_dec * t_new
    past_len = page_table.shape[1] * PAGE_SIZE
    lam0 = 0.8 - 0.6 * math.exp(-0.3 * l)
    out_scale = 1.0 - lam0
    row = lambda v: v.reshape(1, -1).astype(F32)
    bf = lambda w: w.astype(BF16)

    lb_re, lb_im, bb_re, bb_im, lam = _prep(
        p['ssm_a_re'], p['ssm_a_im'], p['ssm_log_dt'], p['ssm_b_re'], p['ssm_b_im'],
        p['lam_q1'], p['lam_k1'], p['lam_q2'], p['lam_k2'], lam0)

    def b_mats(bbt):
        blocks = bbt.reshape(SSM_GROUP_CH, SSM_GROUPS, SSM_STATE).transpose(1, 0, 2)
        return bf(_block_diag(blocks, 4))

    def c_mats(cm):
        return bf(_block_diag(cm.transpose(0, 2, 1), 4))

    mats = (b_mats(bb_re), b_mats(bb_im), c_mats(p['ssm_c_re']), c_mats(p['ssm_c_im']))
    d_row = row(p['ssm_d'])
    glu_b = bf(p['ssm_glu_w'])

    w_in = p['w_in']
    k0, k1 = SSM_WIDTH + ATTN_WIDTH, SSM_WIDTH + 2 * ATTN_WIDTH
    w_uqv = bf(jnp.concatenate([w_in[:, :k0], w_in[:, k1:]], axis=1))
    w_k = bf(w_in[:, k0:k1])
    seg = jnp.kron(jnp.eye(ATTN_WIDTH // ATTN_HEAD_DIM, dtype=F32),
                   jnp.ones((ATTN_HEAD_DIM, ATTN_HEAD_DIM), F32)).astype(BF16)
    qg = jnp.tile(p['q_norm_g'], ATTN_WIDTH // ATTN_HEAD_DIM).reshape(1, ATTN_WIDTH)
    kg = jnp.tile(p['k_norm_g'], ATTN_WIDTH // ATTN_HEAD_DIM).reshape(1, ATTN_WIDTH)
    sg = row(p['subln_g'])
    w_out_b = bf(p['w_out'])
    w1, w2 = w_out_b[:SSM_WIDTH], w_out_b[SSM_WIDTH:]
    wq_b, wk_b, wv_b, wo_b = bf(p['ca_wq']), bf(p['ca_wk']), bf(p['ca_wv']), bf(p['ca_wo'])
    wg_b, wvf_b, wd_b = bf(p['ffn_wg']), bf(p['ffn_wv']), bf(p['ffn_wd'])
    cw = p['ffn_conv_w'].astype(F32)
    cb = row(p['ffn_conv_b'])

    xp = x_prompt.reshape(n_p, D_MODEL)
    attn_bq, attn_bk = 512, 512
    kg_col = jnp.broadcast_to(p['k_norm_g'].reshape(ATTN_HEAD_DIM, 1), (ATTN_HEAD_DIM, 512))
    u_p, q_p, kt_p, v_p, kb_p, vb_p = _in_proj(xp, row(p['ln1_g']), w_uqv, w_k.T, seg, qg, kg_col,
                                               512, BF16, n_batch=bsz, blk=attn_bk)
    zeros_h = jnp.zeros((2 * bsz // SUBLANES, N_SLABS, SUBLANES, LANES), F32)
    ssm_p, hre_p, him_p = _ssm(u_p.reshape(bsz, seq, SSM_WIDTH), mats,
                               (_lam_groups(lb_re, bsz), _lam_groups(lb_im, bsz)),
                               d_row, glu_b, (zeros_h, zeros_h), bsz, 256)

    xs = x_sample.reshape(n_s, D_MODEL)
    u_s, q_s, k_s, v_s = _in_proj(xs, row(p['ln1_g']), w_uqv, w_k, seg, qg, kg, n_s, F32)
    h0 = (_state_to_groups(state_ssm_re.reshape(n_dec, N_STATE), n_dec),
          _state_to_groups(state_ssm_im.reshape(n_dec, N_STATE), n_dec))
    ssm_s, hre_s, him_s = _ssm(u_s.reshape(1, n_s, SSM_WIDTH), mats,
                               (_lam_groups(lb_re, n_dec), _lam_groups(lb_im, n_dec)),
                               d_row, glu_b, h0, n_dec, t_new)

    n_phys = cache_k.shape[0]
    cache_kt = cache_k.transpose(0, 2, 3, 4, 1).reshape(n_phys, ATTN_WIDTH, PAGE_SIZE)
    tok3 = lambda a: a.reshape(n_dec, t_new, ATTN_WIDTH)
    att_p, att_s = _attention(page_table, lam, sg, q_p.reshape(bsz, seq, ATTN_WIDTH), kb_p,
                              vb_p.reshape(bsz, seq, ATTN_WIDTH), tok3(q_s), tok3(k_s), tok3(v_s),
                              cache_kt, cache_v.reshape(n_phys, PAGE_SIZE * ATTN_HEADS, ATTN_V_DIM),
                              attn_bq, past_len, out_scale)

    mk, mv, mkb, mvb = _mem_kv(mem_prompt.reshape(bsz * N_MEM, D_MODEL), row(p['mem_norm_g']),
                               wk_b, wv_b, row(p['ca_k_norm_g']), 256)
    x1_p, co_p = _out_proj(xp, ssm_p.reshape(n_p, SSM_WIDTH), att_p.reshape(n_p, ATTN_WIDTH),
                           w1, w2, row(p['ln2_g']), wq_b, row(p['ca_q_norm_g']), 512, BF16,
                           mem=(mkb.reshape(bsz, N_MEM, D_MODEL), mvb.reshape(bsz, N_MEM, D_MODEL)))

    x1_s, cq_s = _out_proj(xs, ssm_s.reshape(n_s, SSM_WIDTH), att_s.reshape(n_s, ATTN_WIDTH),
                           w1, w2, row(p['ln2_g']), wq_b, row(p['ca_q_norm_g']), n_s, F32)
    y_p, conv_p, co_s = _ffn(x1_p, co_p, wo_b, row(p['ln3_g']), wg_b, wvf_b, cw, cb, wd_b,
                             jnp.zeros((bsz, CONV_WIDTH - 1, FFN_HIDDEN), F32), 512, 1,
                             rider=(cq_s.reshape(n_dec, t_new, D_MODEL),
                                    _mem_rows(cache_mem_k), _mem_rows(cache_mem_v)))

    tmaj = lambda a: a.reshape(n_dec, t_new, -1).transpose(1, 0, 2).reshape(n_s, -1)
    prev_s = state_conv.transpose(1, 0, 2).reshape(1, (CONV_WIDTH - 1) * n_dec, FFN_HIDDEN)
    y_s, conv_s = _ffn(tmaj(x1_s), tmaj(co_s), wo_b, row(p['ln3_g']), wg_b, wvf_b, cw, cb, wd_b,
                       prev_s, n_s, n_dec)
    y_s = y_s.reshape(t_new, n_dec, D_MODEL).transpose(1, 0, 2)
    conv_s = conv_s.reshape(CONV_WIDTH - 1, n_dec, FFN_HIDDEN).transpose(1, 0, 2)

    return (y_p.reshape(bsz, seq, D_MODEL), y_s,
            kt_p.reshape(bsz, ATTN_HEADS, 2, ATTN_HEAD_DIM, seq).transpose(0, 4, 1, 2, 3),
            v_p.reshape(bsz, seq, ATTN_HEADS, ATTN_V_DIM),
            k_s.reshape(n_dec, t_new, ATTN_HEADS, 2, ATTN_HEAD_DIM),
            v_s.reshape(n_dec, t_new, ATTN_HEADS, ATTN_V_DIM),
            _groups_to_state(hre_p, bsz), _groups_to_state(him_p, bsz),
            _groups_to_state(hre_s, n_dec), _groups_to_state(him_s, n_dec),
            conv_p, conv_s,
            mk.reshape(bsz, N_MEM, CA_HEADS, CA_HEAD_DIM), mv.reshape(bsz, N_MEM, CA_HEADS, CA_HEAD_DIM))


def kernel(x_prompt, x_sample, mem_prompt, cache_k, cache_v, page_table, state_ssm_re, state_ssm_im, state_conv, cache_mem_k, cache_mem_v, ln1_g, w_in, ssm_a_re, ssm_a_im, ssm_b_re, ssm_b_im, ssm_c_re, ssm_c_im, ssm_d, ssm_log_dt, ssm_glu_w, q_norm_g, k_norm_g, lam_q1, lam_k1, lam_q2, lam_k2, subln_g, w_out, ln2_g, mem_norm_g, ca_wq, ca_wk, ca_wv, ca_q_norm_g, ca_k_norm_g, ca_wo, ln3_g, ffn_wg, ffn_wv, ffn_conv_w, ffn_conv_b, ffn_wd):
    params = dict(
        ln1_g=ln1_g, w_in=w_in, ssm_a_re=ssm_a_re, ssm_a_im=ssm_a_im, ssm_b_re=ssm_b_re,
        ssm_b_im=ssm_b_im, ssm_c_re=ssm_c_re, ssm_c_im=ssm_c_im, ssm_d=ssm_d,
        ssm_log_dt=ssm_log_dt, ssm_glu_w=ssm_glu_w, q_norm_g=q_norm_g, k_norm_g=k_norm_g,
        lam_q1=lam_q1, lam_k1=lam_k1, lam_q2=lam_q2, lam_k2=lam_k2, subln_g=subln_g,
        w_out=w_out, ln2_g=ln2_g, mem_norm_g=mem_norm_g, ca_wq=ca_wq, ca_wk=ca_wk, ca_wv=ca_wv,
        ca_q_norm_g=ca_q_norm_g, ca_k_norm_g=ca_k_norm_g, ca_wo=ca_wo, ln3_g=ln3_g,
        ffn_wg=ffn_wg, ffn_wv=ffn_wv, ffn_conv_w=ffn_conv_w, ffn_conv_b=ffn_conv_b, ffn_wd=ffn_wd)
    y_p, y_s = x_prompt, x_sample
    per_layer = []
    for l in range(w_in.shape[0]):
        p = {k: v[l] for k, v in params.items()}
        outs = _layer(l, y_p, y_s, mem_prompt, cache_k[l], cache_v[l], page_table,
                      state_ssm_re[l], state_ssm_im[l], state_conv[l],
                      cache_mem_k[l], cache_mem_v[l], p)
        y_p, y_s = outs[0], outs[1]
        per_layer.append(outs[2:])
    return (y_p, y_s) + tuple(jnp.stack(leaf) for leaf in zip(*per_layer))
```

```python
import functools
import math

import jax
import jax.numpy as jnp
from jax import lax
from jax.experimental import pallas as pl
from jax.experimental.pallas import tpu as pltpu

F32 = jnp.float32
BF16 = jnp.bfloat16

D_MODEL = 1024
SSM_WIDTH = 512
SSM_GROUP_CH = 16
SSM_GROUPS = 32
SSM_STATE = 64
N_STATE = SSM_GROUPS * SSM_STATE
ATTN_WIDTH = 512
ATTN_HEAD_DIM = 64
ATTN_V_DIM = 128
ATTN_HEADS = 4
PAGE_SIZE = 128
N_MEM = 256
CA_HEADS = 4
CA_HEAD_DIM = 256
FFN_HIDDEN = 2816
CONV_WIDTH = 3
NORM_EPS = 1e-6

LANES = 128
SUBLANES = 8
VMEM_LIMIT = 56 * 1024 * 1024
VMEM_LIMIT_RIDER = 59 * 1024 * 1024

FFN_TILE = 512
OUT_PROJ_TILE = 1024
IN_PROJ_TILE = 1024
MEM_TILE = 256
SSM_CHUNK = 256
ATTN_BLOCK = 512
LOG2E = math.log2(math.e)

def _cparams(n_axes, vmem_limit=VMEM_LIMIT):
    return pltpu.CompilerParams(
        dimension_semantics=("arbitrary",) * n_axes,
        vmem_limit_bytes=vmem_limit)


def _const_spec(shape):
    nd = len(shape)
    return pl.BlockSpec(shape, lambda *_: (0,) * nd, pipeline_mode=pl.Buffered(1))


def _rms(x, g):
    return x * lax.rsqrt(jnp.mean(x * x, axis=-1, keepdims=True) + NORM_EPS) * g


def _dot(a, b):
    return jnp.dot(a, b, preferred_element_type=F32)


def _dot_nt(a, b):
    return lax.dot_general(a, b, (((1,), (1,)), ((), ())), preferred_element_type=F32)


def _head_rms(x, g, width, scale):
    outs = []
    for h in range(x.shape[-1] // width):
        c = x[:, h * width:(h + 1) * width]
        outs.append(_rms(c, g) * scale)
    return jnp.concatenate(outs, axis=-1)


def _prep_kernel(are_ref, aim_ref, ldt_ref, bre_ref, bim_ref,
                 q1_ref, k1_ref, q2_ref, k2_ref,
                 lbre_ref, lbim_ref, bbre_ref, bbim_ref, lam_ref, *, lam0):
    a_re = are_ref[...]
    a_im = aim_ref[...]
    dt = jnp.exp(ldt_ref[...])
    mag = jnp.exp(a_re * dt)
    lb_re = mag * jnp.cos(a_im * dt)
    lb_im = mag * jnp.sin(a_im * dt)
    den = a_re * a_re + a_im * a_im
    n_re = lb_re - 1.0
    f_re = (n_re * a_re + lb_im * a_im) / den
    f_im = (lb_im * a_re - n_re * a_im) / den
    b_re = bre_ref[...]
    b_im = bim_ref[...]
    lbre_ref[...] = lb_re
    lbim_ref[...] = lb_im
    bbre_ref[...] = f_re * b_re - f_im * b_im
    bbim_ref[...] = f_re * b_im + f_im * b_re
    s1 = jnp.sum(q1_ref[...] * k1_ref[...], axis=-1, keepdims=True)
    s2 = jnp.sum(q2_ref[...] * k2_ref[...], axis=-1, keepdims=True)
    lam_ref[...] = jnp.exp(s1) - jnp.exp(s2) + lam0


def _prep(a_re, a_im, log_dt, b_re, b_im, q1, k1, q2, k2, lam0):
    are = a_re.reshape(1, N_STATE)
    aim = a_im.reshape(1, N_STATE)
    ldt = jnp.repeat(log_dt, SSM_STATE).reshape(1, N_STATE)
    bre = b_re.transpose(2, 0, 1).reshape(SSM_GROUP_CH, N_STATE)
    bim = b_im.transpose(2, 0, 1).reshape(SSM_GROUP_CH, N_STATE)
    vec = lambda v: v.reshape(1, ATTN_HEAD_DIM)
    row = jax.ShapeDtypeStruct((1, N_STATE), F32)
    mat = jax.ShapeDtypeStruct((SSM_GROUP_CH, N_STATE), F32)
    return pl.pallas_call(
        functools.partial(_prep_kernel, lam0=lam0),
        out_shape=(row, row, mat, mat, jax.ShapeDtypeStruct((1, 1), F32)),
        name="prep",
    )(are, aim, ldt, bre, bim, vec(q1), vec(k1), vec(q2), vec(k2))


def _in_proj_kernel(x_ref, g_ref, w_ref, wk_ref, seg_ref, qg_ref, kg_ref,
                    u_ref, q_ref, k_ref, v_ref, *bf_refs, key_major, blk):
    xn = _rms(x_ref[...], g_ref[...]).astype(BF16)
    proj = _dot(xn, w_ref[...])
    seg = seg_ref[...]

    def chunk_norm(z, g):
        ms = _dot((z * z).astype(BF16), seg) * (1.0 / ATTN_HEAD_DIM)
        return z * lax.rsqrt(ms + NORM_EPS) * g

    u_ref[...] = proj[:, :SSM_WIDTH]
    q = proj[:, SSM_WIDTH:SSM_WIDTH + ATTN_WIDTH]
    v = proj[:, SSM_WIDTH + ATTN_WIDTH:]
    q_ref[...] = (chunk_norm(q, qg_ref[...]) * (ATTN_HEAD_DIM ** -0.5 * LOG2E)).astype(q_ref.dtype)
    if key_major:
        kb_ref, vb_ref = bf_refs
        tm = x_ref.shape[0]
        z = _dot_nt(wk_ref[...], xn).reshape(ATTN_WIDTH // ATTN_HEAD_DIM, ATTN_HEAD_DIM, tm)
        ms = jnp.mean(z * z, axis=1, keepdims=True)
        kn = (z * lax.rsqrt(ms + NORM_EPS) * kg_ref[...]).reshape(ATTN_WIDTH, tm)
        k_ref[0] = kn
        for i in range(tm // blk):
            kb_ref[0, i] = kn[:, i * blk:(i + 1) * blk].astype(BF16)
        for h in range(ATTN_HEADS):
            v_ref[:, h, :] = v[:, ATTN_V_DIM * h:ATTN_V_DIM * (h + 1)]
        vb_ref[...] = v.astype(BF16)
    else:
        k_ref[...] = chunk_norm(_dot(xn, wk_ref[...]), kg_ref[...])
        v_ref[...] = v


def _in_proj(x, ln1_g, w_uqv, wk, seg, qg, kg, tm, q_dtype, n_batch=None, blk=None):
    n = x.shape[0]
    key_major = n_batch is not None
    bsz = n_batch if key_major else 1
    per_b = n // bsz // tm
    tok = lambda width: pl.BlockSpec((tm, width), lambda b, i: (b * per_b + i, 0))
    out_specs = [tok(SSM_WIDTH), tok(ATTN_WIDTH)]
    out_shape = [jax.ShapeDtypeStruct((n, SSM_WIDTH), F32),
                 jax.ShapeDtypeStruct((n, ATTN_WIDTH), q_dtype)]
    if key_major:
        t = n // bsz
        out_specs += [pl.BlockSpec((1, ATTN_WIDTH, tm), lambda b, i: (b, 0, i)),
                      pl.BlockSpec((tm, ATTN_HEADS, ATTN_V_DIM), lambda b, i: (b * per_b + i, 0, 0)),
                      pl.BlockSpec((1, tm // blk, ATTN_WIDTH, blk), lambda b, i: (b, i, 0, 0)),
                      tok(ATTN_WIDTH)]
        out_shape += [jax.ShapeDtypeStruct((bsz, ATTN_WIDTH, t), F32),
                      jax.ShapeDtypeStruct((n, ATTN_HEADS, ATTN_V_DIM), F32),
                      jax.ShapeDtypeStruct((bsz, t // blk, ATTN_WIDTH, blk), BF16),
                      jax.ShapeDtypeStruct((n, ATTN_WIDTH), BF16)]
    else:
        out_specs += [tok(ATTN_WIDTH), tok(ATTN_WIDTH)]
        out_shape += [jax.ShapeDtypeStruct((n, ATTN_WIDTH), F32)] * 2
    return pl.pallas_call(
        functools.partial(_in_proj_kernel, key_major=key_major, blk=blk),
        grid=(bsz, per_b),
        in_specs=[tok(D_MODEL), _const_spec((1, D_MODEL)), _const_spec(w_uqv.shape),
                  _const_spec(wk.shape), _const_spec((ATTN_WIDTH, ATTN_WIDTH)),
                  _const_spec((1, ATTN_WIDTH)), _const_spec(kg.shape)],
        out_specs=out_specs,
        out_shape=out_shape,
        compiler_params=_cparams(2),
        name="in_proj",
    )(x, ln1_g, w_uqv, wk, seg, qg, kg)


N_SLABS = N_STATE // 2 // LANES


def _gelu_tanh(x):
    c = math.sqrt(2.0 / math.pi)
    return 0.5 * x * (1.0 + jnp.tanh(c * (x + 0.044715 * (x * x * x))))


def _ssm_kernel(u_ref, bre_ref, bim_ref, cre_ref, cim_ref, lre_ref, lim_ref,
                d_ref, glu_ref, h0re_ref, h0im_ref,
                out_ref, hre_ref, him_ref, sre, sim,
                *, n_seq, tc, pitch, n_bulk, rows_bulk, pitch_bulk):
    c = pl.program_id(0)
    rows = n_seq * tc
    n_groups = 2 * n_seq // SUBLANES

    @pl.when(c == 0)
    def _():
        hre_ref[...] = h0re_ref[...]
        him_ref[...] = h0im_ref[...]

    u = u_ref[...].reshape(rows, SSM_WIDTH)
    ub = u.astype(BF16)

    for j in range(4):
        hh, jj = divmod(j, 2)
        uj = ub[:, LANES * j:LANES * (j + 1)]
        for src, dst in ((bre_ref, sre), (bim_ref, sim)):
            bu = _dot(uj, src[j])
            for kk in range(4):
                slab = 4 * jj + kk
                for bb in range(n_bulk):
                    r0 = (hh * n_bulk + bb) * pitch_bulk
                    dst[slab, r0:r0 + rows_bulk, :] = (
                        bu[bb * rows_bulk:(bb + 1) * rows_bulk, LANES * kk:LANES * (kk + 1)])

    def group_body(g, carry):
        base = g * (SUBLANES * pitch)
        lr = [lre_ref[g, k] for k in range(N_SLABS)]
        li = [lim_ref[g, k] for k in range(N_SLABS)]
        hr0 = tuple(hre_ref[g, k] for k in range(N_SLABS))
        hi0 = tuple(him_ref[g, k] for k in range(N_SLABS))

        def step(t, hc):
            hr, hi = hc
            idx = pl.ds(base + t, SUBLANES, stride=pitch)
            nr, ni = [], []
            for k in range(N_SLABS):
                br = sre[k, idx, :]
                bi = sim[k, idx, :]
                r = lr[k] * hr[k] - li[k] * hi[k] + br
                i = lr[k] * hi[k] + li[k] * hr[k] + bi
                sre[k, idx, :] = r
                sim[k, idx, :] = i
                nr.append(r)
                ni.append(i)
            return tuple(nr), tuple(ni)

        hr, hi = lax.fori_loop(0, tc, step, (hr0, hi0), unroll=min(tc, 4))
        for k in range(N_SLABS):
            hre_ref[g, k] = hr[k]
            him_ref[g, k] = hi[k]
        return carry

    lax.fori_loop(0, n_groups, group_body, 0)

    y_rows = []
    for bb in range(n_bulk):
        y_cols = []
        for j in range(4):
            hh, jj = divmod(j, 2)
            r0 = (hh * n_bulk + bb) * pitch_bulk
            hr = jnp.concatenate(
                [sre[4 * jj + kk, r0:r0 + rows_bulk, :] for kk in range(4)], axis=1).astype(BF16)
            hi = jnp.concatenate(
                [sim[4 * jj + kk, r0:r0 + rows_bulk, :] for kk in range(4)], axis=1).astype(BF16)
            y_cols.append(_dot(hr, cre_ref[j]) - _dot(hi, cim_ref[j]))
        y_rows.append(jnp.concatenate(y_cols, axis=1))
    y = jnp.concatenate(y_rows, axis=0) if n_bulk > 1 else y_rows[0]
    y = y + d_ref[...] * u
    gl = _gelu_tanh(y)
    gate = jax.nn.sigmoid(_dot(gl.astype(BF16), glu_ref[...]))
    out_ref[...] = (gl * gate).astype(out_ref.dtype).reshape(out_ref.shape)


def _ssm(u3, mats, lam_g, d_row, glu_b, h0, n_seq, tc):
    bre, bim, cre, cim = mats
    lre_g, lim_g = lam_g
    h0re, h0im = h0
    n_groups = 2 * n_seq // SUBLANES
    t_total = u3.shape[0] * u3.shape[1] // n_seq
    n_chunks = t_total // tc
    if tc % SUBLANES == 0:
        pitch = tc + SUBLANES
        n_bulk, rows_bulk, pitch_bulk = n_seq, tc, pitch
        ublock = (n_seq, tc, SSM_WIDTH)
    else:
        assert n_chunks == 1
        pitch = tc
        n_bulk, rows_bulk, pitch_bulk = 1, n_seq * tc, n_seq * tc
        ublock = (1, n_seq * tc, SSM_WIDTH)
    scr_rows = 2 * n_bulk * pitch_bulk
    hshape = (n_groups, N_SLABS, SUBLANES, LANES)
    kern = functools.partial(_ssm_kernel, n_seq=n_seq, tc=tc, pitch=pitch,
                             n_bulk=n_bulk, rows_bulk=rows_bulk, pitch_bulk=pitch_bulk)
    return pl.pallas_call(
        kern,
        grid=(n_chunks,),
        in_specs=[pl.BlockSpec(ublock, lambda c: (0, c, 0)),
                  _const_spec(bre.shape), _const_spec(bim.shape),
                  _const_spec(cre.shape), _const_spec(cim.shape),
                  _const_spec(hshape), _const_spec(hshape),
                  _const_spec((1, SSM_WIDTH)), _const_spec((SSM_WIDTH, SSM_WIDTH)),
                  _const_spec(hshape), _const_spec(hshape)],
        out_specs=[pl.BlockSpec(ublock, lambda c: (0, c, 0)),
                   _const_spec(hshape), _const_spec(hshape)],
        out_shape=[jax.ShapeDtypeStruct(u3.shape, BF16),
                   jax.ShapeDtypeStruct(hshape, F32),
                   jax.ShapeDtypeStruct(hshape, F32)],
        scratch_shapes=[pltpu.VMEM((N_SLABS, scr_rows, LANES), F32),
                        pltpu.VMEM((N_SLABS, scr_rows, LANES), F32)],
        compiler_params=_cparams(1),
        name="ssm",
    )(u3, bre, bim, cre, cim, lre_g, lim_g, d_row, glu_b, h0re, h0im)


def _state_to_groups(h, n_seq):
    x = h.reshape(n_seq, 2, N_SLABS, LANES).transpose(1, 0, 2, 3)
    x = x.reshape(2 * n_seq // SUBLANES, SUBLANES, N_SLABS, LANES)
    return x.transpose(0, 2, 1, 3)


def _groups_to_state(x, n_seq):
    x = x.transpose(0, 2, 1, 3).reshape(2, n_seq, N_SLABS, LANES)
    return x.transpose(1, 0, 2, 3).reshape(n_seq, SSM_GROUPS, SSM_STATE)


def _head_slope(h):
    return jnp.where(h == 0, 2.0 ** -2, jnp.where(h == 1, 2.0 ** -4,
                     jnp.where(h == 2, 2.0 ** -6, 2.0 ** -8))).astype(F32)


def _attn_prompt_kernel(lam_ref, q_ref, k_ref, v_ref, sg_ref, o_ref,
                        q_scr, s_scr, m_scr, l_scr, acc_scr, *, bq, bk, out_scale, rider=None):
    ride = lambda: next(rider, None) if rider is not None else None
    ride()
    h = pl.program_id(1)
    qi = pl.program_id(2)
    slope = _head_slope(h) * LOG2E
    lam = lam_ref[0, 0]
    n_maps = 2

    q = q_ref[0]
    lane = lax.broadcasted_iota(jnp.int32, q.shape, 1)
    zero = jnp.zeros_like(q)
    q_scr[0] = jnp.where(lane < ATTN_HEAD_DIM, q, zero)
    q_scr[1] = jnp.where(lane >= ATTN_HEAD_DIM, q, zero)

    col = lax.broadcasted_iota(jnp.int32, (1, bk), 1)
    q0 = qi * bq

    def lane_tiles(x):
        return [x[:, LANES * i:LANES * (i + 1)] for i in range(bk // LANES)]

    n_full = q0 // bk

    def run_pairs(first, count, step):
        odd = lax.rem(count, 2)

        @pl.when(odd == 1)
        def _():
            step(first)

        def body(t, carry):
            j = first + odd + 2 * t
            step(j)
            step(j + 1)
            return carry

        lax.fori_loop(0, count // 2, body, 0)

    def scores(j, i, masked):
        kpos = col + j * bk
        s = _dot(q_scr[i], k_ref[0, j]) + slope * (kpos - q0).astype(F32)
        if masked:
            qpos = lax.broadcasted_iota(jnp.int32, (bq, bk), 0) + q0
            s = jnp.where(qpos >= kpos, s, -jnp.inf)
        return s

    def accumulate(j, i, p):
        start = pl.multiple_of(j * bk, bk)
        lpart = l_scr[i]
        for piece in lane_tiles(p):
            lpart = lpart + piece
        l_scr[i] = lpart
        acc_scr[i] += _dot(p.astype(BF16), v_ref[0, pl.ds(start, bk), :])

    def score_step(j, masked=False):
        for i in range(n_maps):
            s = scores(j, i, masked)
            s_scr[j, i] = s
            mpart = m_scr[i]
            for piece in lane_tiles(s):
                mpart = jnp.maximum(mpart, piece)
            m_scr[i] = mpart

    m_scr[...] = jnp.full(m_scr.shape, -jnp.inf, F32)
    score_step(n_full, masked=True)
    ride()
    run_pairs(0, n_full, score_step)
    ms = [jnp.max(m_scr[i], axis=-1, keepdims=True) for i in range(n_maps)]
    ride()

    def sum_step(j):
        for i in range(n_maps):
            accumulate(j, i, jnp.exp2(s_scr[j, i] - ms[i]))

    l_scr[...] = jnp.zeros(l_scr.shape, F32)
    acc_scr[...] = jnp.zeros(acc_scr.shape, F32)
    run_pairs(0, n_full + 1, sum_step)

    outs = [acc_scr[i] / jnp.sum(l_scr[i], axis=-1, keepdims=True) for i in range(n_maps)]
    o = outs[0] - lam * outs[1]
    o_ref[0] = (_rms(o, sg_ref[...]) * out_scale).astype(o_ref.dtype)
    ride()


def _attn_sample_steps(lam_ref, q_ref, kn_ref, vn_ref, sg_ref, k_refs, v_refs, o_ref,
                       *, t_new, past_len, out_scale):
    n_pages = len(k_refs)
    lam = lam_ref[0, 0]
    n_rows = 2 * ATTN_HEADS * t_new

    q4 = q_ref[0]
    qe = jnp.concatenate([q4] * (2 * ATTN_HEADS), axis=0)
    row = lax.broadcasted_iota(jnp.int32, qe.shape, 0)
    lane = lax.broadcasted_iota(jnp.int32, qe.shape, 1)
    qe = jnp.where(lane // ATTN_HEAD_DIM == row // t_new, qe, 0.0)
    qeb = qe.astype(BF16)

    rcol = lax.broadcasted_iota(jnp.int32, (n_rows, 1), 0)
    head_c = rcol // (2 * t_new)
    tok_c = rcol % t_new
    slope = _head_slope(head_c) * LOG2E
    rows_h = 2 * t_new

    kt = jnp.concatenate([k_refs[p][...].astype(BF16) for p in range(n_pages)], axis=1)
    kpos = lax.broadcasted_iota(jnp.int32, (1, n_pages * PAGE_SIZE), 1) - past_len
    s = _dot(qeb, kt) + slope * kpos.astype(F32)

    kn = kn_ref[0]
    vn = vn_ref[0]
    s_new = []
    for c in range(t_new):
        sc = jnp.sum(qe * kn[c:c + 1, :], axis=-1, keepdims=True) + slope * float(c)
        s_new.append(jnp.where(tok_c >= c, sc, -jnp.inf))

    yield
    m = jnp.max(s, axis=-1, keepdims=True)
    for sc in s_new:
        m = jnp.maximum(m, sc)
    pe = jnp.exp2(s - m)
    l = jnp.sum(pe, axis=-1, keepdims=True)

    pv = []
    for h in range(ATTN_HEADS):
        vh = jnp.concatenate(
            [v_refs[p][pl.ds(h, PAGE_SIZE, stride=ATTN_HEADS), :].astype(BF16)
             for p in range(n_pages)], axis=0)
        pv.append(_dot(pe[rows_h * h:rows_h * (h + 1)].astype(BF16), vh))
    acc = jnp.concatenate(pv, axis=0)
    yield

    for c in range(t_new):
        pc = jnp.exp2(s_new[c] - m)
        l = l + pc
        vc = jnp.concatenate(
            [jnp.broadcast_to(vn[c:c + 1, ATTN_V_DIM * h:ATTN_V_DIM * (h + 1)], (rows_h, ATTN_V_DIM))
             for h in range(ATTN_HEADS)], axis=0)
        acc = acc + pc * vc

    acc = acc / l
    outs = []
    for h in range(ATTN_HEADS):
        a0 = acc[rows_h * h:rows_h * h + t_new]
        a1 = acc[rows_h * h + t_new:rows_h * (h + 1)]
        outs.append(_rms(a0 - lam * a1, sg_ref[...]) * out_scale)
    o_ref[0] = jnp.concatenate(outs, axis=-1)


def _attention_kernel(pt_ref, lam_ref, sg_ref, q_ref, k_ref, v_ref, qs_ref, kn_ref, vn_ref,
                      kc_ref, vc_ref, o_ref, os_ref, kbuf, vbuf, sem, *scratch,
                      n_pages, t_new, past_len, bq, bk, out_scale):
    n_steps = pl.num_programs(0) * pl.num_programs(1) * pl.num_programs(2)
    step = ((pl.program_id(0) * pl.num_programs(1) + pl.program_id(1)) * pl.num_programs(2)
            + pl.program_id(2))

    def page_copies(seq_idx, slot):
        copies = []
        for p in range(n_pages):
            page = pt_ref[seq_idx * n_pages + p]
            copies.append(pltpu.make_async_copy(kc_ref.at[page], kbuf.at[slot, p], sem.at[slot]))
            copies.append(pltpu.make_async_copy(vc_ref.at[page], vbuf.at[slot, p], sem.at[slot]))
        return copies

    @pl.when(step == 0)
    def _():
        for copy in page_copies(0, 0):
            copy.start()

    @pl.when(step + 1 < n_steps)
    def _():
        for copy in page_copies(step + 1, lax.rem(step + 1, 2)):
            copy.start()

    slot = lax.rem(step, 2)
    for copy in page_copies(step, slot):
        copy.wait()

    k_refs = [kbuf.at[slot, p] for p in range(n_pages)]
    v_refs = [vbuf.at[slot, p] for p in range(n_pages)]
    sample = _attn_sample_steps(lam_ref, qs_ref, kn_ref, vn_ref, sg_ref, k_refs, v_refs, os_ref,
                                t_new=t_new, past_len=past_len, out_scale=out_scale)
    _attn_prompt_kernel(lam_ref, q_ref, k_ref, v_ref, sg_ref, o_ref, *scratch,
                        bq=bq, bk=bk, out_scale=out_scale, rider=sample)


def _attention(page_table, lam, sg, qb, kb, vb, q3, kn3, vn3, cache_kt, cache_v3,
               bq, past_len, out_scale):
    bsz, t, _ = qb.shape
    bk = kb.shape[-1]
    nq = t // bq
    n_dec, t_new, _ = q3.shape
    n_pages = page_table.shape[1]
    assert n_dec == bsz * ATTN_HEADS * nq, "one sample sequence per prompt grid step"
    pt = page_table.reshape(-1)

    def seq(b, h, i):
        return (b * ATTN_HEADS + h) * nq + i

    tok_spec = pl.BlockSpec((1, t_new, ATTN_WIDTH), lambda b, h, i, pt: (seq(b, h, i), 0, 0))

    head_spec = pl.BlockSpec((1, bq, ATTN_V_DIM), lambda b, h, i, pt: (b, i, h))
    grid_spec = pltpu.PrefetchScalarGridSpec(
        num_scalar_prefetch=1,
        grid=(bsz, ATTN_HEADS, nq),
        in_specs=[pl.BlockSpec(memory_space=pltpu.SMEM), _const_spec((1, ATTN_V_DIM)),
                  head_spec,
                  pl.BlockSpec((1, t // bk, ATTN_V_DIM, bk), lambda b, h, i, pt: (b, 0, h, 0)),
                  pl.BlockSpec((1, t, ATTN_V_DIM), lambda b, h, i, pt: (b, 0, h)),
                  tok_spec, tok_spec, tok_spec,
                  pl.BlockSpec(memory_space=pl.ANY), pl.BlockSpec(memory_space=pl.ANY)],
        out_specs=[head_spec, tok_spec],
        scratch_shapes=[pltpu.VMEM((2, n_pages) + cache_kt.shape[1:], F32),
                        pltpu.VMEM((2, n_pages) + cache_v3.shape[1:], F32),
                        pltpu.SemaphoreType.DMA((2,)),
                        pltpu.VMEM((2, bq, ATTN_V_DIM), BF16),
                        pltpu.VMEM((t // bk, 2, bq, bk), F32),
                        pltpu.VMEM((2, bq, LANES), F32),
                        pltpu.VMEM((2, bq, LANES), F32),
                        pltpu.VMEM((2, bq, ATTN_V_DIM), F32)])
    kern = functools.partial(_attention_kernel, n_pages=n_pages, t_new=t_new, past_len=past_len,
                             bq=bq, bk=bk, out_scale=out_scale)
    return pl.pallas_call(
        kern,
        grid_spec=grid_spec,
        out_shape=[jax.ShapeDtypeStruct(qb.shape, BF16), jax.ShapeDtypeStruct(q3.shape, F32)],
        compiler_params=_cparams(3),
        name="attention",
    )(pt, lam, sg, qb, kb, vb, q3, kn3, vn3, cache_kt, cache_v3)


def _memory_attention(cq, mk_ref, mv_ref):
    outs = []
    for h in range(CA_HEADS):
        sl = slice(CA_HEAD_DIM * h, CA_HEAD_DIM * (h + 1))
        s = _dot_nt(cq[:, sl], mk_ref[0, :, sl])
        p = jnp.exp(s - jnp.max(s, axis=-1, keepdims=True))
        l = jnp.sum(p, axis=-1, keepdims=True)
        outs.append(_dot((p / l).astype(BF16), mv_ref[0, :, sl]))
    return jnp.concatenate(outs, axis=-1)


def _out_proj_kernel(x_ref, s_ref, o_ref, w1_ref, w2_ref, g2_ref, wq_ref, qg_ref, *refs):
    x1 = (x_ref[...] + _dot(s_ref[...].astype(BF16), w1_ref[...])
          + _dot(o_ref[...].astype(BF16), w2_ref[...]))
    xn = _rms(x1, g2_ref[...]).astype(BF16)
    cq = _head_rms(_dot(xn, wq_ref[...]), qg_ref[...], CA_HEAD_DIM, CA_HEAD_DIM ** -0.5)
    if len(refs) == 4:
        mk_ref, mv_ref, x1_ref, out_ref = refs
        out_ref[...] = _memory_attention(cq.astype(BF16), mk_ref, mv_ref).astype(out_ref.dtype)
    else:
        x1_ref, out_ref = refs
        out_ref[...] = cq.astype(out_ref.dtype)
    x1_ref[...] = x1


def _out_proj(x, ssm_o, att_o, w1, w2, g2, wq, qg, tm, out_dtype, mem=None):
    n = x.shape[0]
    bsz = 1 if mem is None else mem[0].shape[0]
    per_b = n // bsz // tm
    tok = lambda width: pl.BlockSpec((tm, width), lambda b, i: (b * per_b + i, 0))
    mem_spec = pl.BlockSpec((1, N_MEM, D_MODEL), lambda b, i: (b, 0, 0))
    return pl.pallas_call(
        _out_proj_kernel,
        grid=(bsz, per_b),
        in_specs=[tok(D_MODEL), tok(SSM_WIDTH), tok(ATTN_WIDTH),
                  _const_spec((SSM_WIDTH, D_MODEL)), _const_spec((ATTN_WIDTH, D_MODEL)),
                  _const_spec((1, D_MODEL)), _const_spec((D_MODEL, D_MODEL)),
                  _const_spec((1, CA_HEAD_DIM))] + ([] if mem is None else [mem_spec, mem_spec]),
        out_specs=[tok(D_MODEL), tok(D_MODEL)],
        out_shape=[jax.ShapeDtypeStruct((n, D_MODEL), F32),
                   jax.ShapeDtypeStruct((n, D_MODEL), out_dtype)],
        compiler_params=_cparams(2),
        name="out_proj",
    )(x, ssm_o, att_o, w1, w2, g2, wq, qg, *(() if mem is None else mem))


def _mem_kv_kernel(m_ref, g_ref, wk_ref, wv_ref, kg_ref, mk_ref, mv_ref, mkb_ref, mvb_ref):
    mn = _rms(m_ref[...], g_ref[...]).astype(BF16)
    mk = _head_rms(_dot(mn, wk_ref[...]), kg_ref[...], CA_HEAD_DIM, 1.0)
    mv = _dot(mn, wv_ref[...])
    mk_ref[...] = mk
    mv_ref[...] = mv
    mkb_ref[...] = mk.astype(BF16)
    mvb_ref[...] = mv.astype(BF16)


def _mem_kv(mem, g, wk, wv, kg, tm):
    n = mem.shape[0]
    tok = pl.BlockSpec((tm, D_MODEL), lambda i: (i, 0))
    wspec = _const_spec((D_MODEL, D_MODEL))
    return pl.pallas_call(
        _mem_kv_kernel,
        grid=(n // tm,),
        in_specs=[tok, _const_spec((1, D_MODEL)), wspec, wspec, _const_spec((1, CA_HEAD_DIM))],
        out_specs=[tok, tok, tok, tok],
        out_shape=[jax.ShapeDtypeStruct((n, D_MODEL), F32)] * 2
                  + [jax.ShapeDtypeStruct((n, D_MODEL), BF16)] * 2,
        compiler_params=_cparams(1),
        name="mem_kv",
    )(mem, g, wk, wv, kg)


def _cross_sample_one(cq_ref, mk_ref, mv_ref, co_ref, b, t_new):
    halves = CA_HEAD_DIM // LANES
    per_key = halves * CA_HEADS

    def heads(ref):
        return jnp.concatenate(
            [ref[b, pl.ds(half * CA_HEADS + h, N_MEM, stride=per_key), :]
             for h in range(CA_HEADS) for half in range(halves)], axis=1).astype(BF16)

    qe = jnp.concatenate([cq_ref[b]] * CA_HEADS, axis=0)
    row = lax.broadcasted_iota(jnp.int32, qe.shape, 0)
    lane = lax.broadcasted_iota(jnp.int32, qe.shape, 1)
    qe = jnp.where(lane // CA_HEAD_DIM == row // t_new, qe, 0.0).astype(BF16)
    s = _dot_nt(qe, heads(mk_ref))
    yield
    p = jnp.exp(s - jnp.max(s, axis=-1, keepdims=True))
    l = jnp.sum(p, axis=-1, keepdims=True)
    full = _dot((p / l).astype(BF16), heads(mv_ref))
    outs = [full[h * t_new:(h + 1) * t_new, CA_HEAD_DIM * h:CA_HEAD_DIM * (h + 1)]
            for h in range(CA_HEADS)]
    co_ref[b] = jnp.concatenate(outs, axis=-1)


def _mem_rows(mem):
    n = mem.shape[0]
    halves = CA_HEAD_DIM // LANES
    return (mem.reshape(n, N_MEM, CA_HEADS, halves, LANES).transpose(0, 1, 3, 2, 4)
            .reshape(n, N_MEM * halves * CA_HEADS, LANES))


def _ffn_kernel(x_ref, co_ref, wo_ref, g3_ref, wg_ref, wv_ref, cw_ref, cb_ref, wd_ref, prev_ref,
                *refs, tm, shift, pad, t_new):
    if len(refs) == 7:
        cq_ref, mk_ref, mv_ref, y_ref, cs_ref, cos_ref, hbuf = refs
    else:
        y_ref, cs_ref, hbuf = refs
    i = pl.program_id(1)

    @pl.when(i == 0)
    def _():
        hbuf[pad - 2 * shift:pad, :] = prev_ref[0]

    riders = ([_cross_sample_one(cq_ref, mk_ref, mv_ref, cos_ref, b, t_new)
               for b in range(cq_ref.shape[0])] if len(refs) == 7 else [])
    started = []

    def ride():
        if started:
            next(started.pop(), None)
        if riders:
            started.append(riders.pop(0))
            next(started[-1])

    ride()
    x2 = x_ref[...] + _dot(co_ref[...].astype(BF16), wo_ref[...])
    xn = _rms(x2, g3_ref[...]).astype(BF16)
    ride()
    hbuf[pad:pad + tm, :] = _dot(xn, wg_ref[...])
    ride()
    hv = _dot(xn, wv_ref[...])
    ride()
    cw = cw_ref[...]
    conv = (cb_ref[...] + cw[0:1, :] * hbuf[pad - 2 * shift:pad - 2 * shift + tm, :]
            + cw[1:2, :] * hbuf[pad - shift:pad - shift + tm, :]
            + cw[2:3, :] * hbuf[pad:pad + tm, :])
    act = (conv * jax.nn.sigmoid(conv) * hv).astype(BF16)
    y_ref[...] = x2 + _dot(act, wd_ref[...])
    while started or riders:
        ride()
    last = hbuf[pad + tm - 2 * shift:pad + tm, :]
    cs_ref[0] = last
    hbuf[pad - 2 * shift:pad, :] = last


def _ffn(x1, co, wo, g3, wg, wv, cw, cb, wd, prev, tm, shift, rider=None):
    n = x1.shape[0]
    n_seq_blocks = prev.shape[0]
    per_b = n // n_seq_blocks // tm
    pad = -(-2 * shift // SUBLANES) * SUBLANES
    tok = pl.BlockSpec((tm, D_MODEL), lambda b, i: (b * per_b + i, 0))
    state = pl.BlockSpec((1, 2 * shift, FFN_HIDDEN), lambda b, i: (b, 0, 0))
    in_specs = [tok, tok, _const_spec((D_MODEL, D_MODEL)), _const_spec((1, D_MODEL)),
                _const_spec((D_MODEL, FFN_HIDDEN)), _const_spec((D_MODEL, FFN_HIDDEN)),
                _const_spec((CONV_WIDTH, FFN_HIDDEN)), _const_spec((1, FFN_HIDDEN)),
                _const_spec((FFN_HIDDEN, D_MODEL)), state]
    out_specs = [tok, state]
    out_shape = [jax.ShapeDtypeStruct((n, D_MODEL), F32), jax.ShapeDtypeStruct(prev.shape, F32)]
    operands = (x1, co, wo, g3, wg, wv, cw, cb, wd, prev)
    t_new = None
    if rider is not None:
        cq3, mem_k, mem_v = rider
        n_dec, t_new, _ = cq3.shape
        per_step = n_dec // (n_seq_blocks * per_b)
        assert per_step * n_seq_blocks * per_b == n_dec
        seqs = pl.BlockSpec((per_step, t_new, D_MODEL), lambda b, i: (b * per_b + i, 0, 0))
        mem = pl.BlockSpec((per_step,) + mem_k.shape[1:], lambda b, i: (b * per_b + i, 0, 0))
        in_specs += [seqs, mem, mem]
        out_specs.append(seqs)
        out_shape.append(jax.ShapeDtypeStruct(cq3.shape, F32))
        operands += rider
    return pl.pallas_call(
        functools.partial(_ffn_kernel, tm=tm, shift=shift, pad=pad, t_new=t_new),
        grid=(n_seq_blocks, per_b),
        in_specs=in_specs,
        out_specs=out_specs,
        out_shape=out_shape,
        scratch_shapes=[pltpu.VMEM((pad + tm, FFN_HIDDEN), F32)],
        compiler_params=_cparams(2, VMEM_LIMIT if rider is None else VMEM_LIMIT_RIDER),
        name="ffn",
    )(*operands)


def _block_diag(blocks, n_outer):
    n, r, c = blocks.shape
    per = n // n_outer
    b = blocks.reshape(n_outer, per, r, c)
    eye = jnp.eye(per, dtype=blocks.dtype)
    full = b[:, :, :, None, :] * eye[None, :, None, :, None]
    return full.reshape(n_outer, per * r, per * c)


def _lam_groups(lb, n_seq):
    halves = lb.reshape(2, 1, N_SLABS, LANES)
    rows = jnp.broadcast_to(halves, (2, n_seq, N_SLABS, LANES))
    rows = rows.reshape(2 * n_seq // SUBLANES, SUBLANES, N_SLABS, LANES)
    return rows.transpose(0, 2, 1, 3)


def _layer(l, x_prompt, x_sample, mem_prompt, cache_k, cache_v, page_table,
           state_ssm_re, state_ssm_im, state_conv, cache_mem_k, cache_mem_v, p):
    bsz, seq, _ = x_prompt.shape
    n_dec, t_new, _ = x_sample.shape
    n_p = bsz * seq
    n_s = n_dec * t_new
    past_len = page_table.shape[1] * PAGE_SIZE
    lam0 = 0.8 - 0.6 * math.exp(-0.3 * l)
    out_scale = 1.0 - lam0
    row = lambda v: v.reshape(1, -1).astype(F32)
    bf = lambda w: w.astype(BF16)

    lb_re, lb_im, bb_re, bb_im, lam = _prep(
        p['ssm_a_re'], p['ssm_a_im'], p['ssm_log_dt'], p['ssm_b_re'], p['ssm_b_im'],
        p['lam_q1'], p['lam_k1'], p['lam_q2'], p['lam_k2'], lam0)

    def b_mats(bbt):
        blocks = bbt.reshape(SSM_GROUP_CH, SSM_GROUPS, SSM_STATE).transpose(1, 0, 2)
        return bf(_block_diag(blocks, 4))

    def c_mats(cm):
        return bf(_block_diag(cm.transpose(0, 2, 1), 4))

    mats = (b_mats(bb_re), b_mats(bb_im), c_mats(p['ssm_c_re']), c_mats(p['ssm_c_im']))
    d_row = row(p['ssm_d'])
    glu_b = bf(p['ssm_glu_w'])

    w_in = p['w_in']
    k0, k1 = SSM_WIDTH + ATTN_WIDTH, SSM_WIDTH + 2 * ATTN_WIDTH
    w_uqv = bf(jnp.concatenate([w_in[:, :k0], w_in[:, k1:]], axis=1))
    w_k = bf(w_in[:, k0:k1])
    seg = jnp.kron(jnp.eye(ATTN_WIDTH // ATTN_HEAD_DIM, dtype=F32),
                   jnp.ones((ATTN_HEAD_DIM, ATTN_HEAD_DIM), F32)).astype(BF16)
    qg = jnp.tile(p['q_norm_g'], ATTN_WIDTH // ATTN_HEAD_DIM).reshape(1, ATTN_WIDTH)
    kg = jnp.tile(p['k_norm_g'], ATTN_WIDTH // ATTN_HEAD_DIM).reshape(1, ATTN_WIDTH)
    sg = row(p['subln_g'])
    w_out_b = bf(p['w_out'])
    w1, w2 = w_out_b[:SSM_WIDTH], w_out_b[SSM_WIDTH:]
    wq_b, wk_b, wv_b, wo_b = bf(p['ca_wq']), bf(p['ca_wk']), bf(p['ca_wv']), bf(p['ca_wo'])
    wg_b, wvf_b, wd_b = bf(p['ffn_wg']), bf(p['ffn_wv']), bf(p['ffn_wd'])
    cw = p['ffn_conv_w'].astype(F32)
    cb = row(p['ffn_conv_b'])

    xp = x_prompt.reshape(n_p, D_MODEL)
    kg_col = jnp.broadcast_to(p['k_norm_g'].reshape(ATTN_HEAD_DIM, 1), (ATTN_HEAD_DIM, IN_PROJ_TILE))
    u_p, q_p, kt_p, v_p, kb_p, vb_p = _in_proj(xp, row(p['ln1_g']), w_uqv, w_k.T, seg, qg, kg_col,
                                               IN_PROJ_TILE, BF16, n_batch=bsz, blk=ATTN_BLOCK)
    zeros_h = jnp.zeros((2 * bsz // SUBLANES, N_SLABS, SUBLANES, LANES), F32)
    ssm_p, hre_p, him_p = _ssm(u_p.reshape(bsz, seq, SSM_WIDTH), mats,
                               (_lam_groups(lb_re, bsz), _lam_groups(lb_im, bsz)),
                               d_row, glu_b, (zeros_h, zeros_h), bsz, SSM_CHUNK)

    xs = x_sample.reshape(n_s, D_MODEL)
    u_s, q_s, k_s, v_s = _in_proj(xs, row(p['ln1_g']), w_uqv, w_k, seg, qg, kg, n_s, F32)
    h0 = (_state_to_groups(state_ssm_re.reshape(n_dec, N_STATE), n_dec),
          _state_to_groups(state_ssm_im.reshape(n_dec, N_STATE), n_dec))
    ssm_s, hre_s, him_s = _ssm(u_s.reshape(1, n_s, SSM_WIDTH), mats,
                               (_lam_groups(lb_re, n_dec), _lam_groups(lb_im, n_dec)),
                               d_row, glu_b, h0, n_dec, t_new)

    n_phys = cache_k.shape[0]
    cache_kt = cache_k.transpose(0, 2, 3, 4, 1).reshape(n_phys, ATTN_WIDTH, PAGE_SIZE)
    tok3 = lambda a: a.reshape(n_dec, t_new, ATTN_WIDTH)
    att_p, att_s = _attention(page_table, lam, sg, q_p.reshape(bsz, seq, ATTN_WIDTH), kb_p,
                              vb_p.reshape(bsz, seq, ATTN_WIDTH), tok3(q_s), tok3(k_s), tok3(v_s),
                              cache_kt, cache_v.reshape(n_phys, PAGE_SIZE * ATTN_HEADS, ATTN_V_DIM),
                              ATTN_BLOCK, past_len, out_scale)

    mk, mv, mkb, mvb = _mem_kv(mem_prompt.reshape(bsz * N_MEM, D_MODEL), row(p['mem_norm_g']),
                               wk_b, wv_b, row(p['ca_k_norm_g']), MEM_TILE)
    x1_p, co_p = _out_proj(xp, ssm_p.reshape(n_p, SSM_WIDTH), att_p.reshape(n_p, ATTN_WIDTH),
                           w1, w2, row(p['ln2_g']), wq_b, row(p['ca_q_norm_g']), OUT_PROJ_TILE, BF16,
                           mem=(mkb.reshape(bsz, N_MEM, D_MODEL), mvb.reshape(bsz, N_MEM, D_MODEL)))

    x1_s, cq_s = _out_proj(xs, ssm_s.reshape(n_s, SSM_WIDTH), att_s.reshape(n_s, ATTN_WIDTH),
                           w1, w2, row(p['ln2_g']), wq_b, row(p['ca_q_norm_g']), n_s, F32)
    y_p, conv_p, co_s = _ffn(x1_p, co_p, wo_b, row(p['ln3_g']), wg_b, wvf_b, cw, cb, wd_b,
                             jnp.zeros((bsz, CONV_WIDTH - 1, FFN_HIDDEN), F32), FFN_TILE, 1,
                             rider=(cq_s.reshape(n_dec, t_new, D_MODEL),
                                    _mem_rows(cache_mem_k), _mem_rows(cache_mem_v)))

    tmaj = lambda a: a.reshape(n_dec, t_new, -1).transpose(1, 0, 2).reshape(n_s, -1)
    prev_s = state_conv.transpose(1, 0, 2).reshape(1, (CONV_WIDTH - 1) * n_dec, FFN_HIDDEN)
    y_s, conv_s = _ffn(tmaj(x1_s), tmaj(co_s), wo_b, row(p['ln3_g']), wg_b, wvf_b, cw, cb, wd_b,
                       prev_s, n_s, n_dec)
    y_s = y_s.reshape(t_new, n_dec, D_MODEL).transpose(1, 0, 2)
    conv_s = conv_s.reshape(CONV_WIDTH - 1, n_dec, FFN_HIDDEN).transpose(1, 0, 2)

    return (y_p.reshape(bsz, seq, D_MODEL), y_s,
            kt_p.reshape(bsz, ATTN_HEADS, 2, ATTN_HEAD_DIM, seq).transpose(0, 4, 1, 2, 3),
            v_p.reshape(bsz, seq, ATTN_HEADS, ATTN_V_DIM),
            k_s.reshape(n_dec, t_new, ATTN_HEADS, 2, ATTN_HEAD_DIM),
            v_s.reshape(n_dec, t_new, ATTN_HEADS, ATTN_V_DIM),
            _groups_to_state(hre_p, bsz), _groups_to_state(him_p, bsz),
            _groups_to_state(hre_s, n_dec), _groups_to_state(him_s, n_dec),
            conv_p, conv_s,
            mk.reshape(bsz, N_MEM, CA_HEADS, CA_HEAD_DIM), mv.reshape(bsz, N_MEM, CA_HEADS, CA_HEAD_DIM))


def kernel(x_prompt, x_sample, mem_prompt, cache_k, cache_v, page_table, state_ssm_re, state_ssm_im, state_conv, cache_mem_k, cache_mem_v, ln1_g, w_in, ssm_a_re, ssm_a_im, ssm_b_re, ssm_b_im, ssm_c_re, ssm_c_im, ssm_d, ssm_log_dt, ssm_glu_w, q_norm_g, k_norm_g, lam_q1, lam_k1, lam_q2, lam_k2, subln_g, w_out, ln2_g, mem_norm_g, ca_wq, ca_wk, ca_wv, ca_q_norm_g, ca_k_norm_g, ca_wo, ln3_g, ffn_wg, ffn_wv, ffn_conv_w, ffn_conv_b, ffn_wd):
    params = dict(
        ln1_g=ln1_g, w_in=w_in, ssm_a_re=ssm_a_re, ssm_a_im=ssm_a_im, ssm_b_re=ssm_b_re,
        ssm_b_im=ssm_b_im, ssm_c_re=ssm_c_re, ssm_c_im=ssm_c_im, ssm_d=ssm_d,
        ssm_log_dt=ssm_log_dt, ssm_glu_w=ssm_glu_w, q_norm_g=q_norm_g, k_norm_g=k_norm_g,
        lam_q1=lam_q1, lam_k1=lam_k1, lam_q2=lam_q2, lam_k2=lam_k2, subln_g=subln_g,
        w_out=w_out, ln2_g=ln2_g, mem_norm_g=mem_norm_g, ca_wq=ca_wq, ca_wk=ca_wk, ca_wv=ca_wv,
        ca_q_norm_g=ca_q_norm_g, ca_k_norm_g=ca_k_norm_g, ca_wo=ca_wo, ln3_g=ln3_g,
        ffn_wg=ffn_wg, ffn_wv=ffn_wv, ffn_conv_w=ffn_conv_w, ffn_conv_b=ffn_conv_b, ffn_wd=ffn_wd)
    y_p, y_s = x_prompt, x_sample
    per_layer = []
    for l in range(w_in.shape[0]):
        p = {k: v[l] for k, v in params.items()}
        outs = _layer(l, y_p, y_s, mem_prompt, cache_k[l], cache_v[l], page_table,
                      state_ssm_re[l], state_ssm_im[l], state_conv[l],
                      cache_mem_k[l], cache_mem_v[l], p)
        y_p, y_s = outs[0], outs[1]
        per_layer.append(outs[2:])
    return (y_p, y_s) + tuple(jnp.stack(leaf) for leaf in zip(*per_layer))
```

```python
import functools
import math

import jax
import jax.numpy as jnp
from jax import lax
from jax.experimental import pallas as pl
from jax.experimental.pallas import tpu as pltpu

F32 = jnp.float32
BF16 = jnp.bfloat16

D_MODEL = 1024
SSM_WIDTH = 512
SSM_GROUP_CH = 16
SSM_GROUPS = 32
SSM_STATE = 64
N_STATE = SSM_GROUPS * SSM_STATE
ATTN_WIDTH = 512
ATTN_HEAD_DIM = 64
ATTN_V_DIM = 128
ATTN_HEADS = 4
PAGE_SIZE = 128
N_MEM = 256
CA_HEADS = 4
CA_HEAD_DIM = 256
FFN_HIDDEN = 2816
CONV_WIDTH = 3
NORM_EPS = 1e-6

LANES = 128
SUBLANES = 8
VMEM_LIMIT = 56 * 1024 * 1024
VMEM_LIMIT_RIDER = 59 * 1024 * 1024

FFN_TILE = 512
OUT_PROJ_TILE = 1024
IN_PROJ_TILE = 1024
MEM_TILE = 256
SSM_CHUNK = 256
ATTN_BLOCK = 512
LOG2E = math.log2(math.e)

def _cparams(n_axes, vmem_limit=VMEM_LIMIT):
    return pltpu.CompilerParams(
        dimension_semantics=("arbitrary",) * n_axes,
        vmem_limit_bytes=vmem_limit)


def _const_spec(shape):
    nd = len(shape)
    return pl.BlockSpec(shape, lambda *_: (0,) * nd, pipeline_mode=pl.Buffered(1))


def _rms(x, g):
    return x * lax.rsqrt(jnp.mean(x * x, axis=-1, keepdims=True) + NORM_EPS) * g


def _dot(a, b):
    return jnp.dot(a, b, preferred_element_type=F32)


def _dot_nt(a, b):
    return lax.dot_general(a, b, (((1,), (1,)), ((), ())), preferred_element_type=F32)


def _head_rms(x, g, width, scale):
    outs = []
    for h in range(x.shape[-1] // width):
        c = x[:, h * width:(h + 1) * width]
        outs.append(_rms(c, g) * scale)
    return jnp.concatenate(outs, axis=-1)


def _prep_kernel(are_ref, aim_ref, ldt_ref, bre_ref, bim_ref,
                 q1_ref, k1_ref, q2_ref, k2_ref,
                 lbre_ref, lbim_ref, bbre_ref, bbim_ref, lam_ref, *, lam0):
    a_re = are_ref[...]
    a_im = aim_ref[...]
    dt = jnp.exp(ldt_ref[...])
    mag = jnp.exp(a_re * dt)
    lb_re = mag * jnp.cos(a_im * dt)
    lb_im = mag * jnp.sin(a_im * dt)
    den = a_re * a_re + a_im * a_im
    n_re = lb_re - 1.0
    f_re = (n_re * a_re + lb_im * a_im) / den
    f_im = (lb_im * a_re - n_re * a_im) / den
    b_re = bre_ref[...]
    b_im = bim_ref[...]
    lbre_ref[...] = lb_re
    lbim_ref[...] = lb_im
    bbre_ref[...] = f_re * b_re - f_im * b_im
    bbim_ref[...] = f_re * b_im + f_im * b_re
    s1 = jnp.sum(q1_ref[...] * k1_ref[...], axis=-1, keepdims=True)
    s2 = jnp.sum(q2_ref[...] * k2_ref[...], axis=-1, keepdims=True)
    lam_ref[...] = jnp.exp(s1) - jnp.exp(s2) + lam0


def _prep(a_re, a_im, log_dt, b_re, b_im, q1, k1, q2, k2, lam0):
    are = a_re.reshape(1, N_STATE)
    aim = a_im.reshape(1, N_STATE)
    ldt = jnp.repeat(log_dt, SSM_STATE).reshape(1, N_STATE)
    bre = b_re.transpose(2, 0, 1).reshape(SSM_GROUP_CH, N_STATE)
    bim = b_im.transpose(2, 0, 1).reshape(SSM_GROUP_CH, N_STATE)
    vec = lambda v: v.reshape(1, ATTN_HEAD_DIM)
    row = jax.ShapeDtypeStruct((1, N_STATE), F32)
    mat = jax.ShapeDtypeStruct((SSM_GROUP_CH, N_STATE), F32)
    return pl.pallas_call(
        functools.partial(_prep_kernel, lam0=lam0),
        out_shape=(row, row, mat, mat, jax.ShapeDtypeStruct((1, 1), F32)),
        name="prep",
    )(are, aim, ldt, bre, bim, vec(q1), vec(k1), vec(q2), vec(k2))


def _in_proj_kernel(x_ref, g_ref, w_ref, wk_ref, seg_ref, qg_ref, kg_ref,
                    u_ref, q_ref, k_ref, v_ref, *bf_refs, key_major, blk):
    xn = _rms(x_ref[...], g_ref[...]).astype(BF16)
    proj = _dot(xn, w_ref[...])
    seg = seg_ref[...]

    def chunk_norm(z, g):
        ms = _dot((z * z).astype(BF16), seg) * (1.0 / ATTN_HEAD_DIM)
        return z * lax.rsqrt(ms + NORM_EPS) * g

    u_ref[...] = proj[:, :SSM_WIDTH]
    q = proj[:, SSM_WIDTH:SSM_WIDTH + ATTN_WIDTH]
    v = proj[:, SSM_WIDTH + ATTN_WIDTH:]
    q_ref[...] = (chunk_norm(q, qg_ref[...]) * (ATTN_HEAD_DIM ** -0.5 * LOG2E)).astype(q_ref.dtype)
    if key_major:
        kb_ref, vb_ref = bf_refs
        tm = x_ref.shape[0]
        z = _dot_nt(wk_ref[...], xn).reshape(ATTN_WIDTH // ATTN_HEAD_DIM, ATTN_HEAD_DIM, tm)
        ms = jnp.mean(z * z, axis=1, keepdims=True)
        kn = (z * lax.rsqrt(ms + NORM_EPS) * kg_ref[...]).reshape(ATTN_WIDTH, tm)
        k_ref[0] = kn
        for i in range(tm // blk):
            kb_ref[0, i] = kn[:, i * blk:(i + 1) * blk].astype(BF16)
        for h in range(ATTN_HEADS):
            v_ref[:, h, :] = v[:, ATTN_V_DIM * h:ATTN_V_DIM * (h + 1)]
        vb_ref[...] = v.astype(BF16)
    else:
        k_ref[...] = chunk_norm(_dot(xn, wk_ref[...]), kg_ref[...])
        v_ref[...] = v


def _in_proj(x, ln1_g, w_uqv, wk, seg, qg, kg, tm, q_dtype, n_batch=None, blk=None):
    n = x.shape[0]
    key_major = n_batch is not None
    bsz = n_batch if key_major else 1
    per_b = n // bsz // tm
    tok = lambda width: pl.BlockSpec((tm, width), lambda b, i: (b * per_b + i, 0))
    out_specs = [tok(SSM_WIDTH), tok(ATTN_WIDTH)]
    out_shape = [jax.ShapeDtypeStruct((n, SSM_WIDTH), F32),
                 jax.ShapeDtypeStruct((n, ATTN_WIDTH), q_dtype)]
    if key_major:
        t = n // bsz
        out_specs += [pl.BlockSpec((1, ATTN_WIDTH, tm), lambda b, i: (b, 0, i)),
                      pl.BlockSpec((tm, ATTN_HEADS, ATTN_V_DIM), lambda b, i: (b * per_b + i, 0, 0)),
                      pl.BlockSpec((1, tm // blk, ATTN_WIDTH, blk), lambda b, i: (b, i, 0, 0)),
                      tok(ATTN_WIDTH)]
        out_shape += [jax.ShapeDtypeStruct((bsz, ATTN_WIDTH, t), F32),
                      jax.ShapeDtypeStruct((n, ATTN_HEADS, ATTN_V_DIM), F32),
                      jax.ShapeDtypeStruct((bsz, t // blk, ATTN_WIDTH, blk), BF16),
                      jax.ShapeDtypeStruct((n, ATTN_WIDTH), BF16)]
    else:
        out_specs += [tok(ATTN_WIDTH), tok(ATTN_WIDTH)]
        out_shape += [jax.ShapeDtypeStruct((n, ATTN_WIDTH), F32)] * 2
    return pl.pallas_call(
        functools.partial(_in_proj_kernel, key_major=key_major, blk=blk),
        grid=(bsz, per_b),
        in_specs=[tok(D_MODEL), _const_spec((1, D_MODEL)), _const_spec(w_uqv.shape),
                  _const_spec(wk.shape), _const_spec((ATTN_WIDTH, ATTN_WIDTH)),
                  _const_spec((1, ATTN_WIDTH)), _const_spec(kg.shape)],
        out_specs=out_specs,
        out_shape=out_shape,
        compiler_params=_cparams(2),
        name="in_proj",
    )(x, ln1_g, w_uqv, wk, seg, qg, kg)


N_SLABS = N_STATE // 2 // LANES


def _gelu_tanh(x):
    c = math.sqrt(2.0 / math.pi)
    return 0.5 * x * (1.0 + jnp.tanh(c * (x + 0.044715 * (x * x * x))))


def _ssm_kernel(u_ref, bre_ref, bim_ref, cre_ref, cim_ref, lre_ref, lim_ref,
                d_ref, glu_ref, h0re_ref, h0im_ref,
                out_ref, hre_ref, him_ref, sre, sim,
                *, n_seq, tc, pitch, n_bulk, rows_bulk, pitch_bulk):
    c = pl.program_id(0)
    rows = n_seq * tc
    n_groups = 2 * n_seq // SUBLANES

    @pl.when(c == 0)
    def _():
        hre_ref[...] = h0re_ref[...]
        him_ref[...] = h0im_ref[...]

    u = u_ref[...].reshape(rows, SSM_WIDTH)
    ub = u.astype(BF16)

    for j in range(4):
        hh, jj = divmod(j, 2)
        uj = ub[:, LANES * j:LANES * (j + 1)]
        for src, dst in ((bre_ref, sre), (bim_ref, sim)):
            bu = _dot(uj, src[j])
            for kk in range(4):
                slab = 4 * jj + kk
                for bb in range(n_bulk):
                    r0 = (hh * n_bulk + bb) * pitch_bulk
                    dst[slab, r0:r0 + rows_bulk, :] = (
                        bu[bb * rows_bulk:(bb + 1) * rows_bulk, LANES * kk:LANES * (kk + 1)])

    def group_body(g, carry):
        base = g * (SUBLANES * pitch)
        lr = [lre_ref[g, k] for k in range(N_SLABS)]
        li = [lim_ref[g, k] for k in range(N_SLABS)]
        hr0 = tuple(hre_ref[g, k] for k in range(N_SLABS))
        hi0 = tuple(him_ref[g, k] for k in range(N_SLABS))

        def step(t, hc):
            hr, hi = hc
            idx = pl.ds(base + t, SUBLANES, stride=pitch)
            nr, ni = [], []
            for k in range(N_SLABS):
                br = sre[k, idx, :]
                bi = sim[k, idx, :]
                r = lr[k] * hr[k] - li[k] * hi[k] + br
                i = lr[k] * hi[k] + li[k] * hr[k] + bi
                sre[k, idx, :] = r
                sim[k, idx, :] = i
                nr.append(r)
                ni.append(i)
            return tuple(nr), tuple(ni)

        hr, hi = lax.fori_loop(0, tc, step, (hr0, hi0), unroll=min(tc, 4))
        for k in range(N_SLABS):
            hre_ref[g, k] = hr[k]
            him_ref[g, k] = hi[k]
        return carry

    lax.fori_loop(0, n_groups, group_body, 0)

    y_rows = []
    for bb in range(n_bulk):
        y_cols = []
        for j in range(4):
            hh, jj = divmod(j, 2)
            r0 = (hh * n_bulk + bb) * pitch_bulk
            hr = jnp.concatenate(
                [sre[4 * jj + kk, r0:r0 + rows_bulk, :] for kk in range(4)], axis=1).astype(BF16)
            hi = jnp.concatenate(
                [sim[4 * jj + kk, r0:r0 + rows_bulk, :] for kk in range(4)], axis=1).astype(BF16)
            y_cols.append(_dot(hr, cre_ref[j]) - _dot(hi, cim_ref[j]))
        y_rows.append(jnp.concatenate(y_cols, axis=1))
    y = jnp.concatenate(y_rows, axis=0) if n_bulk > 1 else y_rows[0]
    y = y + d_ref[...] * u
    gl = _gelu_tanh(y)
    gate = jax.nn.sigmoid(_dot(gl.astype(BF16), glu_ref[...]))
    out_ref[...] = (gl * gate).astype(out_ref.dtype).reshape(out_ref.shape)


def _ssm(u3, mats, lam_g, d_row, glu_b, h0, n_seq, tc):
    bre, bim, cre, cim = mats
    lre_g, lim_g = lam_g
    h0re, h0im = h0
    n_groups = 2 * n_seq // SUBLANES
    t_total = u3.shape[0] * u3.shape[1] // n_seq
    n_chunks = t_total // tc
    if tc % SUBLANES == 0:
        pitch = tc + SUBLANES + SUBLANES // 2
        n_bulk, rows_bulk, pitch_bulk = n_seq, tc, pitch
        ublock = (n_seq, tc, SSM_WIDTH)
    else:
        assert n_chunks == 1
        pitch = tc
        n_bulk, rows_bulk, pitch_bulk = 1, n_seq * tc, n_seq * tc
        ublock = (1, n_seq * tc, SSM_WIDTH)
    scr_rows = 2 * n_bulk * pitch_bulk
    hshape = (n_groups, N_SLABS, SUBLANES, LANES)
    kern = functools.partial(_ssm_kernel, n_seq=n_seq, tc=tc, pitch=pitch,
                             n_bulk=n_bulk, rows_bulk=rows_bulk, pitch_bulk=pitch_bulk)
    return pl.pallas_call(
        kern,
        grid=(n_chunks,),
        in_specs=[pl.BlockSpec(ublock, lambda c: (0, c, 0)),
                  _const_spec(bre.shape), _const_spec(bim.shape),
                  _const_spec(cre.shape), _const_spec(cim.shape),
                  _const_spec(hshape), _const_spec(hshape),
                  _const_spec((1, SSM_WIDTH)), _const_spec((SSM_WIDTH, SSM_WIDTH)),
                  _const_spec(hshape), _const_spec(hshape)],
        out_specs=[pl.BlockSpec(ublock, lambda c: (0, c, 0)),
                   _const_spec(hshape), _const_spec(hshape)],
        out_shape=[jax.ShapeDtypeStruct(u3.shape, BF16),
                   jax.ShapeDtypeStruct(hshape, F32),
                   jax.ShapeDtypeStruct(hshape, F32)],
        scratch_shapes=[pltpu.VMEM((N_SLABS, scr_rows, LANES), F32),
                        pltpu.VMEM((N_SLABS, scr_rows, LANES), F32)],
        compiler_params=_cparams(1),
        name="ssm",
    )(u3, bre, bim, cre, cim, lre_g, lim_g, d_row, glu_b, h0re, h0im)


def _state_to_groups(h, n_seq):
    x = h.reshape(n_seq, 2, N_SLABS, LANES).transpose(1, 0, 2, 3)
    x = x.reshape(2 * n_seq // SUBLANES, SUBLANES, N_SLABS, LANES)
    return x.transpose(0, 2, 1, 3)


def _groups_to_state(x, n_seq):
    x = x.transpose(0, 2, 1, 3).reshape(2, n_seq, N_SLABS, LANES)
    return x.transpose(1, 0, 2, 3).reshape(n_seq, SSM_GROUPS, SSM_STATE)


def _head_slope(h):
    return jnp.where(h == 0, 2.0 ** -2, jnp.where(h == 1, 2.0 ** -4,
                     jnp.where(h == 2, 2.0 ** -6, 2.0 ** -8))).astype(F32)


def _attn_prompt_kernel(lam_ref, q_ref, k_ref, v_ref, sg_ref, o_ref,
                        q_scr, s_scr, m_scr, l_scr, acc_scr, *, bq, bk, out_scale, rider=None):
    ride = lambda: next(rider, None) if rider is not None else None
    ride()
    h = pl.program_id(1)
    qi = pl.program_id(2)
    slope = _head_slope(h) * LOG2E
    lam = lam_ref[0, 0]
    n_maps = 2

    q = q_ref[0]
    lane = lax.broadcasted_iota(jnp.int32, q.shape, 1)
    zero = jnp.zeros_like(q)
    q_scr[0] = jnp.where(lane < ATTN_HEAD_DIM, q, zero)
    q_scr[1] = jnp.where(lane >= ATTN_HEAD_DIM, q, zero)

    col = lax.broadcasted_iota(jnp.int32, (1, bk), 1)
    q0 = qi * bq

    def lane_tiles(x):
        return [x[:, LANES * i:LANES * (i + 1)] for i in range(bk // LANES)]

    n_full = q0 // bk

    def run_pairs(first, count, step):
        odd = lax.rem(count, 2)

        @pl.when(odd == 1)
        def _():
            step(first)

        def body(t, carry):
            j = first + odd + 2 * t
            step(j)
            step(j + 1)
            return carry

        lax.fori_loop(0, count // 2, body, 0)

    def scores(j, i, masked):
        kpos = col + j * bk
        s = _dot(q_scr[i], k_ref[0, j]) + slope * (kpos - q0).astype(F32)
        if masked:
            qpos = lax.broadcasted_iota(jnp.int32, (bq, bk), 0) + q0
            s = jnp.where(qpos >= kpos, s, -jnp.inf)
        return s

    def accumulate(j, i, p):
        start = pl.multiple_of(j * bk, bk)
        lpart = l_scr[i]
        for piece in lane_tiles(p):
            lpart = lpart + piece
        l_scr[i] = lpart
        acc_scr[i] += _dot(p.astype(BF16), v_ref[0, pl.ds(start, bk), :])

    def score_step(j, masked=False):
        for i in range(n_maps):
            s = scores(j, i, masked)
            s_scr[j, i] = s
            mpart = m_scr[i]
            for piece in lane_tiles(s):
                mpart = jnp.maximum(mpart, piece)
            m_scr[i] = mpart

    m_scr[...] = jnp.full(m_scr.shape, -jnp.inf, F32)
    score_step(n_full, masked=True)
    ride()
    run_pairs(0, n_full, score_step)
    ms = [jnp.max(m_scr[i], axis=-1, keepdims=True) for i in range(n_maps)]
    ride()

    def sum_step(j):
        for i in range(n_maps):
            accumulate(j, i, jnp.exp2(s_scr[j, i] - ms[i]))

    l_scr[...] = jnp.zeros(l_scr.shape, F32)
    acc_scr[...] = jnp.zeros(acc_scr.shape, F32)
    run_pairs(0, n_full + 1, sum_step)

    outs = [acc_scr[i] / jnp.sum(l_scr[i], axis=-1, keepdims=True) for i in range(n_maps)]
    o = outs[0] - lam * outs[1]
    o_ref[0] = (_rms(o, sg_ref[...]) * out_scale).astype(o_ref.dtype)
    ride()


def _attn_sample_steps(lam_ref, q_ref, kn_ref, vn_ref, sg_ref, k_refs, v_refs, o_ref,
                       *, t_new, past_len, out_scale):
    n_pages = len(k_refs)
    lam = lam_ref[0, 0]
    n_rows = 2 * ATTN_HEADS * t_new

    q4 = q_ref[0]
    qe = jnp.concatenate([q4] * (2 * ATTN_HEADS), axis=0)
    row = lax.broadcasted_iota(jnp.int32, qe.shape, 0)
    lane = lax.broadcasted_iota(jnp.int32, qe.shape, 1)
    qe = jnp.where(lane // ATTN_HEAD_DIM == row // t_new, qe, 0.0)
    qeb = qe.astype(BF16)

    rcol = lax.broadcasted_iota(jnp.int32, (n_rows, 1), 0)
    head_c = rcol // (2 * t_new)
    tok_c = rcol % t_new
    slope = _head_slope(head_c) * LOG2E
    rows_h = 2 * t_new

    kt = jnp.concatenate([k_refs[p][...].astype(BF16) for p in range(n_pages)], axis=1)
    kpos = lax.broadcasted_iota(jnp.int32, (1, n_pages * PAGE_SIZE), 1) - past_len
    s = _dot(qeb, kt) + slope * kpos.astype(F32)

    kn = kn_ref[0]
    vn = vn_ref[0]
    s_new = []
    for c in range(t_new):
        sc = jnp.sum(qe * kn[c:c + 1, :], axis=-1, keepdims=True) + slope * float(c)
        s_new.append(jnp.where(tok_c >= c, sc, -jnp.inf))

    yield
    m = jnp.max(s, axis=-1, keepdims=True)
    for sc in s_new:
        m = jnp.maximum(m, sc)
    pe = jnp.exp2(s - m)
    l = jnp.sum(pe, axis=-1, keepdims=True)

    pv = []
    for h in range(ATTN_HEADS):
        vh = jnp.concatenate(
            [v_refs[p][pl.ds(h, PAGE_SIZE, stride=ATTN_HEADS), :].astype(BF16)
             for p in range(n_pages)], axis=0)
        pv.append(_dot(pe[rows_h * h:rows_h * (h + 1)].astype(BF16), vh))
    acc = jnp.concatenate(pv, axis=0)
    yield

    for c in range(t_new):
        pc = jnp.exp2(s_new[c] - m)
        l = l + pc
        vc = jnp.concatenate(
            [jnp.broadcast_to(vn[c:c + 1, ATTN_V_DIM * h:ATTN_V_DIM * (h + 1)], (rows_h, ATTN_V_DIM))
             for h in range(ATTN_HEADS)], axis=0)
        acc = acc + pc * vc

    acc = acc / l
    outs = []
    for h in range(ATTN_HEADS):
        a0 = acc[rows_h * h:rows_h * h + t_new]
        a1 = acc[rows_h * h + t_new:rows_h * (h + 1)]
        outs.append(_rms(a0 - lam * a1, sg_ref[...]) * out_scale)
    o_ref[0] = jnp.concatenate(outs, axis=-1)


def _attention_kernel(pt_ref, lam_ref, sg_ref, q_ref, k_ref, v_ref, qs_ref, kn_ref, vn_ref,
                      kc_ref, vc_ref, o_ref, os_ref, kbuf, vbuf, sem, *scratch,
                      n_pages, t_new, past_len, bq, bk, out_scale):
    n_steps = pl.num_programs(0) * pl.num_programs(1) * pl.num_programs(2)
    step = ((pl.program_id(0) * pl.num_programs(1) + pl.program_id(1)) * pl.num_programs(2)
            + pl.program_id(2))

    def page_copies(seq_idx, slot):
        copies = []
        for p in range(n_pages):
            page = pt_ref[seq_idx * n_pages + p]
            copies.append(pltpu.make_async_copy(kc_ref.at[page], kbuf.at[slot, p], sem.at[slot]))
            copies.append(pltpu.make_async_copy(vc_ref.at[page], vbuf.at[slot, p], sem.at[slot]))
        return copies

    @pl.when(step == 0)
    def _():
        for copy in page_copies(0, 0):
            copy.start()

    @pl.when(step + 1 < n_steps)
    def _():
        for copy in page_copies(step + 1, lax.rem(step + 1, 2)):
            copy.start()

    slot = lax.rem(step, 2)
    for copy in page_copies(step, slot):
        copy.wait()

    k_refs = [kbuf.at[slot, p] for p in range(n_pages)]
    v_refs = [vbuf.at[slot, p] for p in range(n_pages)]
    sample = _attn_sample_steps(lam_ref, qs_ref, kn_ref, vn_ref, sg_ref, k_refs, v_refs, os_ref,
                                t_new=t_new, past_len=past_len, out_scale=out_scale)
    _attn_prompt_kernel(lam_ref, q_ref, k_ref, v_ref, sg_ref, o_ref, *scratch,
                        bq=bq, bk=bk, out_scale=out_scale, rider=sample)


def _attention(page_table, lam, sg, qb, kb, vb, q3, kn3, vn3, cache_kt, cache_v3,
               bq, past_len, out_scale):
    bsz, t, _ = qb.shape
    bk = kb.shape[-1]
    nq = t // bq
    n_dec, t_new, _ = q3.shape
    n_pages = page_table.shape[1]
    assert n_dec == bsz * ATTN_HEADS * nq, "one sample sequence per prompt grid step"
    pt = page_table.reshape(-1)

    def seq(b, h, i):
        return (b * ATTN_HEADS + h) * nq + i

    tok_spec = pl.BlockSpec((1, t_new, ATTN_WIDTH), lambda b, h, i, pt: (seq(b, h, i), 0, 0))

    head_spec = pl.BlockSpec((1, bq, ATTN_V_DIM), lambda b, h, i, pt: (b, i, h))
    grid_spec = pltpu.PrefetchScalarGridSpec(
        num_scalar_prefetch=1,
        grid=(bsz, ATTN_HEADS, nq),
        in_specs=[pl.BlockSpec(memory_space=pltpu.SMEM), _const_spec((1, ATTN_V_DIM)),
                  head_spec,
                  pl.BlockSpec((1, t // bk, ATTN_V_DIM, bk), lambda b, h, i, pt: (b, 0, h, 0)),
                  pl.BlockSpec((1, t, ATTN_V_DIM), lambda b, h, i, pt: (b, 0, h)),
                  tok_spec, tok_spec, tok_spec,
                  pl.BlockSpec(memory_space=pl.ANY), pl.BlockSpec(memory_space=pl.ANY)],
        out_specs=[head_spec, tok_spec],
        scratch_shapes=[pltpu.VMEM((2, n_pages) + cache_kt.shape[1:], F32),
                        pltpu.VMEM((2, n_pages) + cache_v3.shape[1:], F32),
                        pltpu.SemaphoreType.DMA((2,)),
                        pltpu.VMEM((2, bq, ATTN_V_DIM), BF16),
                        pltpu.VMEM((t // bk, 2, bq, bk), F32),
                        pltpu.VMEM((2, bq, LANES), F32),
                        pltpu.VMEM((2, bq, LANES), F32),
                        pltpu.VMEM((2, bq, ATTN_V_DIM), F32)])
    kern = functools.partial(_attention_kernel, n_pages=n_pages, t_new=t_new, past_len=past_len,
                             bq=bq, bk=bk, out_scale=out_scale)
    return pl.pallas_call(
        kern,
        grid_spec=grid_spec,
        out_shape=[jax.ShapeDtypeStruct(qb.shape, BF16), jax.ShapeDtypeStruct(q3.shape, F32)],
        compiler_params=_cparams(3),
        name="attention",
    )(pt, lam, sg, qb, kb, vb, q3, kn3, vn3, cache_kt, cache_v3)


def _memory_attention(cq, mk_ref, mv_ref):
    outs = []
    for h in range(CA_HEADS):
        sl = slice(CA_HEAD_DIM * h, CA_HEAD_DIM * (h + 1))
        s = _dot_nt(cq[:, sl], mk_ref[0, :, sl])
        p = jnp.exp(s - jnp.max(s, axis=-1, keepdims=True))
        l = jnp.sum(p, axis=-1, keepdims=True)
        outs.append(_dot((p / l).astype(BF16), mv_ref[0, :, sl]))
    return jnp.concatenate(outs, axis=-1)


def _out_proj_kernel(x_ref, s_ref, o_ref, w1_ref, w2_ref, g2_ref, wq_ref, qg_ref, *refs):
    x1 = (x_ref[...] + _dot(s_ref[...].astype(BF16), w1_ref[...])
          + _dot(o_ref[...].astype(BF16), w2_ref[...]))
    xn = _rms(x1, g2_ref[...]).astype(BF16)
    cq = _head_rms(_dot(xn, wq_ref[...]), qg_ref[...], CA_HEAD_DIM, CA_HEAD_DIM ** -0.5)
    if len(refs) == 4:
        mk_ref, mv_ref, x1_ref, out_ref = refs
        out_ref[...] = _memory_attention(cq.astype(BF16), mk_ref, mv_ref).astype(out_ref.dtype)
    else:
        x1_ref, out_ref = refs
        out_ref[...] = cq.astype(out_ref.dtype)
    x1_ref[...] = x1


def _out_proj(x, ssm_o, att_o, w1, w2, g2, wq, qg, tm, out_dtype, mem=None):
    n = x.shape[0]
    bsz = 1 if mem is None else mem[0].shape[0]
    per_b = n // bsz // tm
    tok = lambda width: pl.BlockSpec((tm, width), lambda b, i: (b * per_b + i, 0))
    mem_spec = pl.BlockSpec((1, N_MEM, D_MODEL), lambda b, i: (b, 0, 0))
    return pl.pallas_call(
        _out_proj_kernel,
        grid=(bsz, per_b),
        in_specs=[tok(D_MODEL), tok(SSM_WIDTH), tok(ATTN_WIDTH),
                  _const_spec((SSM_WIDTH, D_MODEL)), _const_spec((ATTN_WIDTH, D_MODEL)),
                  _const_spec((1, D_MODEL)), _const_spec((D_MODEL, D_MODEL)),
                  _const_spec((1, CA_HEAD_DIM))] + ([] if mem is None else [mem_spec, mem_spec]),
        out_specs=[tok(D_MODEL), tok(D_MODEL)],
        out_shape=[jax.ShapeDtypeStruct((n, D_MODEL), F32),
                   jax.ShapeDtypeStruct((n, D_MODEL), out_dtype)],
        compiler_params=_cparams(2),
        name="out_proj",
    )(x, ssm_o, att_o, w1, w2, g2, wq, qg, *(() if mem is None else mem))


def _mem_kv_kernel(m_ref, g_ref, wk_ref, wv_ref, kg_ref, mk_ref, mv_ref, mkb_ref, mvb_ref):
    mn = _rms(m_ref[...], g_ref[...]).astype(BF16)
    mk = _head_rms(_dot(mn, wk_ref[...]), kg_ref[...], CA_HEAD_DIM, 1.0)
    mv = _dot(mn, wv_ref[...])
    mk_ref[...] = mk
    mv_ref[...] = mv
    mkb_ref[...] = mk.astype(BF16)
    mvb_ref[...] = mv.astype(BF16)


def _mem_kv(mem, g, wk, wv, kg, tm):
    n = mem.shape[0]
    tok = pl.BlockSpec((tm, D_MODEL), lambda i: (i, 0))
    wspec = _const_spec((D_MODEL, D_MODEL))
    return pl.pallas_call(
        _mem_kv_kernel,
        grid=(n // tm,),
        in_specs=[tok, _const_spec((1, D_MODEL)), wspec, wspec, _const_spec((1, CA_HEAD_DIM))],
        out_specs=[tok, tok, tok, tok],
        out_shape=[jax.ShapeDtypeStruct((n, D_MODEL), F32)] * 2
                  + [jax.ShapeDtypeStruct((n, D_MODEL), BF16)] * 2,
        compiler_params=_cparams(1),
        name="mem_kv",
    )(mem, g, wk, wv, kg)


def _cross_sample_one(cq_ref, mk_ref, mv_ref, co_ref, b, t_new):
    halves = CA_HEAD_DIM // LANES
    per_key = halves * CA_HEADS

    def heads(ref):
        return jnp.concatenate(
            [ref[b, pl.ds(half * CA_HEADS + h, N_MEM, stride=per_key), :]
             for h in range(CA_HEADS) for half in range(halves)], axis=1).astype(BF16)

    qe = jnp.concatenate([cq_ref[b]] * CA_HEADS, axis=0)
    row = lax.broadcasted_iota(jnp.int32, qe.shape, 0)
    lane = lax.broadcasted_iota(jnp.int32, qe.shape, 1)
    qe = jnp.where(lane // CA_HEAD_DIM == row // t_new, qe, 0.0).astype(BF16)
    s = _dot_nt(qe, heads(mk_ref))
    yield
    p = jnp.exp(s - jnp.max(s, axis=-1, keepdims=True))
    l = jnp.sum(p, axis=-1, keepdims=True)
    full = _dot((p / l).astype(BF16), heads(mv_ref))
    outs = [full[h * t_new:(h + 1) * t_new, CA_HEAD_DIM * h:CA_HEAD_DIM * (h + 1)]
            for h in range(CA_HEADS)]
    co_ref[b] = jnp.concatenate(outs, axis=-1)


def _mem_rows(mem):
    n = mem.shape[0]
    halves = CA_HEAD_DIM // LANES
    return (mem.reshape(n, N_MEM, CA_HEADS, halves, LANES).transpose(0, 1, 3, 2, 4)
            .reshape(n, N_MEM * halves * CA_HEADS, LANES))


def _ffn_kernel(x_ref, co_ref, wo_ref, g3_ref, wg_ref, wv_ref, cw_ref, cb_ref, wd_ref, prev_ref,
                *refs, tm, shift, pad, t_new):
    if len(refs) == 7:
        cq_ref, mk_ref, mv_ref, y_ref, cs_ref, cos_ref, hbuf = refs
    else:
        y_ref, cs_ref, hbuf = refs
    i = pl.program_id(1)

    @pl.when(i == 0)
    def _():
        hbuf[pad - 2 * shift:pad, :] = prev_ref[0]

    riders = ([_cross_sample_one(cq_ref, mk_ref, mv_ref, cos_ref, b, t_new)
               for b in range(cq_ref.shape[0])] if len(refs) == 7 else [])
    started = []

    def ride():
        if started:
            next(started.pop(), None)
        if riders:
            started.append(riders.pop(0))
            next(started[-1])

    ride()
    x2 = x_ref[...] + _dot(co_ref[...].astype(BF16), wo_ref[...])
    xn = _rms(x2, g3_ref[...]).astype(BF16)
    ride()
    hbuf[pad:pad + tm, :] = _dot(xn, wg_ref[...])
    ride()
    hv = _dot(xn, wv_ref[...])
    ride()
    cw = cw_ref[...]
    conv = (cb_ref[...] + cw[0:1, :] * hbuf[pad - 2 * shift:pad - 2 * shift + tm, :]
            + cw[1:2, :] * hbuf[pad - shift:pad - shift + tm, :]
            + cw[2:3, :] * hbuf[pad:pad + tm, :])
    act = (conv * jax.nn.sigmoid(conv) * hv).astype(BF16)
    y_ref[...] = x2 + _dot(act, wd_ref[...])
    while started or riders:
        ride()
    last = hbuf[pad + tm - 2 * shift:pad + tm, :]
    cs_ref[0] = last
    hbuf[pad - 2 * shift:pad, :] = last


def _ffn(x1, co, wo, g3, wg, wv, cw, cb, wd, prev, tm, shift, rider=None):
    n = x1.shape[0]
    n_seq_blocks = prev.shape[0]
    per_b = n // n_seq_blocks // tm
    pad = -(-2 * shift // SUBLANES) * SUBLANES
    tok = pl.BlockSpec((tm, D_MODEL), lambda b, i: (b * per_b + i, 0))
    state = pl.BlockSpec((1, 2 * shift, FFN_HIDDEN), lambda b, i: (b, 0, 0))
    in_specs = [tok, tok, _const_spec((D_MODEL, D_MODEL)), _const_spec((1, D_MODEL)),
                _const_spec((D_MODEL, FFN_HIDDEN)), _const_spec((D_MODEL, FFN_HIDDEN)),
                _const_spec((CONV_WIDTH, FFN_HIDDEN)), _const_spec((1, FFN_HIDDEN)),
                _const_spec((FFN_HIDDEN, D_MODEL)), state]
    out_specs = [tok, state]
    out_shape = [jax.ShapeDtypeStruct((n, D_MODEL), F32), jax.ShapeDtypeStruct(prev.shape, F32)]
    operands = (x1, co, wo, g3, wg, wv, cw, cb, wd, prev)
    t_new = None
    if rider is not None:
        cq3, mem_k, mem_v = rider
        n_dec, t_new, _ = cq3.shape
        per_step = n_dec // (n_seq_blocks * per_b)
        assert per_step * n_seq_blocks * per_b == n_dec
        seqs = pl.BlockSpec((per_step, t_new, D_MODEL), lambda b, i: (b * per_b + i, 0, 0))
        mem = pl.BlockSpec((per_step,) + mem_k.shape[1:], lambda b, i: (b * per_b + i, 0, 0))
        in_specs += [seqs, mem, mem]
        out_specs.append(seqs)
        out_shape.append(jax.ShapeDtypeStruct(cq3.shape, F32))
        operands += rider
    return pl.pallas_call(
        functools.partial(_ffn_kernel, tm=tm, shift=shift, pad=pad, t_new=t_new),
        grid=(n_seq_blocks, per_b),
        in_specs=in_specs,
        out_specs=out_specs,
        out_shape=out_shape,
        scratch_shapes=[pltpu.VMEM((pad + tm, FFN_HIDDEN), F32)],
        compiler_params=_cparams(2, VMEM_LIMIT if rider is None else VMEM_LIMIT_RIDER),
        name="ffn",
    )(*operands)


def _block_diag(blocks, n_outer):
    n, r, c = blocks.shape
    per = n // n_outer
    b = blocks.reshape(n_outer, per, r, c)
    eye = jnp.eye(per, dtype=blocks.dtype)
    full = b[:, :, :, None, :] * eye[None, :, None, :, None]
    return full.reshape(n_outer, per * r, per * c)


def _lam_groups(lb, n_seq):
    halves = lb.reshape(2, 1, N_SLABS, LANES)
    rows = jnp.broadcast_to(halves, (2, n_seq, N_SLABS, LANES))
    rows = rows.reshape(2 * n_seq // SUBLANES, SUBLANES, N_SLABS, LANES)
    return rows.transpose(0, 2, 1, 3)


def _layer(l, x_prompt, x_sample, mem_prompt, cache_k, cache_v, page_table,
           state_ssm_re, state_ssm_im, state_conv, cache_mem_k, cache_mem_v, p):
    bsz, seq, _ = x_prompt.shape
    n_dec, t_new, _ = x_sample.shape
    n_p = bsz * seq
    n_s = n_dec * t_new
    past_len = page_table.shape[1] * PAGE_SIZE
    lam0 = 0.8 - 0.6 * math.exp(-0.3 * l)
    out_scale = 1.0 - lam0
    row = lambda v: v.reshape(1, -1).astype(F32)
    bf = lambda w: w.astype(BF16)

    lb_re, lb_im, bb_re, bb_im, lam = _prep(
        p['ssm_a_re'], p['ssm_a_im'], p['ssm_log_dt'], p['ssm_b_re'], p['ssm_b_im'],
        p['lam_q1'], p['lam_k1'], p['lam_q2'], p['lam_k2'], lam0)

    def b_mats(bbt):
        blocks = bbt.reshape(SSM_GROUP_CH, SSM_GROUPS, SSM_STATE).transpose(1, 0, 2)
        return bf(_block_diag(blocks, 4))

    def c_mats(cm):
        return bf(_block_diag(cm.transpose(0, 2, 1), 4))

    mats = (b_mats(bb_re), b_mats(bb_im), c_mats(p['ssm_c_re']), c_mats(p['ssm_c_im']))
    d_row = row(p['ssm_d'])
    glu_b = bf(p['ssm_glu_w'])

    w_in = p['w_in']
    k0, k1 = SSM_WIDTH + ATTN_WIDTH, SSM_WIDTH + 2 * ATTN_WIDTH
    w_uqv = bf(jnp.concatenate([w_in[:, :k0], w_in[:, k1:]], axis=1))
    w_k = bf(w_in[:, k0:k1])
    seg = jnp.kron(jnp.eye(ATTN_WIDTH // ATTN_HEAD_DIM, dtype=F32),
                   jnp.ones((ATTN_HEAD_DIM, ATTN_HEAD_DIM), F32)).astype(BF16)
    qg = jnp.tile(p['q_norm_g'], ATTN_WIDTH // ATTN_HEAD_DIM).reshape(1, ATTN_WIDTH)
    kg = jnp.tile(p['k_norm_g'], ATTN_WIDTH // ATTN_HEAD_DIM).reshape(1, ATTN_WIDTH)
    sg = row(p['subln_g'])
    w_out_b = bf(p['w_out'])
    w1, w2 = w_out_b[:SSM_WIDTH], w_out_b[SSM_WIDTH:]
    wq_b, wk_b, wv_b, wo_b = bf(p['ca_wq']), bf(p['ca_wk']), bf(p['ca_wv']), bf(p['ca_wo'])
    wg_b, wvf_b, wd_b = bf(p['ffn_wg']), bf(p['ffn_wv']), bf(p['ffn_wd'])
    cw = p['ffn_conv_w'].astype(F32)
    cb = row(p['ffn_conv_b'])

    xp = x_prompt.reshape(n_p, D_MODEL)
    kg_col = jnp.broadcast_to(p['k_norm_g'].reshape(ATTN_HEAD_DIM, 1), (ATTN_HEAD_DIM, IN_PROJ_TILE))
    u_p, q_p, kt_p, v_p, kb_p, vb_p = _in_proj(xp, row(p['ln1_g']), w_uqv, w_k.T, seg, qg, kg_col,
                                               IN_PROJ_TILE, BF16, n_batch=bsz, blk=ATTN_BLOCK)
    zeros_h = jnp.zeros((2 * bsz // SUBLANES, N_SLABS, SUBLANES, LANES), F32)
    ssm_p, hre_p, him_p = _ssm(u_p.reshape(bsz, seq, SSM_WIDTH), mats,
                               (_lam_groups(lb_re, bsz), _lam_groups(lb_im, bsz)),
                               d_row, glu_b, (zeros_h, zeros_h), bsz, SSM_CHUNK)

    xs = x_sample.reshape(n_s, D_MODEL)
    u_s, q_s, k_s, v_s = _in_proj(xs, row(p['ln1_g']), w_uqv, w_k, seg, qg, kg, n_s, F32)
    h0 = (_state_to_groups(state_ssm_re.reshape(n_dec, N_STATE), n_dec),
          _state_to_groups(state_ssm_im.reshape(n_dec, N_STATE), n_dec))
    ssm_s, hre_s, him_s = _ssm(u_s.reshape(1, n_s, SSM_WIDTH), mats,
                               (_lam_groups(lb_re, n_dec), _lam_groups(lb_im, n_dec)),
                               d_row, glu_b, h0, n_dec, t_new)

    n_phys = cache_k.shape[0]
    cache_kt = cache_k.transpose(0, 2, 3, 4, 1).reshape(n_phys, ATTN_WIDTH, PAGE_SIZE)
    tok3 = lambda a: a.reshape(n_dec, t_new, ATTN_WIDTH)
    att_p, att_s = _attention(page_table, lam, sg, q_p.reshape(bsz, seq, ATTN_WIDTH), kb_p,
                              vb_p.reshape(bsz, seq, ATTN_WIDTH), tok3(q_s), tok3(k_s), tok3(v_s),
                              cache_kt, cache_v.reshape(n_phys, PAGE_SIZE * ATTN_HEADS, ATTN_V_DIM),
                              ATTN_BLOCK, past_len, out_scale)

    mk, mv, mkb, mvb = _mem_kv(mem_prompt.reshape(bsz * N_MEM, D_MODEL), row(p['mem_norm_g']),
                               wk_b, wv_b, row(p['ca_k_norm_g']), MEM_TILE)
    x1_p, co_p = _out_proj(xp, ssm_p.reshape(n_p, SSM_WIDTH), att_p.reshape(n_p, ATTN_WIDTH),
                           w1, w2, row(p['ln2_g']), wq_b, row(p['ca_q_norm_g']), OUT_PROJ_TILE, BF16,
                           mem=(mkb.reshape(bsz, N_MEM, D_MODEL), mvb.reshape(bsz, N_MEM, D_MODEL)))

    x1_s, cq_s = _out_proj(xs, ssm_s.reshape(n_s, SSM_WIDTH), att_s.reshape(n_s, ATTN_WIDTH),
                           w1, w2, row(p['ln2_g']), wq_b, row(p['ca_q_norm_g']), n_s, F32)
    y_p, conv_p, co_s = _ffn(x1_p, co_p, wo_b, row(p['ln3_g']), wg_b, wvf_b, cw, cb, wd_b,
                             jnp.zeros((bsz, CONV_WIDTH - 1, FFN_HIDDEN), F32), FFN_TILE, 1,
                             rider=(cq_s.reshape(n_dec, t_new, D_MODEL),
                                    _mem_rows(cache_mem_k), _mem_rows(cache_mem_v)))

    tmaj = lambda a: a.reshape(n_dec, t_new, -1).transpose(1, 0, 2).reshape(n_s, -1)
    prev_s = state_conv.transpose(1, 0, 2).reshape(1, (CONV_WIDTH - 1) * n_dec, FFN_HIDDEN)
    y_s, conv_s = _ffn(tmaj(x1_s), tmaj(co_s), wo_b, row(p['ln3_g']), wg_b, wvf_b, cw, cb, wd_b,
                       prev_s, n_s, n_dec)
    y_s = y_s.reshape(t_new, n_dec, D_MODEL).transpose(1, 0, 2)
    conv_s = conv_s.reshape(CONV_WIDTH - 1, n_dec, FFN_HIDDEN).transpose(1, 0, 2)

    return (y_p.reshape(bsz, seq, D_MODEL), y_s,
            kt_p.reshape(bsz, ATTN_HEADS, 2, ATTN_HEAD_DIM, seq).transpose(0, 4, 1, 2, 3),
            v_p.reshape(bsz, seq, ATTN_HEADS, ATTN_V_DIM),
            k_s.reshape(n_dec, t_new, ATTN_HEADS, 2, ATTN_HEAD_DIM),
            v_s.reshape(n_dec, t_new, ATTN_HEADS, ATTN_V_DIM),
            _groups_to_state(hre_p, bsz), _groups_to_state(him_p, bsz),
            _groups_to_state(hre_s, n_dec), _groups_to_state(him_s, n_dec),
            conv_p, conv_s,
            mk.reshape(bsz, N_MEM, CA_HEADS, CA_HEAD_DIM), mv.reshape(bsz, N_MEM, CA_HEADS, CA_HEAD_DIM))


def kernel(x_prompt, x_sample, mem_prompt, cache_k, cache_v, page_table, state_ssm_re, state_ssm_im, state_conv, cache_mem_k, cache_mem_v, ln1_g, w_in, ssm_a_re, ssm_a_im, ssm_b_re, ssm_b_im, ssm_c_re, ssm_c_im, ssm_d, ssm_log_dt, ssm_glu_w, q_norm_g, k_norm_g, lam_q1, lam_k1, lam_q2, lam_k2, subln_g, w_out, ln2_g, mem_norm_g, ca_wq, ca_wk, ca_wv, ca_q_norm_g, ca_k_norm_g, ca_wo, ln3_g, ffn_wg, ffn_wv, ffn_conv_w, ffn_conv_b, ffn_wd):
    params = dict(
        ln1_g=ln1_g, w_in=w_in, ssm_a_re=ssm_a_re, ssm_a_im=ssm_a_im, ssm_b_re=ssm_b_re,
        ssm_b_im=ssm_b_im, ssm_c_re=ssm_c_re, ssm_c_im=ssm_c_im, ssm_d=ssm_d,
        ssm_log_dt=ssm_log_dt, ssm_glu_w=ssm_glu_w, q_norm_g=q_norm_g, k_norm_g=k_norm_g,
        lam_q1=lam_q1, lam_k1=lam_k1, lam_q2=lam_q2, lam_k2=lam_k2, subln_g=subln_g,
        w_out=w_out, ln2_g=ln2_g, mem_norm_g=mem_norm_g, ca_wq=ca_wq, ca_wk=ca_wk, ca_wv=ca_wv,
        ca_q_norm_g=ca_q_norm_g, ca_k_norm_g=ca_k_norm_g, ca_wo=ca_wo, ln3_g=ln3_g,
        ffn_wg=ffn_wg, ffn_wv=ffn_wv, ffn_conv_w=ffn_conv_w, ffn_conv_b=ffn_conv_b, ffn_wd=ffn_wd)
    y_p, y_s = x_prompt, x_sample
    per_layer = []
    for l in range(w_in.shape[0]):
        p = {k: v[l] for k, v in params.items()}
        outs = _layer(l, y_p, y_s, mem_prompt, cache_k[l], cache_v[l], page_table,
                      state_ssm_re[l], state_ssm_im[l], state_conv[l],
                      cache_mem_k[l], cache_mem_v[l], p)
        y_p, y_s = outs[0], outs[1]
        per_layer.append(outs[2:])
    return (y_p, y_s) + tuple(jnp.stack(leaf) for leaf in zip(*per_layer))
```
